```python
import jax, jax.numpy as jnp
from jax import lax
import numpy as np

D_MODEL = 1024
BATCH = 8
SEQ = 8192
DEPTH = 1

N_HEADS = 8
QK_NOPE_DIM = 64
QK_ROPE_DIM = 32
V_HEAD_DIM = 64
Q_LORA_RANK = 256
KV_LORA_RANK = 256
ROPE_THETA = 10000.0
ATTN_WIDTH = N_HEADS * V_HEAD_DIM
Q_BLOCK = 128
FOURIER_GROUPS = 8
FOURIER_GROUP_DIM = 64
FOURIER_WIDTH = FOURIER_GROUPS * FOURIER_GROUP_DIM
N_BRANCHES = 2
N_EXPERT_GROUPS = 4
EXPERTS_PER_GROUP = 8
N_EXPERTS = N_EXPERT_GROUPS * EXPERTS_PER_GROUP
TOP_K_INNER = 2
D_EXPERT = 256
N_ADA = 6
EPS = 1e-6

IN_SIZES = (Q_LORA_RANK, KV_LORA_RANK, QK_ROPE_DIM, FOURIER_WIDTH, N_BRANCHES * D_MODEL)
IN_SPLITS = tuple(int(v) for v in np.cumsum(IN_SIZES)[:-1])
IN_COLS = int(sum(IN_SIZES))

kernel_name = "hybrid_mla_fnet_hmoe_adaln_encoder"


def rmsnorm(x, g):
    xf = x.astype(jnp.float32)
    r = lax.rsqrt(jnp.mean(xf * xf, axis=-1, keepdims=True) + EPS)
    return (xf * r * g.astype(jnp.float32)).astype(x.dtype)


def modulate(h, shift, scale):
    return h * (1.0 + scale[:, None, :]) + shift[:, None, :]


def rope_angles(positions):
    inv_freq = ROPE_THETA ** (-jnp.arange(0, QK_ROPE_DIM, 2, dtype=jnp.float32) / QK_ROPE_DIM)
    ang = positions.astype(jnp.float32)[..., None] * inv_freq
    return jnp.cos(ang), jnp.sin(ang)


def apply_rope(x, cos, sin):
    half = QK_ROPE_DIM // 2
    x1, x2 = x[..., :half], x[..., half:]
    cos = cos.astype(x.dtype)
    sin = sin.astype(x.dtype)
    return jnp.concatenate([x1 * cos - x2 * sin, x2 * cos + x1 * sin], axis=-1)


def mla_attention(q_lat, kv_lat, k_rope, positions, g_q, g_kv, w_uq, w_uk, w_uv):
    B, S, _ = q_lat.shape
    q_lat = rmsnorm(q_lat, g_q)
    kv_lat = rmsnorm(kv_lat, g_kv)
    q = jnp.einsum('bsr,rhd->bshd', q_lat, w_uq)
    q_nope, q_rope = q[..., :QK_NOPE_DIM], q[..., QK_NOPE_DIM:]
    k_nope = jnp.einsum('bsr,rhd->bshd', kv_lat, w_uk)
    v = jnp.einsum('bsr,rhd->bshd', kv_lat, w_uv)
    cos, sin = rope_angles(positions)
    q_rope = apply_rope(q_rope, cos[:, :, None, :], sin[:, :, None, :])
    k_rope = apply_rope(k_rope, cos, sin)
    scale = (QK_NOPE_DIM + QK_ROPE_DIM) ** -0.5
    nb = S // Q_BLOCK
    qn_blk = (q_nope * scale).reshape(B, nb, Q_BLOCK, N_HEADS, QK_NOPE_DIM).transpose(1, 0, 2, 3, 4)
    qr_blk = (q_rope * scale).reshape(B, nb, Q_BLOCK, N_HEADS, QK_ROPE_DIM).transpose(1, 0, 2, 3, 4)

    def attend(blk):
        qn, qr = blk
        s = (jnp.einsum('bqhd,bkhd->bhqk', qn, k_nope)
             + jnp.einsum('bqhd,bkd->bhqk', qr, k_rope))
        p = jax.nn.softmax(s.astype(jnp.float32), axis=-1).astype(v.dtype)
        return jnp.einsum('bhqk,bkhd->bqhd', p, v)

    o = lax.map(attend, (qn_blk, qr_blk))
    return o.transpose(1, 0, 2, 3, 4).reshape(B, S, ATTN_WIDTH)


def fourier_mix(u):
    B, S, _ = u.shape
    ug = u.astype(jnp.float32).reshape(B, S, FOURIER_GROUPS, FOURIER_GROUP_DIM)
    f = jnp.fft.fft2(ug, axes=(1, 3), norm='ortho').real
    return f.reshape(B, S, FOURIER_WIDTH).astype(u.dtype)


def hierarchical_moe(h, w_rg, b_rg, w_re, b_re, w_gate, w_up, w_down):
    B, S, D = h.shape
    t = h.reshape(B * S, D)
    g_logits = (t @ w_rg).astype(jnp.float32) + b_rg.astype(jnp.float32)
    g_prob = jax.nn.softmax(g_logits, axis=-1)
    g_sel = jnp.argmax(g_prob, axis=-1)
    g_p = jnp.take_along_axis(g_prob, g_sel[:, None], axis=-1)[:, 0]
    e_logits = jnp.einsum('td,dge->tge', t, w_re).astype(jnp.float32) + b_re.astype(jnp.float32)
    e_logits = jnp.take_along_axis(e_logits, g_sel[:, None, None], axis=1)[:, 0]
    e_prob = jax.nn.softmax(e_logits, axis=-1)
    top_p, top_i = lax.top_k(e_prob, TOP_K_INNER)
    top_p = top_p / jnp.sum(top_p, axis=-1, keepdims=True)
    weights = g_p[:, None] * top_p
    global_idx = g_sel[:, None] * EXPERTS_PER_GROUP + top_i
    combine = jnp.sum(jax.nn.one_hot(global_idx, N_EXPERTS, dtype=jnp.float32)
                      * weights[..., None], axis=1).astype(t.dtype)
    out = jnp.zeros_like(t)
    for e in range(N_EXPERTS):
        hid = jax.nn.silu(t @ w_gate[e]) * (t @ w_up[e])
        out = out + combine[:, e:e + 1] * (hid @ w_down[e])
    return out.reshape(B, S, D)


def setup_inputs(seed: int = 0) -> dict:
    key = jax.random.key(seed)
    ks = jax.random.split(key, 32)
    f32 = jnp.float32
    L, D = DEPTH, D_MODEL

    def nrm(k, shape, fan_in, mult=1.0):
        return jax.random.normal(k, shape, f32) * (mult * fan_in ** -0.5)

    def gain(k, shape):
        return 1.0 + 0.02 * jax.random.normal(k, shape, f32)

    x = jax.random.normal(ks[0], (BATCH, SEQ, D), f32)
    c = jax.random.normal(ks[1], (BATCH, D), f32)
    positions = jnp.sort(jax.random.randint(ks[2], (BATCH, SEQ), 0, 4 * SEQ), axis=-1).astype(jnp.int32)
    return {
        "x": x,
        "c": c,
        "positions": positions,
        "w_ada": nrm(ks[3], (L, D, N_ADA * D), D, 0.5),
        "b_ada": 0.02 * jax.random.normal(ks[4], (L, N_ADA * D), f32),
        "g_norm_mix": gain(ks[5], (L, D)),
        "w_in": nrm(ks[6], (L, D, IN_COLS), D),
        "g_q_lat": gain(ks[7], (L, Q_LORA_RANK)),
        "g_kv_lat": gain(ks[8], (L, KV_LORA_RANK)),
        "w_uq": nrm(ks[9], (L, Q_LORA_RANK, N_HEADS, QK_NOPE_DIM + QK_ROPE_DIM), Q_LORA_RANK),
        "w_uk": nrm(ks[10], (L, KV_LORA_RANK, N_HEADS, QK_NOPE_DIM), KV_LORA_RANK),
        "w_uv": nrm(ks[11], (L, KV_LORA_RANK, N_HEADS, V_HEAD_DIM), KV_LORA_RANK),
        "w_branch_attn": nrm(ks[12], (L, ATTN_WIDTH, D), ATTN_WIDTH),
        "w_branch_fourier": nrm(ks[13], (L, FOURIER_WIDTH, D), FOURIER_WIDTH),
        "b_gates": 0.02 * jax.random.normal(ks[14], (L, N_BRANCHES * D), f32),
        "w_out": nrm(ks[15], (L, D, D), D),
        "g_norm_ffn": gain(ks[16], (L, D)),
        "w_router_group": nrm(ks[17], (L, D, N_EXPERT_GROUPS), D),
        "b_router_group": 0.01 * jax.random.normal(ks[18], (L, N_EXPERT_GROUPS), f32),
        "w_router_expert": nrm(ks[19], (L, D, N_EXPERT_GROUPS, EXPERTS_PER_GROUP), D),
        "b_router_expert": 0.01 * jax.random.normal(ks[20], (L, N_EXPERT_GROUPS, EXPERTS_PER_GROUP), f32),
        "w_expert_gate": nrm(ks[21], (L, N_EXPERTS, D, D_EXPERT), D),
        "w_expert_up": nrm(ks[22], (L, N_EXPERTS, D, D_EXPERT), D),
        "w_expert_down": nrm(ks[23], (L, N_EXPERTS, D_EXPERT, D), D_EXPERT),
        "w_ada_final": nrm(ks[24], (D, 2 * D), D, 0.5),
        "b_ada_final": 0.02 * jax.random.normal(ks[25], (2 * D,), f32),
        "g_norm_final": gain(ks[26], (D,)),
    }


def reference(x, c, positions, w_ada, b_ada, g_norm_mix, w_in, g_q_lat, g_kv_lat,
              w_uq, w_uk, w_uv, w_branch_attn, w_branch_fourier, b_gates, w_out,
              g_norm_ffn, w_router_group, b_router_group, w_router_expert, b_router_expert,
              w_expert_gate, w_expert_up, w_expert_down, w_ada_final, b_ada_final, g_norm_final):
    h = x
    c_act = jax.nn.silu(c)
    for l in range(DEPTH):
        mod = c_act @ w_ada[l] + b_ada[l]
        sh1, sc1, gt1, sh2, sc2, gt2 = jnp.split(mod, N_ADA, axis=-1)

        u = modulate(rmsnorm(h, g_norm_mix[l]), sh1, sc1)
        z = u @ w_in[l]
        q_lat, kv_lat, k_rope, f_in, gate_logits = jnp.split(z, IN_SPLITS, axis=-1)
        a = mla_attention(q_lat, kv_lat, k_rope, positions, g_q_lat[l], g_kv_lat[l],
                          w_uq[l], w_uk[l], w_uv[l]) @ w_branch_attn[l]
        fo = fourier_mix(f_in) @ w_branch_fourier[l]
        ga, gf = jnp.split(gate_logits + b_gates[l], N_BRANCHES, axis=-1)
        merged = jax.nn.sigmoid(ga) * a + jax.nn.sigmoid(gf) * fo
        h = h + gt1[:, None, :] * (merged @ w_out[l])

        v = modulate(rmsnorm(h, g_norm_ffn[l]), sh2, sc2)
        h = h + gt2[:, None, :] * hierarchical_moe(
            v, w_router_group[l], b_router_group[l], w_router_expert[l], b_router_expert[l],
            w_expert_gate[l], w_expert_up[l], w_expert_down[l])

    fmod = c_act @ w_ada_final + b_ada_final
    shf, scf = jnp.split(fmod, 2, axis=-1)
    return modulate(rmsnorm(h, g_norm_final), shf, scf)
```

```python
import functools

import numpy as np
import jax
import jax.numpy as jnp
from jax import lax
from jax.experimental import pallas as pl
from jax.experimental.pallas import tpu as pltpu

F32 = jnp.float32
BF16 = jnp.bfloat16

N_HEADS = 8
NOPE = 64
ROPE = 32
V_DIM = 64
HEAD_PAD = 128
Q_LORA = 256
KV_LORA = 256
ROPE_THETA = 10000.0
F_GROUPS = 8
F_GDIM = 64
F_WIDTH = F_GROUPS * F_GDIM
N_GROUPS = 4
E_PER_GROUP = 8
N_EXPERTS = N_GROUPS * E_PER_GROUP
D_EXPERT = 256
EPS = 1e-6
FFT_N2 = 64
LANES = 128
VMEM_LIMIT = 56 * 1024 * 1024


def _cparams(sem):
    return pltpu.CompilerParams(dimension_semantics=sem, vmem_limit_bytes=VMEM_LIMIT)


def _mods_kernel(c_ref, w_ref, b_ref, o_ref):
    c = c_ref[...]
    ca = c * jax.nn.sigmoid(c)
    o_ref[...] = jnp.dot(ca, w_ref[...], preferred_element_type=F32,
                         precision=lax.Precision.HIGHEST) + b_ref[...]


def _mods(c, w, b):
    bsz, d = c.shape
    n = w.shape[1]
    tn = 1024
    return pl.pallas_call(
        _mods_kernel,
        grid=(n // tn,),
        in_specs=[pl.BlockSpec((bsz, d), lambda j: (0, 0)),
                  pl.BlockSpec((d, tn), lambda j: (0, j)),
                  pl.BlockSpec((1, tn), lambda j: (0, j))],
        out_specs=pl.BlockSpec((bsz, tn), lambda j: (0, j)),
        out_shape=jax.ShapeDtypeStruct((bsz, n), F32),
        compiler_params=_cparams(("arbitrary",)),
        name="mods",
    )(c, w, b.reshape(1, n))


def _rms(x, g):
    r = lax.rsqrt(jnp.mean(x * x, axis=-1, keepdims=True) + EPS)
    return x * r * g


def _inproj_kernel(x_ref, pos_ref, sh_ref, sc_ref, gmix_ref, wcat_ref, gq_ref, gkv_ref,
                   wq_ref, wkv_ref, vones_ref, bd_ref, bg_ref, invf_ref,
                   q_ref, k_ref, v_ref, zr_ref, zi_ref, gate_ref, *, scale):
    d = x_ref.shape[-1]
    hp = N_HEADS * HEAD_PAD
    u = _rms(x_ref[0], gmix_ref[...]) * (1.0 + sc_ref[0]) + sh_ref[0]
    ub = u.astype(BF16)

    o_q, o_kv, o_kr, o_f, o_g = 0, Q_LORA, Q_LORA + KV_LORA, Q_LORA + KV_LORA + 2 * HEAD_PAD, \
        Q_LORA + KV_LORA + 2 * HEAD_PAD + F_WIDTH

    ang = pos_ref[0].astype(F32) * invf_ref[...]
    cos = jnp.cos(ang)
    sin = jnp.sin(ang)

    qn = _rms(jnp.dot(ub, wcat_ref[:, o_q:o_kv], preferred_element_type=F32), gq_ref[...]).astype(BF16)
    qab = jnp.dot(qn, wq_ref[...], preferred_element_type=F32)
    cq = cos * scale
    sq = sin * scale
    for h in range(N_HEADS):
        a = qab[:, h * HEAD_PAD:(h + 1) * HEAD_PAD]
        b = qab[:, hp + h * HEAD_PAD:hp + (h + 1) * HEAD_PAD]
        q_ref[0, h] = (a * cq + b * sq).astype(BF16)

    kvn = _rms(jnp.dot(ub, wcat_ref[:, o_kv:o_kr], preferred_element_type=F32), gkv_ref[...]).astype(BF16)
    kvab = jnp.dot(kvn, wkv_ref[...], preferred_element_type=F32)
    kr = jnp.dot(ub, wcat_ref[:, o_kr:o_f], preferred_element_type=F32)
    krr = kr[:, :HEAD_PAD] * cos + kr[:, HEAD_PAD:] * sin
    for h in range(N_HEADS):
        k_ref[0, h] = (kvab[:, h * HEAD_PAD:(h + 1) * HEAD_PAD] + krr).astype(BF16)
        v_ref[0, h] = (kvab[:, hp + h * HEAD_PAD:hp + (h + 1) * HEAD_PAD]
                       + vones_ref[:, h * HEAD_PAD:(h + 1) * HEAD_PAD]).astype(BF16)

    fin = jnp.dot(ub, wcat_ref[:, o_f:o_g], preferred_element_type=F32).astype(BF16)
    zz = jnp.dot(fin, bd_ref[...], preferred_element_type=F32)
    zr_ref[0] = zz[:, :F_WIDTH].astype(BF16)
    zi_ref[0] = zz[:, F_WIDTH:].astype(BF16)

    zg = jnp.dot(ub, wcat_ref[:, o_g:], preferred_element_type=F32) + bg_ref[...]
    gate_ref[0] = jax.nn.sigmoid(zg).astype(BF16)
    del d


def _inproj(x, pos3, sh1, sc1, gmix, wcat, gq, gkv, wq, wkv, vones, bd, bg, invf, tm):
    bsz, s, d = x.shape
    hp = N_HEADS * HEAD_PAD
    nt = s // tm
    const = lambda shape: pl.BlockSpec(shape, lambda b, i: (0,) * len(shape))
    tok = lambda w: pl.BlockSpec((1, tm, w), lambda b, i: (b, i, 0))
    per_b = pl.BlockSpec((1, 1, d), lambda b, i: (b, 0, 0))
    head = pl.BlockSpec((1, N_HEADS, tm, HEAD_PAD), lambda b, i: (b, 0, i, 0))
    scale = float((NOPE + ROPE) ** -0.5)
    return pl.pallas_call(
        functools.partial(_inproj_kernel, scale=scale),
        grid=(bsz, nt),
        in_specs=[tok(d), tok(1), per_b, per_b, const((1, d)), const(wcat.shape),
                  const((1, Q_LORA)), const((1, KV_LORA)), const(wq.shape), const(wkv.shape),
                  const((1, hp)), const(bd.shape), const((1, bg.shape[1])), const((1, HEAD_PAD))],
        out_specs=[head, head, head, tok(F_WIDTH), tok(F_WIDTH), tok(2 * d)],
        out_shape=[jax.ShapeDtypeStruct((bsz, N_HEADS, s, HEAD_PAD), BF16)] * 3
        + [jax.ShapeDtypeStruct((bsz, s, F_WIDTH), BF16)] * 2
        + [jax.ShapeDtypeStruct((bsz, s, 2 * d), BF16)],
        compiler_params=_cparams(("parallel", "parallel")),
        name="inproj",
    )(x, pos3, sh1, sc1, gmix, wcat, gq, gkv, wq, wkv, vones, bd, bg, invf)


def _attn_kernel(q_ref, k_ref, v_ref, o_ref, *, tk, nk):
    tq = q_ref.shape[2]
    lane = lax.broadcasted_iota(jnp.int32, (1, HEAD_PAD), 1)
    outs = []
    for hh in range(2):
        q = q_ref[0, hh]

        def body(c, carry, hh=hh, q=q):
            m, acc = carry
            off = pl.multiple_of(c * tk, tk)
            kc = k_ref[0, hh, pl.ds(off, tk), :]
            vc = v_ref[0, hh, pl.ds(off, tk), :]
            s = lax.dot_general(q, kc, (((1,), (1,)), ((), ())), preferred_element_type=F32)
            m_new = jnp.maximum(m, jnp.max(s, axis=-1, keepdims=True))
            alpha = jnp.exp(m - m_new)
            p = jnp.exp(s - m_new).astype(BF16)
            acc = alpha * acc + jnp.dot(p, vc, preferred_element_type=F32)
            return m_new, acc

        m0 = jnp.full((tq, 1), -jnp.inf, F32)
        acc0 = jnp.zeros((tq, HEAD_PAD), F32)
        _, acc = lax.fori_loop(0, nk, body, (m0, acc0))
        lcol = V_DIM if hh == 0 else 0
        outs.append(acc / acc[:, lcol:lcol + 1])
    o_ref[0] = jnp.where(lane < V_DIM, outs[0], outs[1]).astype(BF16)


def _attention(q, k, v, tq, tk):
    bsz, nh, s, _ = q.shape
    return pl.pallas_call(
        functools.partial(_attn_kernel, tk=tk, nk=s // tk),
        grid=(bsz, nh // 2, s // tq),
        in_specs=[pl.BlockSpec((1, 2, tq, HEAD_PAD), lambda b, j, i: (b, j, i, 0)),
                  pl.BlockSpec((1, 2, s, HEAD_PAD), lambda b, j, i: (b, j, 0, 0)),
                  pl.BlockSpec((1, 2, s, HEAD_PAD), lambda b, j, i: (b, j, 0, 0))],
        out_specs=pl.BlockSpec((1, tq, HEAD_PAD), lambda b, j, i: (b, i, j)),
        out_shape=jax.ShapeDtypeStruct((bsz, s, nh * V_DIM), BF16),
        compiler_params=_cparams(("parallel", "parallel", "arbitrary")),
        name="attention",
    )(q, k, v)


def _fft_a_kernel(zr_ref, zi_ref, m1_ref, br_ref, bi_ref):
    n1 = zr_ref.shape[1]
    ts2 = m1_ref.shape[0]
    for j in range(ts2):
        sl = slice(j * F_WIDTH, (j + 1) * F_WIDTH)
        xx = jnp.concatenate([zr_ref[0, :, sl], zi_ref[0, :, sl]], axis=0)
        y = jnp.dot(m1_ref[j], xx, preferred_element_type=F32)
        br_ref[0, :, sl] = y[:n1].astype(BF16)
        bi_ref[0, :, sl] = y[n1:].astype(BF16)


def _fft_a(zr, zi, m1, ts2):
    bsz, s, _ = zr.shape
    n1 = s // FFT_N2
    zr2 = zr.reshape(bsz, n1, FFT_N2 * F_WIDTH)
    zi2 = zi.reshape(bsz, n1, FFT_N2 * F_WIDTH)
    blk = pl.BlockSpec((1, n1, ts2 * F_WIDTH), lambda b, j: (b, 0, j))
    return pl.pallas_call(
        _fft_a_kernel,
        grid=(bsz, FFT_N2 // ts2),
        in_specs=[blk, blk, pl.BlockSpec((ts2, 2 * n1, 2 * n1), lambda b, j: (j, 0, 0))],
        out_specs=[blk, blk],
        out_shape=[jax.ShapeDtypeStruct(zr2.shape, BF16)] * 2,
        compiler_params=_cparams(("parallel", "parallel")),
        name="fft_a",
    )(zr2, zi2, m1)


def _fft_b_kernel(br_ref, bi_ref, wb_ref, o_ref):
    tk1 = br_ref.shape[1]
    for j in range(tk1):
        xx = jnp.concatenate([br_ref[0, j], bi_ref[0, j]], axis=0)
        y = jnp.dot(wb_ref[...], xx, preferred_element_type=F32)
        o_ref[0, :, j * F_WIDTH:(j + 1) * F_WIDTH] = y.astype(BF16)


def _fft_b(br, bi, wb, tk1):
    bsz, n1, _ = br.shape
    br4 = br.reshape(bsz, n1, FFT_N2, F_WIDTH)
    bi4 = bi.reshape(bsz, n1, FFT_N2, F_WIDTH)
    blk = pl.BlockSpec((1, tk1, FFT_N2, F_WIDTH), lambda b, j: (b, j, 0, 0))
    out = pl.pallas_call(
        _fft_b_kernel,
        grid=(bsz, n1 // tk1),
        in_specs=[blk, blk, pl.BlockSpec((FFT_N2, 2 * FFT_N2), lambda b, j: (0, 0))],
        out_specs=pl.BlockSpec((1, FFT_N2, tk1 * F_WIDTH), lambda b, j: (b, 0, j)),
        out_shape=jax.ShapeDtypeStruct((bsz, FFT_N2, n1 * F_WIDTH), BF16),
        compiler_params=_cparams(("parallel", "parallel")),
        name="fft_b",
    )(br4, bi4, wb)
    return out.reshape(bsz, n1 * FFT_N2, F_WIDTH)


def _dft_constants(s):
    n1, n2 = s // FFT_N2, FFT_N2
    m = np.arange(F_GDIM)
    th = 2.0 * np.pi * np.outer(m, m) / F_GDIM
    eye = np.eye(F_GROUPS)
    bd = np.concatenate([np.kron(eye, np.cos(th)), -np.kron(eye, np.sin(th))], axis=1) / np.sqrt(F_GDIM)
    k1 = np.arange(n1)[None, :, None]
    s1 = np.arange(n1)[None, None, :]
    s2 = np.arange(n2)[:, None, None]
    ph = 2.0 * np.pi * ((k1 * (n2 * s1 + s2)) % s) / s
    cr, ci = np.cos(ph) / np.sqrt(n1), -np.sin(ph) / np.sqrt(n1)
    m1 = np.concatenate([np.concatenate([cr, -ci], axis=2), np.concatenate([ci, cr], axis=2)], axis=1)
    k2 = np.arange(n2)
    tb = 2.0 * np.pi * np.outer(k2, k2) / n2
    wb = np.concatenate([np.cos(tb), np.sin(tb)], axis=1) / np.sqrt(n2)
    return (jnp.asarray(bd, BF16), jnp.asarray(m1, BF16), jnp.asarray(wb, BF16))


def _merge_kernel(at_ref, fo_ref, gate_ref, x_ref, wba_ref, wbf_ref, wout_ref, gt1_ref, sh2_ref,
                  sc2_ref, gffn_ref, wr_ref, br_ref, h1_ref, v2_ref, comb_ref):
    d = x_ref.shape[-1]
    a = jnp.dot(at_ref[0], wba_ref[...], preferred_element_type=F32)
    fo = jnp.dot(fo_ref[0], wbf_ref[...], preferred_element_type=F32)
    merged = gate_ref[0, :, :d].astype(F32) * a + gate_ref[0, :, d:].astype(F32) * fo
    y = jnp.dot(merged.astype(BF16), wout_ref[...], preferred_element_type=F32)
    h1 = x_ref[0] + gt1_ref[0] * y
    h1_ref[0] = h1
    v2 = _rms(h1, gffn_ref[...]) * (1.0 + sc2_ref[0]) + sh2_ref[0]
    v2_ref[0] = v2.astype(BF16)

    logit = jnp.dot(v2, wr_ref[...], preferred_element_type=F32,
                    precision=lax.Precision.HIGHEST) + br_ref[...]
    lane = lax.broadcasted_iota(jnp.int32, logit.shape, 1)
    neg = -jnp.inf
    big = jnp.int32(LANES)

    def softmax_masked(mask):
        z = jnp.where(mask, logit, neg)
        e = jnp.exp(z - jnp.max(z, axis=-1, keepdims=True))
        return e / jnp.sum(e, axis=-1, keepdims=True)

    def top1(p, mask):
        pm = jnp.where(mask, p, -1.0)
        best = jnp.max(pm, axis=-1, keepdims=True)
        idx = jnp.min(jnp.where(pm == best, lane, big), axis=-1, keepdims=True)
        return best, idx

    gmask = (lane >= N_EXPERTS) & (lane < N_EXPERTS + N_GROUPS)
    g_prob = softmax_masked(gmask)
    g_p, g_lane = top1(g_prob, gmask)
    e_lo = (g_lane - N_EXPERTS) * E_PER_GROUP
    emask = (lane >= e_lo) & (lane < e_lo + E_PER_GROUP)
    e_prob = softmax_masked(emask)
    p1, i1 = top1(e_prob, emask)
    p2, i2 = top1(e_prob, emask & (lane != i1))
    den = p1 + p2
    comb_ref[0] = jnp.where(lane == i1, g_p * (p1 / den), 0.0) + jnp.where(lane == i2, g_p * (p2 / den), 0.0)


def _merge(attn, four, gates, x, wba, wbf, wout, gt1, sh2, sc2, gffn, wr, br, tm):
    bsz, s, d = x.shape
    const = lambda shape: pl.BlockSpec(shape, lambda b, i: (0,) * len(shape))
    tok = lambda w: pl.BlockSpec((1, tm, w), lambda b, i: (b, i, 0))
    per_b = pl.BlockSpec((1, 1, d), lambda b, i: (b, 0, 0))
    return pl.pallas_call(
        _merge_kernel,
        grid=(bsz, s // tm),
        in_specs=[tok(attn.shape[-1]), tok(F_WIDTH), tok(2 * d), tok(d), const(wba.shape),
                  const(wbf.shape), const(wout.shape), per_b, per_b, per_b, const((1, d)),
                  const(wr.shape), const((1, LANES))],
        out_specs=[tok(d), tok(d), tok(LANES)],
        out_shape=[jax.ShapeDtypeStruct((bsz, s, d), F32), jax.ShapeDtypeStruct((bsz, s, d), BF16),
                   jax.ShapeDtypeStruct((bsz, s, LANES), F32)],
        compiler_params=_cparams(("parallel", "parallel")),
        name="merge",
    )(attn, four, gates, x, wba, wbf, wout, gt1, sh2, sc2, gffn, wr, br)


def _moe_kernel(v2_ref, comb_ref, wgu_ref, wd_ref, h1_ref, gt2_ref, shf_ref, scf_ref, gfin_ref,
                o_ref, acc_ref):
    e = pl.program_id(2)
    hc = jnp.dot(v2_ref[0], wgu_ref[0], preferred_element_type=F32)
    g = hc[:, :D_EXPERT]
    hid = g * jax.nn.sigmoid(g) * hc[:, D_EXPERT:]
    lane = lax.broadcasted_iota(jnp.int32, (1, LANES), 1)
    ce = jnp.sum(jnp.where(lane == e, comb_ref[0], 0.0), axis=-1, keepdims=True)
    contrib = jnp.dot((hid * ce).astype(BF16), wd_ref[0], preferred_element_type=F32)

    @pl.when(e == 0)
    def _():
        acc_ref[...] = contrib

    @pl.when(e > 0)
    def _():
        acc_ref[...] += contrib

    @pl.when(e == N_EXPERTS - 1)
    def _():
        h2 = h1_ref[0] + gt2_ref[0] * acc_ref[...]
        o_ref[0] = _rms(h2, gfin_ref[...]) * (1.0 + scf_ref[0]) + shf_ref[0]


def _moe(v2, comb, wgu, wd, h1, gt2, shf, scf, gfin, tm):
    bsz, s, d = h1.shape
    tok = lambda w: pl.BlockSpec((1, tm, w), lambda b, i, e: (b, i, 0))
    per_b = pl.BlockSpec((1, 1, d), lambda b, i, e: (b, 0, 0))
    return pl.pallas_call(
        _moe_kernel,
        grid=(bsz, s // tm, N_EXPERTS),
        in_specs=[tok(d), tok(LANES),
                  pl.BlockSpec((1, d, 2 * D_EXPERT), lambda b, i, e: (e, 0, 0)),
                  pl.BlockSpec((1, D_EXPERT, d), lambda b, i, e: (e, 0, 0)),
                  tok(d), per_b, per_b, per_b, pl.BlockSpec((1, d), lambda b, i, e: (0, 0))],
        out_specs=tok(d),
        out_shape=jax.ShapeDtypeStruct((bsz, s, d), F32),
        scratch_shapes=[pltpu.VMEM((tm, d), F32)],
        compiler_params=_cparams(("parallel", "parallel", "arbitrary")),
        name="moe",
    )(v2, comb, wgu, wd, h1, gt2, shf, scf, gfin)


def _rot_cols(w):
    half = ROPE // 2
    return jnp.concatenate([-w[..., half:], w[..., :half]], axis=-1)


def _pad_cols(w, lo, width):
    return jnp.pad(w, [(0, 0)] * (w.ndim - 1) + [(lo, width - lo - w.shape[-1])])


def _prep_weights(w_in, w_uq, w_uk, w_uv):
    o1, o2, o3, o4 = Q_LORA, Q_LORA + KV_LORA, Q_LORA + KV_LORA + ROPE, Q_LORA + KV_LORA + ROPE + F_WIDTH
    w_kr = w_in[:, o2:o3]
    wcat = jnp.concatenate([
        w_in[:, :o2],
        _pad_cols(w_kr, NOPE, HEAD_PAD), _pad_cols(_rot_cols(w_kr), NOPE, HEAD_PAD),
        w_in[:, o3:]], axis=1).astype(BF16)
    del o1, o4
    wq_a = _pad_cols(w_uq, 0, HEAD_PAD)
    wq_b = _pad_cols(_rot_cols(w_uq[..., NOPE:]), NOPE, HEAD_PAD)
    r = w_uq.shape[0]
    wq = jnp.concatenate([wq_a.reshape(r, -1), wq_b.reshape(r, -1)], axis=1).astype(BF16)
    wk = _pad_cols(w_uk, 0, HEAD_PAD).reshape(r, -1)
    odd = (jnp.arange(N_HEADS) % 2 == 1)[None, :, None]
    wv = jnp.where(odd, _pad_cols(w_uv, V_DIM, HEAD_PAD), _pad_cols(w_uv, 0, HEAD_PAD)).reshape(r, -1)
    wkv = jnp.concatenate([wk, wv], axis=1).astype(BF16)
    ones = np.zeros((N_HEADS, HEAD_PAD), np.float32)
    ones[0::2, V_DIM] = 1.0
    ones[1::2, 0] = 1.0
    return wcat, wq, wkv, jnp.asarray(ones.reshape(1, -1))


def kernel(x, c, positions, w_ada, b_ada, g_norm_mix, w_in, g_q_lat, g_kv_lat, w_uq, w_uk, w_uv, w_branch_attn, w_branch_fourier, b_gates, w_out, g_norm_ffn, w_router_group, b_router_group, w_router_expert, b_router_expert, w_expert_gate, w_expert_up, w_expert_down, w_ada_final, b_ada_final, g_norm_final):
    bsz, s, d = x.shape
    assert w_ada.shape[0] == 1 and s % FFT_N2 == 0
    tm = min(512, s)

    mod = _mods(c, w_ada[0], b_ada[0])
    fmod = _mods(c, w_ada_final, b_ada_final)
    sh1, sc1, gt1, sh2, sc2, gt2 = [m.reshape(bsz, 1, d) for m in jnp.split(mod, 6, axis=-1)]
    shf, scf = [m.reshape(bsz, 1, d) for m in jnp.split(fmod, 2, axis=-1)]

    wcat, wq, wkv, vones = _prep_weights(w_in[0], w_uq[0], w_uk[0], w_uv[0])
    bd, m1, wb = _dft_constants(s)
    inv_freq = ROPE_THETA ** (-jnp.arange(0, ROPE, 2, dtype=F32) / ROPE)
    invf = _pad_cols(jnp.concatenate([inv_freq, inv_freq])[None, :], NOPE, HEAD_PAD)

    q, k, v, zr, zi, gates = _inproj(
        x, positions.reshape(bsz, s, 1), sh1, sc1, g_norm_mix[0][None, :], wcat,
        g_q_lat[0][None, :], g_kv_lat[0][None, :], wq, wkv, vones, bd, b_gates[0][None, :], invf, tm)

    attn = _attention(q, k, v, tq=min(512, s), tk=min(512, s))
    br, bi = _fft_a(zr, zi, m1, ts2=8)
    four = _fft_b(br, bi, wb, tk1=16)

    wr = jnp.concatenate([w_router_expert[0].reshape(d, N_EXPERTS), w_router_group[0]], axis=1)
    wr = _pad_cols(wr, 0, LANES)
    brt = _pad_cols(jnp.concatenate([b_router_expert[0].reshape(-1), b_router_group[0]])[None, :], 0, LANES)
    h1, v2, comb = _merge(attn, four, gates, x, w_branch_attn[0].astype(BF16),
                          w_branch_fourier[0].astype(BF16), w_out[0].astype(BF16), gt1, sh2, sc2,
                          g_norm_ffn[0][None, :], wr, brt, tm)

    wgu = jnp.concatenate([w_expert_gate[0], w_expert_up[0]], axis=-1).astype(BF16)
    wd = w_expert_down[0].astype(BF16)
    return _moe(v2, comb, wgu, wd, h1, gt2, shf, scf, g_norm_final[None, :], tm=min(1024, s))
```

```python
import functools

import numpy as np
import jax
import jax.numpy as jnp
from jax import lax
from jax.experimental import pallas as pl
from jax.experimental.pallas import tpu as pltpu

F32 = jnp.float32
BF16 = jnp.bfloat16

N_HEADS = 8
NOPE = 64
ROPE = 32
V_DIM = 64
HEAD_PAD = 128
Q_LORA = 256
KV_LORA = 256
ROPE_THETA = 10000.0
F_GROUPS = 8
F_GDIM = 64
F_WIDTH = F_GROUPS * F_GDIM
N_GROUPS = 4
E_PER_GROUP = 8
N_EXPERTS = N_GROUPS * E_PER_GROUP
D_EXPERT = 256
EPS = 1e-6
FFT_N2 = 64
LANES = 128
QK_ROWS = 256
VMEM_LIMIT = 56 * 1024 * 1024


def _cparams(sem):
    return pltpu.CompilerParams(dimension_semantics=sem, vmem_limit_bytes=VMEM_LIMIT)


def _mods_kernel(c_ref, w_ref, b_ref, o_ref):
    c = c_ref[...]
    ca = c * jax.nn.sigmoid(c)
    o_ref[...] = jnp.dot(ca, w_ref[...], preferred_element_type=F32,
                         precision=lax.Precision.HIGHEST) + b_ref[...]


def _mods(c, w, b):
    bsz, d = c.shape
    n = w.shape[1]
    tn = 1024
    return pl.pallas_call(
        _mods_kernel,
        grid=(n // tn,),
        in_specs=[pl.BlockSpec((bsz, d), lambda j: (0, 0)),
                  pl.BlockSpec((d, tn), lambda j: (0, j)),
                  pl.BlockSpec((1, tn), lambda j: (0, j))],
        out_specs=pl.BlockSpec((bsz, tn), lambda j: (0, j)),
        out_shape=jax.ShapeDtypeStruct((bsz, n), F32),
        compiler_params=_cparams(("arbitrary",)),
        name="mods",
    )(c, w, b.reshape(1, n))


def _rms(x, g):
    r = lax.rsqrt(jnp.mean(x * x, axis=-1, keepdims=True) + EPS)
    return x * r * g


def _inproj_kernel(x_ref, pos_ref, sh_ref, sc_ref, gmix_ref, wcat_ref, gq_ref, gkv_ref,
                   wqt_ref, wk_ref, wvt_ref, vones_ref, bd_ref, bg_ref, invf_ref,
                   qt_ref, k_ref, vt_ref, zr_ref, zi_ref, gate_ref, *, scale):
    hp = N_HEADS * HEAD_PAD
    nt_dims = (((1,), (1,)), ((), ()))
    u = _rms(x_ref[0], gmix_ref[...]) * (1.0 + sc_ref[0]) + sh_ref[0]
    ub = u.astype(BF16)

    o_q, o_kv, o_kr, o_f, o_g = 0, Q_LORA, Q_LORA + KV_LORA, Q_LORA + KV_LORA + 2 * HEAD_PAD, \
        Q_LORA + KV_LORA + 2 * HEAD_PAD + F_WIDTH

    ang_t = invf_ref[...] * pos_ref[0].astype(F32)
    cos_t = jnp.cos(ang_t)
    sin_t = jnp.sin(ang_t)

    qn = _rms(jnp.dot(ub, wcat_ref[:, o_q:o_kv], preferred_element_type=F32), gq_ref[...]).astype(BF16)
    qab_t = lax.dot_general(wqt_ref[...], qn, nt_dims, preferred_element_type=F32)
    cq = cos_t * scale
    sq = sin_t * scale
    for h in range(N_HEADS):
        a = qab_t[h * HEAD_PAD:(h + 1) * HEAD_PAD]
        b = qab_t[hp + h * HEAD_PAD:hp + (h + 1) * HEAD_PAD]
        qt_ref[0, h] = (a * cq + b * sq).astype(BF16)

    kvn = _rms(jnp.dot(ub, wcat_ref[:, o_kv:o_kr], preferred_element_type=F32), gkv_ref[...]).astype(BF16)
    kn = jnp.dot(kvn, wk_ref[...], preferred_element_type=F32)
    kr = jnp.dot(ub, wcat_ref[:, o_kr:o_f], preferred_element_type=F32)
    krr = kr[:, :HEAD_PAD] * cos_t.T + kr[:, HEAD_PAD:] * sin_t.T
    vt = lax.dot_general(wvt_ref[...], kvn, nt_dims, preferred_element_type=F32) + vones_ref[...]
    for h in range(N_HEADS):
        k_ref[0, h] = (kn[:, h * HEAD_PAD:(h + 1) * HEAD_PAD] + krr).astype(BF16)
        vt_ref[0, h, 0] = vt[h * HEAD_PAD:(h + 1) * HEAD_PAD].astype(BF16)

    fin = jnp.dot(ub, wcat_ref[:, o_f:o_g], preferred_element_type=F32).astype(BF16)
    zz = jnp.dot(fin, bd_ref[...], preferred_element_type=F32)
    zr_ref[0] = zz[:, :F_WIDTH].astype(BF16)
    zi_ref[0] = zz[:, F_WIDTH:].astype(BF16)

    zg = jnp.dot(ub, wcat_ref[:, o_g:], preferred_element_type=F32) + bg_ref[...]
    gate_ref[0] = jax.nn.sigmoid(zg).astype(BF16)


def _inproj(x, pos_row, sh1, sc1, gmix, wcat, gq, gkv, wqt, wk, wvt, vones, bd, bg, invf, tm):
    bsz, s, d = x.shape
    hp = N_HEADS * HEAD_PAD
    nt = s // tm
    const = lambda shape: pl.BlockSpec(shape, lambda b, i: (0,) * len(shape))
    tok = lambda w: pl.BlockSpec((1, tm, w), lambda b, i: (b, i, 0))
    per_b = pl.BlockSpec((1, 1, d), lambda b, i: (b, 0, 0))
    scale = float((NOPE + ROPE) ** -0.5 * np.log2(np.e))
    return pl.pallas_call(
        functools.partial(_inproj_kernel, scale=scale),
        grid=(bsz, nt),
        in_specs=[tok(d), pl.BlockSpec((1, 1, tm), lambda b, i: (b, 0, i)), per_b, per_b,
                  const((1, d)), const(wcat.shape), const((1, Q_LORA)), const((1, KV_LORA)),
                  const(wqt.shape), const(wk.shape), const(wvt.shape), const((hp, 1)),
                  const(bd.shape), const((1, bg.shape[1])), const((HEAD_PAD, 1))],
        out_specs=[pl.BlockSpec((1, N_HEADS, HEAD_PAD, tm), lambda b, i: (b, 0, 0, i)),
                   pl.BlockSpec((1, N_HEADS, tm, HEAD_PAD), lambda b, i: (b, 0, i, 0)),
                   pl.BlockSpec((1, N_HEADS, 1, HEAD_PAD, tm), lambda b, i: (b, 0, i, 0, 0)),
                   tok(F_WIDTH), tok(F_WIDTH), tok(2 * d)],
        out_shape=[jax.ShapeDtypeStruct((bsz, N_HEADS, HEAD_PAD, s), BF16),
                   jax.ShapeDtypeStruct((bsz, N_HEADS, s, HEAD_PAD), BF16),
                   jax.ShapeDtypeStruct((bsz, N_HEADS, nt, HEAD_PAD, tm), BF16),
                   jax.ShapeDtypeStruct((bsz, s, F_WIDTH), BF16),
                   jax.ShapeDtypeStruct((bsz, s, F_WIDTH), BF16),
                   jax.ShapeDtypeStruct((bsz, s, 2 * d), BF16)],
        compiler_params=_cparams(("parallel", "parallel")),
        name="inproj",
    )(x, pos_row, sh1, sc1, gmix, wcat, gq, gkv, wqt, wk, wvt, vones, bd, bg, invf)


def _attn_kernel(qt_ref, k_ref, vt_ref, o_ref, pa0_ref, pa1_ref, pb0_ref, pb1_ref, *, nk):
    tq = qt_ref.shape[3]
    tk = vt_ref.shape[4]
    qts = [qt_ref[0, hh] for hh in range(2)]
    pa_refs = (pa0_ref, pa1_ref)
    pb_refs = (pb0_ref, pb1_ref)

    def scores(c, hh):
        ss = []
        for i in range(tk // QK_ROWS):
            off = c * tk + i * QK_ROWS
            ss.append(jnp.dot(k_ref[0, hh, off:off + QK_ROWS, :], qts[hh],
                              preferred_element_type=F32))
        return ss

    def probs(ss, m, p_ref):
        m_new = m
        for s in ss:
            m_new = jnp.maximum(m_new, jnp.max(s, axis=0, keepdims=True))
        for i, s in enumerate(ss):
            p_ref[i * QK_ROWS:(i + 1) * QK_ROWS, :] = jnp.exp2(s - m_new).astype(BF16)
        return m_new, jnp.exp2(m - m_new)

    def accumulate(c, hh, acc, alpha, p_ref):
        return alpha * acc + jnp.dot(vt_ref[0, hh, c], p_ref[...], preferred_element_type=F32)

    hs = range(2)
    m = [jnp.full((1, tq), -jnp.inf, F32) for _ in hs]
    alpha = [None for _ in hs]
    accs = [jnp.zeros((HEAD_PAD, tq), F32) for _ in hs]
    for c in range(nk):
        cur, prev = (pa_refs, pb_refs) if c % 2 == 0 else (pb_refs, pa_refs)
        s = [scores(c, hh) for hh in hs]
        if c > 0:
            for hh in hs:
                accs[hh] = accumulate(c - 1, hh, accs[hh], alpha[hh], prev[hh])
        for hh in hs:
            m[hh], alpha[hh] = probs(s[hh], m[hh], cur[hh])
    last = pa_refs if (nk - 1) % 2 == 0 else pb_refs
    accs = [accumulate(nk - 1, hh, accs[hh], alpha[hh], last[hh]) for hh in hs]
    row = lax.broadcasted_iota(jnp.int32, (HEAD_PAD, 1), 0)
    out_t = jnp.where(row < V_DIM, accs[0] / accs[0][V_DIM:V_DIM + 1], accs[1] / accs[1][0:1])
    o_ref[0] = out_t.T.astype(BF16)


def _attention(qt, k, vt, tq):
    bsz, nh, s, _ = k.shape
    nk, tk = vt.shape[2], vt.shape[4]
    return pl.pallas_call(
        functools.partial(_attn_kernel, nk=nk),
        grid=(bsz, nh // 2, s // tq),
        in_specs=[pl.BlockSpec((1, 2, HEAD_PAD, tq), lambda b, j, i: (b, j, 0, i)),
                  pl.BlockSpec((1, 2, s, HEAD_PAD), lambda b, j, i: (b, j, 0, 0)),
                  pl.BlockSpec((1, 2, nk, HEAD_PAD, tk), lambda b, j, i: (b, j, 0, 0, 0))],
        out_specs=pl.BlockSpec((1, tq, HEAD_PAD), lambda b, j, i: (b, i, j)),
        out_shape=jax.ShapeDtypeStruct((bsz, s, nh * V_DIM), BF16),
        scratch_shapes=[pltpu.VMEM((tk, tq), BF16)] * 4,
        compiler_params=_cparams(("parallel", "parallel", "arbitrary")),
        name="attention",
    )(qt, k, vt)


def _fft_a_kernel(zr_ref, zi_ref, m1_ref, br_ref, bi_ref):
    n1 = zr_ref.shape[1]
    ts2 = m1_ref.shape[0]
    for j in range(ts2):
        sl = slice(j * F_WIDTH, (j + 1) * F_WIDTH)
        xx = jnp.concatenate([zr_ref[0, :, sl], zi_ref[0, :, sl]], axis=0)
        y = jnp.dot(m1_ref[j], xx, preferred_element_type=F32)
        br_ref[0, :, sl] = y[:n1].astype(BF16)
        bi_ref[0, :, sl] = y[n1:].astype(BF16)


def _fft_a(zr, zi, m1, ts2):
    bsz, s, _ = zr.shape
    n1 = s // FFT_N2
    zr2 = zr.reshape(bsz, n1, FFT_N2 * F_WIDTH)
    zi2 = zi.reshape(bsz, n1, FFT_N2 * F_WIDTH)
    blk = pl.BlockSpec((1, n1, ts2 * F_WIDTH), lambda b, j: (b, 0, j))
    return pl.pallas_call(
        _fft_a_kernel,
        grid=(bsz, FFT_N2 // ts2),
        in_specs=[blk, blk, pl.BlockSpec((ts2, 2 * n1, 2 * n1), lambda b, j: (j, 0, 0))],
        out_specs=[blk, blk],
        out_shape=[jax.ShapeDtypeStruct(zr2.shape, BF16)] * 2,
        compiler_params=_cparams(("parallel", "parallel")),
        name="fft_a",
    )(zr2, zi2, m1)


def _fft_b_kernel(br_ref, bi_ref, wb_ref, o_ref):
    tk1 = br_ref.shape[1]
    for j in range(tk1):
        xx = jnp.concatenate([br_ref[0, j], bi_ref[0, j]], axis=0)
        y = jnp.dot(wb_ref[...], xx, preferred_element_type=F32)
        o_ref[0, :, j * F_WIDTH:(j + 1) * F_WIDTH] = y.astype(BF16)


def _fft_b(br, bi, wb, tk1):
    bsz, n1, _ = br.shape
    br4 = br.reshape(bsz, n1, FFT_N2, F_WIDTH)
    bi4 = bi.reshape(bsz, n1, FFT_N2, F_WIDTH)
    blk = pl.BlockSpec((1, tk1, FFT_N2, F_WIDTH), lambda b, j: (b, j, 0, 0))
    out = pl.pallas_call(
        _fft_b_kernel,
        grid=(bsz, n1 // tk1),
        in_specs=[blk, blk, pl.BlockSpec((FFT_N2, 2 * FFT_N2), lambda b, j: (0, 0))],
        out_specs=pl.BlockSpec((1, FFT_N2, tk1 * F_WIDTH), lambda b, j: (b, 0, j)),
        out_shape=jax.ShapeDtypeStruct((bsz, FFT_N2, n1 * F_WIDTH), BF16),
        compiler_params=_cparams(("parallel", "parallel")),
        name="fft_b",
    )(br4, bi4, wb)
    return out.reshape(bsz, n1 * FFT_N2, F_WIDTH)


def _dft_constants(s):
    n1, n2 = s // FFT_N2, FFT_N2
    m = np.arange(F_GDIM)
    th = 2.0 * np.pi * np.outer(m, m) / F_GDIM
    eye = np.eye(F_GROUPS)
    bd = np.concatenate([np.kron(eye, np.cos(th)), -np.kron(eye, np.sin(th))], axis=1) / np.sqrt(F_GDIM)
    k1 = np.arange(n1)[None, :, None]
    s1 = np.arange(n1)[None, None, :]
    s2 = np.arange(n2)[:, None, None]
    ph = 2.0 * np.pi * ((k1 * (n2 * s1 + s2)) % s) / s
    cr, ci = np.cos(ph) / np.sqrt(n1), -np.sin(ph) / np.sqrt(n1)
    m1 = np.concatenate([np.concatenate([cr, -ci], axis=2), np.concatenate([ci, cr], axis=2)], axis=1)
    k2 = np.arange(n2)
    tb = 2.0 * np.pi * np.outer(k2, k2) / n2
    wb = np.concatenate([np.cos(tb), np.sin(tb)], axis=1) / np.sqrt(n2)
    return (jnp.asarray(bd, BF16), jnp.asarray(m1, BF16), jnp.asarray(wb, BF16))


def _merge_kernel(at_ref, fo_ref, gate_ref, x_ref, wba_ref, wbf_ref, wout_ref, gt1_ref, sh2_ref,
                  sc2_ref, gffn_ref, wr_ref, br_ref, h1_ref, v2_ref, comb_ref):
    d = x_ref.shape[-1]
    a = jnp.dot(at_ref[0], wba_ref[...], preferred_element_type=F32)
    fo = jnp.dot(fo_ref[0], wbf_ref[...], preferred_element_type=F32)
    merged = gate_ref[0, :, :d].astype(F32) * a + gate_ref[0, :, d:].astype(F32) * fo
    y = jnp.dot(merged.astype(BF16), wout_ref[...], preferred_element_type=F32)
    h1 = x_ref[0] + gt1_ref[0] * y
    h1_ref[0] = h1
    v2 = _rms(h1, gffn_ref[...]) * (1.0 + sc2_ref[0]) + sh2_ref[0]
    v2_ref[0] = v2.astype(BF16)

    logit = jnp.dot(v2, wr_ref[...], preferred_element_type=F32,
                    precision=lax.Precision.HIGHEST) + br_ref[...]
    lane = lax.broadcasted_iota(jnp.int32, logit.shape, 1)
    neg = -jnp.inf
    big = jnp.int32(LANES)

    def softmax_masked(mask):
        z = jnp.where(mask, logit, neg)
        e = jnp.exp(z - jnp.max(z, axis=-1, keepdims=True))
        return e / jnp.sum(e, axis=-1, keepdims=True)

    def top1(p, mask):
        pm = jnp.where(mask, p, -1.0)
        best = jnp.max(pm, axis=-1, keepdims=True)
        idx = jnp.min(jnp.where(pm == best, lane, big), axis=-1, keepdims=True)
        return best, idx

    gmask = (lane >= N_EXPERTS) & (lane < N_EXPERTS + N_GROUPS)
    g_prob = softmax_masked(gmask)
    g_p, g_lane = top1(g_prob, gmask)
    e_lo = (g_lane - N_EXPERTS) * E_PER_GROUP
    emask = (lane >= e_lo) & (lane < e_lo + E_PER_GROUP)
    e_prob = softmax_masked(emask)
    p1, i1 = top1(e_prob, emask)
    p2, i2 = top1(e_prob, emask & (lane != i1))
    den = p1 + p2
    comb_ref[0] = jnp.where(lane == i1, g_p * (p1 / den), 0.0) + jnp.where(lane == i2, g_p * (p2 / den), 0.0)


def _merge(attn, four, gates, x, wba, wbf, wout, gt1, sh2, sc2, gffn, wr, br, tm):
    bsz, s, d = x.shape
    const = lambda shape: pl.BlockSpec(shape, lambda b, i: (0,) * len(shape))
    tok = lambda w: pl.BlockSpec((1, tm, w), lambda b, i: (b, i, 0))
    per_b = pl.BlockSpec((1, 1, d), lambda b, i: (b, 0, 0))
    return pl.pallas_call(
        _merge_kernel,
        grid=(bsz, s // tm),
        in_specs=[tok(attn.shape[-1]), tok(F_WIDTH), tok(2 * d), tok(d), const(wba.shape),
                  const(wbf.shape), const(wout.shape), per_b, per_b, per_b, const((1, d)),
                  const(wr.shape), const((1, LANES))],
        out_specs=[tok(d), tok(d), tok(LANES)],
        out_shape=[jax.ShapeDtypeStruct((bsz, s, d), F32), jax.ShapeDtypeStruct((bsz, s, d), BF16),
                   jax.ShapeDtypeStruct((bsz, s, LANES), F32)],
        compiler_params=_cparams(("parallel", "parallel")),
        name="merge",
    )(attn, four, gates, x, wba, wbf, wout, gt1, sh2, sc2, gffn, wr, br)


def _moe_kernel(v2_ref, comb_ref, wgu_ref, wd_ref, h1_ref, gt2_ref, shf_ref, scf_ref, gfin_ref,
                o_ref, acc_ref):
    e = pl.program_id(2)
    hc = jnp.dot(v2_ref[0], wgu_ref[0], preferred_element_type=F32)
    g = hc[:, :D_EXPERT]
    hid = g * jax.nn.sigmoid(g) * hc[:, D_EXPERT:]
    lane = lax.broadcasted_iota(jnp.int32, (1, LANES), 1)
    ce = jnp.sum(jnp.where(lane == e, comb_ref[0], 0.0), axis=-1, keepdims=True)
    contrib = jnp.dot((hid * ce).astype(BF16), wd_ref[0], preferred_element_type=F32)

    @pl.when(e == 0)
    def _():
        acc_ref[...] = contrib

    @pl.when(e > 0)
    def _():
        acc_ref[...] += contrib

    @pl.when(e == N_EXPERTS - 1)
    def _():
        h2 = h1_ref[0] + gt2_ref[0] * acc_ref[...]
        o_ref[0] = _rms(h2, gfin_ref[...]) * (1.0 + scf_ref[0]) + shf_ref[0]


def _moe(v2, comb, wgu, wd, h1, gt2, shf, scf, gfin, tm):
    bsz, s, d = h1.shape
    tok = lambda w: pl.BlockSpec((1, tm, w), lambda b, i, e: (b, i, 0))
    per_b = pl.BlockSpec((1, 1, d), lambda b, i, e: (b, 0, 0))
    return pl.pallas_call(
        _moe_kernel,
        grid=(bsz, s // tm, N_EXPERTS),
        in_specs=[tok(d), tok(LANES),
                  pl.BlockSpec((1, d, 2 * D_EXPERT), lambda b, i, e: (e, 0, 0)),
                  pl.BlockSpec((1, D_EXPERT, d), lambda b, i, e: (e, 0, 0)),
                  tok(d), per_b, per_b, per_b, pl.BlockSpec((1, d), lambda b, i, e: (0, 0))],
        out_specs=tok(d),
        out_shape=jax.ShapeDtypeStruct((bsz, s, d), F32),
        scratch_shapes=[pltpu.VMEM((tm, d), F32)],
        compiler_params=_cparams(("parallel", "parallel", "arbitrary")),
        name="moe",
    )(v2, comb, wgu, wd, h1, gt2, shf, scf, gfin)


def _rot_cols(w):
    half = ROPE // 2
    return jnp.concatenate([-w[..., half:], w[..., :half]], axis=-1)


def _pad_cols(w, lo, width):
    return jnp.pad(w, [(0, 0)] * (w.ndim - 1) + [(lo, width - lo - w.shape[-1])])


def _prep_weights(w_in, w_uq, w_uk, w_uv):
    o2, o3 = Q_LORA + KV_LORA, Q_LORA + KV_LORA + ROPE
    w_kr = w_in[:, o2:o3]
    wcat = jnp.concatenate([
        w_in[:, :o2],
        _pad_cols(w_kr, NOPE, HEAD_PAD), _pad_cols(_rot_cols(w_kr), NOPE, HEAD_PAD),
        w_in[:, o3:]], axis=1).astype(BF16)
    wq_a = _pad_cols(w_uq, 0, HEAD_PAD)
    wq_b = _pad_cols(_rot_cols(w_uq[..., NOPE:]), NOPE, HEAD_PAD)
    r = w_uq.shape[0]
    wqt = jnp.concatenate([wq_a.reshape(r, -1), wq_b.reshape(r, -1)], axis=1).T.astype(BF16)
    wk = _pad_cols(w_uk, 0, HEAD_PAD).reshape(r, -1).astype(BF16)
    odd = (jnp.arange(N_HEADS) % 2 == 1)[None, :, None]
    wv = jnp.where(odd, _pad_cols(w_uv, V_DIM, HEAD_PAD), _pad_cols(w_uv, 0, HEAD_PAD)).reshape(r, -1)
    wvt = wv.T.astype(BF16)
    ones = np.zeros((N_HEADS, HEAD_PAD), np.float32)
    ones[0::2, V_DIM] = 1.0
    ones[1::2, 0] = 1.0
    return wcat, wqt, wk, wvt, jnp.asarray(ones.reshape(-1, 1))


def kernel(x, c, positions, w_ada, b_ada, g_norm_mix, w_in, g_q_lat, g_kv_lat, w_uq, w_uk, w_uv, w_branch_attn, w_branch_fourier, b_gates, w_out, g_norm_ffn, w_router_group, b_router_group, w_router_expert, b_router_expert, w_expert_gate, w_expert_up, w_expert_down, w_ada_final, b_ada_final, g_norm_final):
    bsz, s, d = x.shape
    assert w_ada.shape[0] == 1 and s % FFT_N2 == 0
    tm = min(512, s)

    mod = _mods(c, w_ada[0], b_ada[0])
    fmod = _mods(c, w_ada_final, b_ada_final)
    sh1, sc1, gt1, sh2, sc2, gt2 = [m.reshape(bsz, 1, d) for m in jnp.split(mod, 6, axis=-1)]
    shf, scf = [m.reshape(bsz, 1, d) for m in jnp.split(fmod, 2, axis=-1)]

    wcat, wqt, wk, wvt, vones = _prep_weights(w_in[0], w_uq[0], w_uk[0], w_uv[0])
    bd, m1, wb = _dft_constants(s)
    inv_freq = ROPE_THETA ** (-jnp.arange(0, ROPE, 2, dtype=F32) / ROPE)
    invf = _pad_cols(jnp.concatenate([inv_freq, inv_freq])[None, :], NOPE, HEAD_PAD).reshape(HEAD_PAD, 1)

    qt, k, vt, zr, zi, gates = _inproj(
        x, positions.reshape(bsz, 1, s), sh1, sc1, g_norm_mix[0][None, :], wcat,
        g_q_lat[0][None, :], g_kv_lat[0][None, :], wqt, wk, wvt, vones, bd, b_gates[0][None, :],
        invf, tm)

    attn = _attention(qt, k, vt, tq=min(256, s))
    br, bi = _fft_a(zr, zi, m1, ts2=8)
    four = _fft_b(br, bi, wb, tk1=16)

    wr = jnp.concatenate([w_router_expert[0].reshape(d, N_EXPERTS), w_router_group[0]], axis=1)
    wr = _pad_cols(wr, 0, LANES)
    brt = _pad_cols(jnp.concatenate([b_router_expert[0].reshape(-1), b_router_group[0]])[None, :], 0, LANES)
    h1, v2, comb = _merge(attn, four, gates, x, w_branch_attn[0].astype(BF16),
                          w_branch_fourier[0].astype(BF16), w_out[0].astype(BF16), gt1, sh2, sc2,
                          g_norm_ffn[0][None, :], wr, brt, tm)

    wgu = jnp.concatenate([w_expert_gate[0], w_expert_up[0]], axis=-1).astype(BF16)
    wd = w_expert_down[0].astype(BF16)
    return _moe(v2, comb, wgu, wd, h1, gt2, shf, scf, g_norm_final[None, :], tm=min(1024, s))
```

```python
import functools

import numpy as np
import jax
import jax.numpy as jnp
from jax import lax
from jax.experimental import pallas as pl
from jax.experimental.pallas import tpu as pltpu

F32 = jnp.float32
BF16 = jnp.bfloat16

N_HEADS = 8
NOPE = 64
ROPE = 32
V_DIM = 64
HEAD_PAD = 128
Q_LORA = 256
KV_LORA = 256
ROPE_THETA = 10000.0
F_GROUPS = 8
F_GDIM = 64
F_WIDTH = F_GROUPS * F_GDIM
N_GROUPS = 4
E_PER_GROUP = 8
N_EXPERTS = N_GROUPS * E_PER_GROUP
D_EXPERT = 256
EPS = 1e-6
FFT_N2 = 64
LANES = 128
QK_ROWS = 256
VMEM_LIMIT = 56 * 1024 * 1024


def _cparams(sem):
    return pltpu.CompilerParams(dimension_semantics=sem, vmem_limit_bytes=VMEM_LIMIT)


def _mods_kernel(c_ref, w_ref, b_ref, o_ref):
    c = c_ref[...]
    ca = c * jax.nn.sigmoid(c)
    o_ref[...] = jnp.dot(ca, w_ref[...], preferred_element_type=F32,
                         precision=lax.Precision.HIGHEST) + b_ref[...]


def _mods(c, w, b):
    bsz, d = c.shape
    n = w.shape[1]
    tn = 1024
    return pl.pallas_call(
        _mods_kernel,
        grid=(n // tn,),
        in_specs=[pl.BlockSpec((bsz, d), lambda j: (0, 0)),
                  pl.BlockSpec((d, tn), lambda j: (0, j)),
                  pl.BlockSpec((1, tn), lambda j: (0, j))],
        out_specs=pl.BlockSpec((bsz, tn), lambda j: (0, j)),
        out_shape=jax.ShapeDtypeStruct((bsz, n), F32),
        compiler_params=_cparams(("arbitrary",)),
        name="mods",
    )(c, w, b.reshape(1, n))


def _rms(x, g):
    r = lax.rsqrt(jnp.mean(x * x, axis=-1, keepdims=True) + EPS)
    return x * r * g


def _inproj_kernel(x_ref, pos_ref, sh_ref, sc_ref, gmix_ref, wcat_ref, gq_ref, gkv_ref,
                   wqt_ref, wk_ref, wvt_ref, vones_ref, bd_ref, bg_ref, invf_ref,
                   qt_ref, k_ref, vt_ref, zr_ref, zi_ref, gate_ref, *, scale):
    hp = N_HEADS * HEAD_PAD
    nt_dims = (((1,), (1,)), ((), ()))
    u = _rms(x_ref[0], gmix_ref[...]) * (1.0 + sc_ref[0]) + sh_ref[0]
    ub = u.astype(BF16)

    o_q, o_kv, o_kr, o_f, o_g = 0, Q_LORA, Q_LORA + KV_LORA, Q_LORA + KV_LORA + 2 * HEAD_PAD, \
        Q_LORA + KV_LORA + 2 * HEAD_PAD + F_WIDTH

    ang_t = invf_ref[...] * pos_ref[0].astype(F32)
    cos_t = jnp.cos(ang_t)
    sin_t = jnp.sin(ang_t)

    qn = _rms(jnp.dot(ub, wcat_ref[:, o_q:o_kv], preferred_element_type=F32), gq_ref[...]).astype(BF16)
    qab_t = lax.dot_general(wqt_ref[...], qn, nt_dims, preferred_element_type=F32)
    cq = cos_t * scale
    sq = sin_t * scale
    for h in range(N_HEADS):
        a = qab_t[h * HEAD_PAD:(h + 1) * HEAD_PAD]
        b = qab_t[hp + h * HEAD_PAD:hp + (h + 1) * HEAD_PAD]
        qt_ref[0, h] = (a * cq + b * sq).astype(BF16)

    kvn = _rms(jnp.dot(ub, wcat_ref[:, o_kv:o_kr], preferred_element_type=F32), gkv_ref[...]).astype(BF16)
    kn = jnp.dot(kvn, wk_ref[...], preferred_element_type=F32)
    kr = jnp.dot(ub, wcat_ref[:, o_kr:o_f], preferred_element_type=F32)
    krr = kr[:, :HEAD_PAD] * cos_t.T + kr[:, HEAD_PAD:] * sin_t.T
    vt = lax.dot_general(wvt_ref[...], kvn, nt_dims, preferred_element_type=F32) + vones_ref[...]
    for h in range(N_HEADS):
        k_ref[0, h] = (kn[:, h * HEAD_PAD:(h + 1) * HEAD_PAD] + krr).astype(BF16)
        vt_ref[0, h, 0] = vt[h * HEAD_PAD:(h + 1) * HEAD_PAD].astype(BF16)

    fin = jnp.dot(ub, wcat_ref[:, o_f:o_g], preferred_element_type=F32).astype(BF16)
    zz = jnp.dot(fin, bd_ref[...], preferred_element_type=F32)
    zr_ref[0] = zz[:, :F_WIDTH].astype(BF16)
    zi_ref[0] = zz[:, F_WIDTH:].astype(BF16)

    zg = jnp.dot(ub, wcat_ref[:, o_g:], preferred_element_type=F32) + bg_ref[...]
    gate_ref[0] = jax.nn.sigmoid(zg).astype(BF16)


def _inproj(x, pos_row, sh1, sc1, gmix, wcat, gq, gkv, wqt, wk, wvt, vones, bd, bg, invf, tm):
    bsz, s, d = x.shape
    hp = N_HEADS * HEAD_PAD
    nt = s // tm
    const = lambda shape: pl.BlockSpec(shape, lambda b, i: (0,) * len(shape))
    tok = lambda w: pl.BlockSpec((1, tm, w), lambda b, i: (b, i, 0))
    per_b = pl.BlockSpec((1, 1, d), lambda b, i: (b, 0, 0))
    scale = float((NOPE + ROPE) ** -0.5 * np.log2(np.e))
    return pl.pallas_call(
        functools.partial(_inproj_kernel, scale=scale),
        grid=(bsz, nt),
        in_specs=[tok(d), pl.BlockSpec((1, 1, tm), lambda b, i: (b, 0, i)), per_b, per_b,
                  const((1, d)), const(wcat.shape), const((1, Q_LORA)), const((1, KV_LORA)),
                  const(wqt.shape), const(wk.shape), const(wvt.shape), const((hp, 1)),
                  const(bd.shape), const((1, bg.shape[1])), const((HEAD_PAD, 1))],
        out_specs=[pl.BlockSpec((1, N_HEADS, HEAD_PAD, tm), lambda b, i: (b, 0, 0, i)),
                   pl.BlockSpec((1, N_HEADS, tm, HEAD_PAD), lambda b, i: (b, 0, i, 0)),
                   pl.BlockSpec((1, N_HEADS, 1, HEAD_PAD, tm), lambda b, i: (b, 0, i, 0, 0)),
                   tok(F_WIDTH), tok(F_WIDTH), tok(2 * d)],
        out_shape=[jax.ShapeDtypeStruct((bsz, N_HEADS, HEAD_PAD, s), BF16),
                   jax.ShapeDtypeStruct((bsz, N_HEADS, s, HEAD_PAD), BF16),
                   jax.ShapeDtypeStruct((bsz, N_HEADS, nt, HEAD_PAD, tm), BF16),
                   jax.ShapeDtypeStruct((bsz, s, F_WIDTH), BF16),
                   jax.ShapeDtypeStruct((bsz, s, F_WIDTH), BF16),
                   jax.ShapeDtypeStruct((bsz, s, 2 * d), BF16)],
        compiler_params=_cparams(("parallel", "parallel")),
        name="inproj",
    )(x, pos_row, sh1, sc1, gmix, wcat, gq, gkv, wqt, wk, wvt, vones, bd, bg, invf)


def _attn_kernel(qt_ref, k_ref, vt_ref, o_ref, pa0_ref, pa1_ref, pb0_ref, pb1_ref, *, nk):
    tq = qt_ref.shape[3]
    tk = vt_ref.shape[4]
    qts = [qt_ref[0, hh] for hh in range(2)]
    pa_refs = (pa0_ref, pa1_ref)
    pb_refs = (pb0_ref, pb1_ref)

    def scores(c, hh):
        ss = []
        for i in range(tk // QK_ROWS):
            off = c * tk + i * QK_ROWS
            ss.append(jnp.dot(k_ref[0, hh, off:off + QK_ROWS, :], qts[hh],
                              preferred_element_type=F32))
        return ss

    def probs(ss, m, p_ref):
        m_new = m
        for s in ss:
            m_new = jnp.maximum(m_new, jnp.max(s, axis=0, keepdims=True))
        for i, s in enumerate(ss):
            p_ref[i * QK_ROWS:(i + 1) * QK_ROWS, :] = jnp.exp2(s - m_new).astype(BF16)
        return m_new, jnp.exp2(m - m_new)

    def accumulate(c, hh, acc, alpha, p_ref):
        return alpha * acc + jnp.dot(vt_ref[0, hh, c], p_ref[...], preferred_element_type=F32)

    hs = range(2)
    m = [jnp.full((1, tq), -jnp.inf, F32) for _ in hs]
    alpha = [None for _ in hs]
    accs = [jnp.zeros((HEAD_PAD, tq), F32) for _ in hs]
    for c in range(nk):
        cur, prev = (pa_refs, pb_refs) if c % 2 == 0 else (pb_refs, pa_refs)
        s = [scores(c, hh) for hh in hs]
        if c > 0:
            for hh in hs:
                accs[hh] = accumulate(c - 1, hh, accs[hh], alpha[hh], prev[hh])
        for hh in hs:
            m[hh], alpha[hh] = probs(s[hh], m[hh], cur[hh])
    last = pa_refs if (nk - 1) % 2 == 0 else pb_refs
    accs = [accumulate(nk - 1, hh, accs[hh], alpha[hh], last[hh]) for hh in hs]
    row = lax.broadcasted_iota(jnp.int32, (HEAD_PAD, 1), 0)
    out_t = jnp.where(row < V_DIM, accs[0] / accs[0][V_DIM:V_DIM + 1], accs[1] / accs[1][0:1])
    o_ref[0] = out_t.T.astype(BF16)


def _attention(qt, k, vt, tq):
    bsz, nh, s, _ = k.shape
    nk, tk = vt.shape[2], vt.shape[4]
    return pl.pallas_call(
        functools.partial(_attn_kernel, nk=nk),
        grid=(bsz, nh // 2, s // tq),
        in_specs=[pl.BlockSpec((1, 2, HEAD_PAD, tq), lambda b, j, i: (b, j, 0, i)),
                  pl.BlockSpec((1, 2, s, HEAD_PAD), lambda b, j, i: (b, j, 0, 0)),
                  pl.BlockSpec((1, 2, nk, HEAD_PAD, tk), lambda b, j, i: (b, j, 0, 0, 0))],
        out_specs=pl.BlockSpec((1, tq, HEAD_PAD), lambda b, j, i: (b, i, j)),
        out_shape=jax.ShapeDtypeStruct((bsz, s, nh * V_DIM), BF16),
        scratch_shapes=[pltpu.VMEM((tk, tq), BF16)] * 4,
        compiler_params=_cparams(("parallel", "parallel", "arbitrary")),
        name="attention",
    )(qt, k, vt)


def _fft_a_kernel(zr_ref, zi_ref, m1_ref, br_ref, bi_ref):
    n1 = zr_ref.shape[1]
    ts2 = m1_ref.shape[0]
    for j in range(ts2):
        sl = slice(j * F_WIDTH, (j + 1) * F_WIDTH)
        xx = jnp.concatenate([zr_ref[0, :, sl], zi_ref[0, :, sl]], axis=0)
        y = jnp.dot(m1_ref[j], xx, preferred_element_type=F32)
        br_ref[0, :, sl] = y[:n1].astype(BF16)
        bi_ref[0, :, sl] = y[n1:].astype(BF16)


def _fft_a(zr, zi, m1, ts2):
    bsz, s, _ = zr.shape
    n1 = s // FFT_N2
    zr2 = zr.reshape(bsz, n1, FFT_N2 * F_WIDTH)
    zi2 = zi.reshape(bsz, n1, FFT_N2 * F_WIDTH)
    blk = pl.BlockSpec((1, n1, ts2 * F_WIDTH), lambda b, j: (b, 0, j))
    return pl.pallas_call(
        _fft_a_kernel,
        grid=(bsz, FFT_N2 // ts2),
        in_specs=[blk, blk, pl.BlockSpec((ts2, 2 * n1, 2 * n1), lambda b, j: (j, 0, 0))],
        out_specs=[blk, blk],
        out_shape=[jax.ShapeDtypeStruct(zr2.shape, BF16)] * 2,
        compiler_params=_cparams(("parallel", "parallel")),
        name="fft_a",
    )(zr2, zi2, m1)


def _fft_b_kernel(br_ref, bi_ref, wb_ref, o_ref):
    tk1 = br_ref.shape[1]
    for j in range(tk1):
        xx = jnp.concatenate([br_ref[0, j], bi_ref[0, j]], axis=0)
        y = jnp.dot(wb_ref[...], xx, preferred_element_type=F32)
        o_ref[0, :, j * F_WIDTH:(j + 1) * F_WIDTH] = y.astype(BF16)


def _fft_b(br, bi, wb, tk1):
    bsz, n1, _ = br.shape
    br4 = br.reshape(bsz, n1, FFT_N2, F_WIDTH)
    bi4 = bi.reshape(bsz, n1, FFT_N2, F_WIDTH)
    blk = pl.BlockSpec((1, tk1, FFT_N2, F_WIDTH), lambda b, j: (b, j, 0, 0))
    out = pl.pallas_call(
        _fft_b_kernel,
        grid=(bsz, n1 // tk1),
        in_specs=[blk, blk, pl.BlockSpec((FFT_N2, 2 * FFT_N2), lambda b, j: (0, 0))],
        out_specs=pl.BlockSpec((1, FFT_N2, tk1 * F_WIDTH), lambda b, j: (b, 0, j)),
        out_shape=jax.ShapeDtypeStruct((bsz, FFT_N2, n1 * F_WIDTH), BF16),
        compiler_params=_cparams(("parallel", "parallel")),
        name="fft_b",
    )(br4, bi4, wb)
    return out.reshape(bsz, n1 * FFT_N2, F_WIDTH)


def _dft_constants(s):
    n1, n2 = s // FFT_N2, FFT_N2
    m = np.arange(F_GDIM)
    th = 2.0 * np.pi * np.outer(m, m) / F_GDIM
    eye = np.eye(F_GROUPS)
    bd = np.concatenate([np.kron(eye, np.cos(th)), -np.kron(eye, np.sin(th))], axis=1) / np.sqrt(F_GDIM)
    k1 = np.arange(n1)[None, :, None]
    s1 = np.arange(n1)[None, None, :]
    s2 = np.arange(n2)[:, None, None]
    ph = 2.0 * np.pi * ((k1 * (n2 * s1 + s2)) % s) / s
    cr, ci = np.cos(ph) / np.sqrt(n1), -np.sin(ph) / np.sqrt(n1)
    m1 = np.concatenate([np.concatenate([cr, -ci], axis=2), np.concatenate([ci, cr], axis=2)], axis=1)
    k2 = np.arange(n2)
    tb = 2.0 * np.pi * np.outer(k2, k2) / n2
    wb = np.concatenate([np.cos(tb), np.sin(tb)], axis=1) / np.sqrt(n2)
    return (jnp.asarray(bd, BF16), jnp.asarray(m1, BF16), jnp.asarray(wb, BF16))


def _merge_kernel(at_ref, fo_ref, gate_ref, x_ref, wba_ref, wbf_ref, wout_ref, gt1_ref, sh2_ref,
                  sc2_ref, gffn_ref, wr_ref, br_ref, h1_ref, v2_ref, route_ref, cnt_ref):
    d = x_ref.shape[-1]
    a = jnp.dot(at_ref[0], wba_ref[...], preferred_element_type=F32)
    fo = jnp.dot(fo_ref[0], wbf_ref[...], preferred_element_type=F32)
    merged = gate_ref[0, :, :d].astype(F32) * a + gate_ref[0, :, d:].astype(F32) * fo
    y = jnp.dot(merged.astype(BF16), wout_ref[...], preferred_element_type=F32)
    h1 = x_ref[0] + gt1_ref[0] * y
    h1_ref[0] = h1
    v2 = _rms(h1, gffn_ref[...]) * (1.0 + sc2_ref[0]) + sh2_ref[0]
    v2_ref[0] = v2.astype(BF16)

    logit = jnp.dot(v2, wr_ref[...], preferred_element_type=F32,
                    precision=lax.Precision.HIGHEST) + br_ref[...]
    lane = lax.broadcasted_iota(jnp.int32, logit.shape, 1)
    neg = -jnp.inf
    big = jnp.int32(LANES)

    def softmax_masked(mask):
        z = jnp.where(mask, logit, neg)
        e = jnp.exp(z - jnp.max(z, axis=-1, keepdims=True))
        return e / jnp.sum(e, axis=-1, keepdims=True)

    def top1(p, mask):
        pm = jnp.where(mask, p, -1.0)
        best = jnp.max(pm, axis=-1, keepdims=True)
        idx = jnp.min(jnp.where(pm == best, lane, big), axis=-1, keepdims=True)
        return best, idx

    gmask = (lane >= N_EXPERTS) & (lane < N_EXPERTS + N_GROUPS)
    g_prob = softmax_masked(gmask)
    g_p, g_lane = top1(g_prob, gmask)
    e_lo = (g_lane - N_EXPERTS) * E_PER_GROUP
    emask = (lane >= e_lo) & (lane < e_lo + E_PER_GROUP)
    e_prob = softmax_masked(emask)
    p1, i1 = top1(e_prob, emask)
    p2, i2 = top1(e_prob, emask & (lane != i1))
    den = p1 + p2
    w1 = g_p * (p1 / den)
    w2 = g_p * (p2 / den)
    route_ref[0] = jnp.where(lane == 0, i1.astype(F32), jnp.where(lane == 1, i2.astype(F32),
                             jnp.where(lane == 2, w1, jnp.where(lane == 3, w2, 0.0))))
    cnt_ref[0, 0] = jnp.sum(((lane == i1) | (lane == i2)).astype(F32), axis=0, keepdims=True)


def _merge(attn, four, gates, x, wba, wbf, wout, gt1, sh2, sc2, gffn, wr, br, tm):
    bsz, s, d = x.shape
    const = lambda shape: pl.BlockSpec(shape, lambda b, i: (0,) * len(shape))
    tok = lambda w: pl.BlockSpec((1, tm, w), lambda b, i: (b, i, 0))
    per_b = pl.BlockSpec((1, 1, d), lambda b, i: (b, 0, 0))
    return pl.pallas_call(
        _merge_kernel,
        grid=(bsz, s // tm),
        in_specs=[tok(attn.shape[-1]), tok(F_WIDTH), tok(2 * d), tok(d), const(wba.shape),
                  const(wbf.shape), const(wout.shape), per_b, per_b, per_b, const((1, d)),
                  const(wr.shape), const((1, LANES))],
        out_specs=[tok(d), tok(d), tok(LANES),
                   pl.BlockSpec((1, 1, 1, LANES), lambda b, i: (b, i, 0, 0))],
        out_shape=[jax.ShapeDtypeStruct((bsz, s, d), F32), jax.ShapeDtypeStruct((bsz, s, d), BF16),
                   jax.ShapeDtypeStruct((bsz, s, LANES), F32),
                   jax.ShapeDtypeStruct((bsz, s // tm, 1, LANES), F32)],
        compiler_params=_cparams(("parallel", "parallel")),
        name="merge",
    )(attn, four, gates, x, wba, wbf, wout, gt1, sh2, sc2, gffn, wr, br)


GRAN = 16
MOE_ROW_TILE = 512


def _moe_metadata(cnt, tm):
    ntile = cnt.shape[0]
    c = cnt[:, :N_EXPERTS].astype(jnp.int32)
    pc = (c + GRAN - 1) // GRAN * GRAN
    loff = jnp.cumsum(pc, axis=1) - pc
    tot = jnp.sum(pc, axis=0)
    tot_r = (tot + MOE_ROW_TILE - 1) // MOE_ROW_TILE * MOE_ROW_TILE
    ends = jnp.cumsum(tot_r)
    base = ends - tot_r
    goff = base[None, :] + jnp.cumsum(pc, axis=0) - pc
    cap_tiles = (2 * ntile * tm + ntile * N_EXPERTS * GRAN) // MOE_ROW_TILE + N_EXPERTS
    n_valid = (ends[-1] // MOE_ROW_TILE).astype(jnp.int32)
    tile_start = jnp.arange(cap_tiles, dtype=jnp.int32) * MOE_ROW_TILE
    tile_expert = jnp.minimum(jnp.searchsorted(ends, tile_start, side="right"),
                              N_EXPERTS - 1).astype(jnp.int32)
    flat = lambda a: a.reshape(-1).astype(jnp.int32)
    tail = tot_r - tot + jnp.where(jnp.arange(N_EXPERTS) == N_EXPERTS - 1,
                                   cap_tiles * MOE_ROW_TILE - ends[-1], 0)
    meta = dict(loff=flat(loff), goff=flat(goff), ngran=flat(pc // GRAN),
                tail_start=flat(base + tot), tail_n=flat(tail // GRAN),
                tile_expert=tile_expert, n_valid=n_valid.reshape(1))
    return meta, loff.astype(F32), cap_tiles


def _run_copies(i, loff_s, goff_s, ngran_s, make_copy):
    def per_expert(e, n_started):
        n = ngran_s[i * N_EXPERTS + e]
        lo = loff_s[i * N_EXPERTS + e]
        go = goff_s[i * N_EXPERTS + e]

        def per_granule(g, carry):
            make_copy(pl.multiple_of(lo + g * GRAN, GRAN), pl.multiple_of(go + g * GRAN, GRAN)).start()
            return carry

        lax.fori_loop(0, n, per_granule, 0)
        return n_started + n

    return lax.fori_loop(0, N_EXPERTS, per_expert, 0)


def _wait_copies(n, make_copy):
    def one(_, carry):
        make_copy(0, 0).wait()
        return carry
    lax.fori_loop(0, n, one, 0)


def _local_positions(route, loffv, ltri):
    lane = lax.broadcasted_iota(jnp.int32, route.shape, 1).astype(F32)
    o1 = lane == route[:, 0:1]
    o2 = lane == route[:, 1:2]
    before = jnp.dot(ltri, (o1 | o2).astype(BF16), preferred_element_type=F32)
    start = before + loffv
    pos1 = jnp.sum(jnp.where(o1, start, 0.0), axis=-1, keepdims=True)
    pos2 = jnp.sum(jnp.where(o2, start, 0.0), axis=-1, keepdims=True)
    return pos1, pos2


def _dispatch_kernel(loff_s, goff_s, ngran_s, tstart_s, tn_s, v2_ref, route_ref, loffv_ref,
                     ltri_ref, xs_ref, posw_ref, loc_ref, zero_ref, sem):
    i = pl.program_id(0)
    rows = loc_ref.shape[0]
    route = route_ref[...]
    pos1, pos2 = _local_positions(route, loffv_ref[0], ltri_ref[...])
    lane = lax.broadcasted_iota(jnp.int32, route.shape, 1)
    posw = jnp.where(lane == 0, pos1, jnp.where(lane == 1, pos2, route))
    posw_ref[...] = posw
    pos_t = posw.T
    riota = lax.broadcasted_iota(jnp.int32, (rows, 1), 0).astype(F32)
    onehot = ((riota == pos_t[0:1]) | (riota == pos_t[1:2])).astype(BF16)
    loc_ref[...] = jnp.dot(onehot, v2_ref[...], preferred_element_type=F32).astype(BF16)

    def out_copy(lo, go):
        return pltpu.make_async_copy(loc_ref.at[pl.ds(lo, GRAN)], xs_ref.at[pl.ds(go, GRAN)], sem)

    n = _run_copies(i, loff_s, goff_s, ngran_s, out_copy)
    _wait_copies(n, out_copy)

    @pl.when(i == pl.num_programs(0) - 1)
    def _():
        zero_ref[...] = jnp.zeros(zero_ref.shape, BF16)

        def zero_copy(go):
            return pltpu.make_async_copy(zero_ref, xs_ref.at[pl.ds(go, GRAN)], sem)

        def per_expert(e, n_started):
            def per_granule(g, carry):
                zero_copy(pl.multiple_of(tstart_s[e] + g * GRAN, GRAN)).start()
                return carry
            lax.fori_loop(0, tn_s[e], per_granule, 0)
            return n_started + tn_s[e]

        nz = lax.fori_loop(0, N_EXPERTS, per_expert, 0)
        _wait_copies(nz, lambda lo, go: zero_copy(go))


def _dispatch(v2, route, meta, loffv, ltri, cap_tiles, tm, rows):
    t, d = v2.shape
    ntile = t // tm
    grid_spec = pltpu.PrefetchScalarGridSpec(
        num_scalar_prefetch=5,
        grid=(ntile,),
        in_specs=[pl.BlockSpec((tm, d), lambda i, *_: (i, 0)),
                  pl.BlockSpec((tm, LANES), lambda i, *_: (i, 0)),
                  pl.BlockSpec((1, 1, LANES), lambda i, *_: (i, 0, 0)),
                  pl.BlockSpec((tm, tm), lambda i, *_: (0, 0))],
        out_specs=[pl.BlockSpec(memory_space=pl.ANY),
                   pl.BlockSpec((tm, LANES), lambda i, *_: (i, 0))],
        scratch_shapes=[pltpu.VMEM((rows, d), BF16), pltpu.VMEM((GRAN, d), BF16),
                        pltpu.SemaphoreType.DMA(())],
    )
    return pl.pallas_call(
        _dispatch_kernel,
        grid_spec=grid_spec,
        out_shape=[jax.ShapeDtypeStruct((cap_tiles * MOE_ROW_TILE, d), BF16),
                   jax.ShapeDtypeStruct((t, LANES), F32)],
        compiler_params=_cparams(("arbitrary",)),
        name="dispatch",
    )(meta["loff"], meta["goff"], meta["ngran"], meta["tail_start"], meta["tail_n"],
      v2, route, loffv, ltri)


def _experts_kernel(texp_s, nvalid_s, x_ref, wgu_ref, wd_ref, y_ref):
    del texp_s

    @pl.when(pl.program_id(0) < nvalid_s[0])
    def _():
        hc = jnp.dot(x_ref[...], wgu_ref[0], preferred_element_type=F32)
        g = hc[:, :D_EXPERT]
        hid = (g * jax.nn.sigmoid(g) * hc[:, D_EXPERT:]).astype(BF16)
        y_ref[...] = jnp.dot(hid, wd_ref[0], preferred_element_type=F32).astype(BF16)

    @pl.when(pl.program_id(0) >= nvalid_s[0])
    def _():
        y_ref[...] = jnp.zeros(y_ref.shape, BF16)


def _experts(xs, wgu, wd, meta, cap_tiles):
    d = xs.shape[1]
    row = lambda j, texp, nv: (jnp.minimum(j, nv[0] - 1), 0)
    wsel = lambda j, texp, nv: (texp[jnp.minimum(j, nv[0] - 1)], 0, 0)
    grid_spec = pltpu.PrefetchScalarGridSpec(
        num_scalar_prefetch=2,
        grid=(cap_tiles,),
        in_specs=[pl.BlockSpec((MOE_ROW_TILE, d), row),
                  pl.BlockSpec((1, d, 2 * D_EXPERT), wsel),
                  pl.BlockSpec((1, D_EXPERT, d), wsel)],
        out_specs=pl.BlockSpec((MOE_ROW_TILE, d), lambda j, texp, nv: (j, 0)),
    )
    return pl.pallas_call(
        _experts_kernel,
        grid_spec=grid_spec,
        out_shape=jax.ShapeDtypeStruct(xs.shape, BF16),
        compiler_params=_cparams(("arbitrary",)),
        name="experts",
    )(meta["tile_expert"], meta["n_valid"], xs, wgu, wd)


def _combine_kernel(loff_s, goff_s, ngran_s, posw_ref, ys_ref, h1_ref, gt2_ref, shf_ref, scf_ref,
                    gfin_ref, o_ref, loc_ref, sem):
    i = pl.program_id(0)
    rows = loc_ref.shape[0]
    loc_ref[...] = jnp.zeros(loc_ref.shape, BF16)

    def in_copy(lo, go):
        return pltpu.make_async_copy(ys_ref.at[pl.ds(go, GRAN)], loc_ref.at[pl.ds(lo, GRAN)], sem)

    n = _run_copies(i, loff_s, goff_s, ngran_s, in_copy)
    _wait_copies(n, in_copy)

    posw = posw_ref[...]
    riota = lax.broadcasted_iota(jnp.int32, (1, rows), 1).astype(F32)
    qw = (jnp.where(riota == posw[:, 0:1], posw[:, 2:3], 0.0)
          + jnp.where(riota == posw[:, 1:2], posw[:, 3:4], 0.0)).astype(BF16)
    moe = jnp.dot(qw, loc_ref[...], preferred_element_type=F32)
    h2 = h1_ref[...] + gt2_ref[0] * moe
    o_ref[...] = _rms(h2, gfin_ref[...]) * (1.0 + scf_ref[0]) + shf_ref[0]


def _combine(posw, ys, h1, gt2, shf, scf, gfin, meta, tm, rows, tiles_per_batch):
    t, d = h1.shape
    per_b = pl.BlockSpec((1, 1, d), lambda i, *_: (i // tiles_per_batch, 0, 0))
    grid_spec = pltpu.PrefetchScalarGridSpec(
        num_scalar_prefetch=3,
        grid=(t // tm,),
        in_specs=[pl.BlockSpec((tm, LANES), lambda i, *_: (i, 0)),
                  pl.BlockSpec(memory_space=pl.ANY),
                  pl.BlockSpec((tm, d), lambda i, *_: (i, 0)),
                  per_b, per_b, per_b, pl.BlockSpec((1, d), lambda i, *_: (0, 0))],
        out_specs=pl.BlockSpec((tm, d), lambda i, *_: (i, 0)),
        scratch_shapes=[pltpu.VMEM((rows, d), BF16), pltpu.SemaphoreType.DMA(())],
    )
    return pl.pallas_call(
        _combine_kernel,
        grid_spec=grid_spec,
        out_shape=jax.ShapeDtypeStruct((t, d), F32),
        compiler_params=_cparams(("arbitrary",)),
        name="combine",
    )(meta["loff"], meta["goff"], meta["ngran"], posw, ys, h1, gt2, shf, scf, gfin)


def _rot_cols(w):
    half = ROPE // 2
    return jnp.concatenate([-w[..., half:], w[..., :half]], axis=-1)


def _pad_cols(w, lo, width):
    return jnp.pad(w, [(0, 0)] * (w.ndim - 1) + [(lo, width - lo - w.shape[-1])])


def _prep_weights(w_in, w_uq, w_uk, w_uv):
    o2, o3 = Q_LORA + KV_LORA, Q_LORA + KV_LORA + ROPE
    w_kr = w_in[:, o2:o3]
    wcat = jnp.concatenate([
        w_in[:, :o2],
        _pad_cols(w_kr, NOPE, HEAD_PAD), _pad_cols(_rot_cols(w_kr), NOPE, HEAD_PAD),
        w_in[:, o3:]], axis=1).astype(BF16)
    wq_a = _pad_cols(w_uq, 0, HEAD_PAD)
    wq_b = _pad_cols(_rot_cols(w_uq[..., NOPE:]), NOPE, HEAD_PAD)
    r = w_uq.shape[0]
    wqt = jnp.concatenate([wq_a.reshape(r, -1), wq_b.reshape(r, -1)], axis=1).T.astype(BF16)
    wk = _pad_cols(w_uk, 0, HEAD_PAD).reshape(r, -1).astype(BF16)
    odd = (jnp.arange(N_HEADS) % 2 == 1)[None, :, None]
    wv = jnp.where(odd, _pad_cols(w_uv, V_DIM, HEAD_PAD), _pad_cols(w_uv, 0, HEAD_PAD)).reshape(r, -1)
    wvt = wv.T.astype(BF16)
    ones = np.zeros((N_HEADS, HEAD_PAD), np.float32)
    ones[0::2, V_DIM] = 1.0
    ones[1::2, 0] = 1.0
    return wcat, wqt, wk, wvt, jnp.asarray(ones.reshape(-1, 1))


def kernel(x, c, positions, w_ada, b_ada, g_norm_mix, w_in, g_q_lat, g_kv_lat, w_uq, w_uk, w_uv, w_branch_attn, w_branch_fourier, b_gates, w_out, g_norm_ffn, w_router_group, b_router_group, w_router_expert, b_router_expert, w_expert_gate, w_expert_up, w_expert_down, w_ada_final, b_ada_final, g_norm_final):
    bsz, s, d = x.shape
    assert w_ada.shape[0] == 1 and s % FFT_N2 == 0
    tm = min(512, s)

    mod = _mods(c, w_ada[0], b_ada[0])
    fmod = _mods(c, w_ada_final, b_ada_final)
    sh1, sc1, gt1, sh2, sc2, gt2 = [m.reshape(bsz, 1, d) for m in jnp.split(mod, 6, axis=-1)]
    shf, scf = [m.reshape(bsz, 1, d) for m in jnp.split(fmod, 2, axis=-1)]

    wcat, wqt, wk, wvt, vones = _prep_weights(w_in[0], w_uq[0], w_uk[0], w_uv[0])
    bd, m1, wb = _dft_constants(s)
    inv_freq = ROPE_THETA ** (-jnp.arange(0, ROPE, 2, dtype=F32) / ROPE)
    invf = _pad_cols(jnp.concatenate([inv_freq, inv_freq])[None, :], NOPE, HEAD_PAD).reshape(HEAD_PAD, 1)

    qt, k, vt, zr, zi, gates = _inproj(
        x, positions.reshape(bsz, 1, s), sh1, sc1, g_norm_mix[0][None, :], wcat,
        g_q_lat[0][None, :], g_kv_lat[0][None, :], wqt, wk, wvt, vones, bd, b_gates[0][None, :],
        invf, tm)

    attn = _attention(qt, k, vt, tq=min(256, s))
    br, bi = _fft_a(zr, zi, m1, ts2=8)
    four = _fft_b(br, bi, wb, tk1=16)

    wr = jnp.concatenate([w_router_expert[0].reshape(d, N_EXPERTS), w_router_group[0]], axis=1)
    wr = _pad_cols(wr, 0, LANES)
    brt = _pad_cols(jnp.concatenate([b_router_expert[0].reshape(-1), b_router_group[0]])[None, :], 0, LANES)
    h1, v2, route, cnt = _merge(attn, four, gates, x, w_branch_attn[0].astype(BF16),
                                w_branch_fourier[0].astype(BF16), w_out[0].astype(BF16), gt1, sh2,
                                sc2, g_norm_ffn[0][None, :], wr, brt, tm)

    t = bsz * s
    meta, loffv, cap_tiles = _moe_metadata(cnt.reshape(-1, LANES), tm)
    rows = 2 * tm + N_EXPERTS * GRAN
    loffv = _pad_cols(loffv, 0, LANES).reshape(-1, 1, LANES)
    ltri = jnp.asarray(np.tril(np.ones((tm, tm), np.float32), -1), BF16)
    xs, posw = _dispatch(v2.reshape(t, d), route.reshape(t, LANES), meta, loffv, ltri, cap_tiles,
                         tm, rows)
    wgu = jnp.concatenate([w_expert_gate[0], w_expert_up[0]], axis=-1).astype(BF16)
    wd = w_expert_down[0].astype(BF16)
    ys = _experts(xs, wgu, wd, meta, cap_tiles)
    out = _combine(posw, ys, h1.reshape(t, d), gt2, shf, scf, g_norm_final[None, :], meta, tm, rows,
                   s // tm)
    return out.reshape(bsz, s, d)
```

```python
import functools

import numpy as np
import jax
import jax.numpy as jnp
from jax import lax
from jax.experimental import pallas as pl
from jax.experimental.pallas import tpu as pltpu

F32 = jnp.float32
BF16 = jnp.bfloat16

N_HEADS = 8
NOPE = 64
ROPE = 32
V_DIM = 64
HEAD_PAD = 128
Q_LORA = 256
KV_LORA = 256
ROPE_THETA = 10000.0
F_GROUPS = 8
F_GDIM = 64
F_WIDTH = F_GROUPS * F_GDIM
N_GROUPS = 4
E_PER_GROUP = 8
N_EXPERTS = N_GROUPS * E_PER_GROUP
D_EXPERT = 256
EPS = 1e-6
FFT_N2 = 64
LANES = 128
QK_ROWS = 256
VMEM_LIMIT = 56 * 1024 * 1024


def _cparams(sem):
    return pltpu.CompilerParams(dimension_semantics=sem, vmem_limit_bytes=VMEM_LIMIT)


def _mods_kernel(c_ref, w_ref, b_ref, o_ref):
    c = c_ref[...]
    ca = c * jax.nn.sigmoid(c)
    o_ref[...] = jnp.dot(ca, w_ref[...], preferred_element_type=F32,
                         precision=lax.Precision.HIGHEST) + b_ref[...]


def _mods(c, w, b):
    bsz, d = c.shape
    n = w.shape[1]
    tn = 1024
    return pl.pallas_call(
        _mods_kernel,
        grid=(n // tn,),
        in_specs=[pl.BlockSpec((bsz, d), lambda j: (0, 0)),
                  pl.BlockSpec((d, tn), lambda j: (0, j)),
                  pl.BlockSpec((1, tn), lambda j: (0, j))],
        out_specs=pl.BlockSpec((bsz, tn), lambda j: (0, j)),
        out_shape=jax.ShapeDtypeStruct((bsz, n), F32),
        compiler_params=_cparams(("arbitrary",)),
        name="mods",
    )(c, w, b.reshape(1, n))


def _rms(x, g):
    r = lax.rsqrt(jnp.mean(x * x, axis=-1, keepdims=True) + EPS)
    return x * r * g


def _inproj_kernel(x_ref, pos_ref, sh_ref, sc_ref, gmix_ref, wcat_ref, gq_ref, gkv_ref,
                   wqt_ref, wk_ref, wvt_ref, vones_ref, bd_ref, bg_ref, invf_ref,
                   qt_ref, k_ref, vt_ref, zr_ref, zi_ref, gate_ref, *, scale):
    hp = N_HEADS * HEAD_PAD
    nt_dims = (((1,), (1,)), ((), ()))
    u = _rms(x_ref[0], gmix_ref[...]) * (1.0 + sc_ref[0]) + sh_ref[0]
    ub = u.astype(BF16)

    o_q, o_kv, o_kr, o_f, o_g = 0, Q_LORA, Q_LORA + KV_LORA, Q_LORA + KV_LORA + 2 * HEAD_PAD, \
        Q_LORA + KV_LORA + 2 * HEAD_PAD + F_WIDTH

    ang_t = invf_ref[...] * pos_ref[0].astype(F32)
    cos_t = jnp.cos(ang_t)
    sin_t = jnp.sin(ang_t)

    qn = _rms(jnp.dot(ub, wcat_ref[:, o_q:o_kv], preferred_element_type=F32), gq_ref[...]).astype(BF16)
    qab_t = lax.dot_general(wqt_ref[...], qn, nt_dims, preferred_element_type=F32)
    cq = cos_t * scale
    sq = sin_t * scale
    for h in range(N_HEADS):
        a = qab_t[h * HEAD_PAD:(h + 1) * HEAD_PAD]
        b = qab_t[hp + h * HEAD_PAD:hp + (h + 1) * HEAD_PAD]
        qt_ref[0, h] = (a * cq + b * sq).astype(BF16)

    kvn = _rms(jnp.dot(ub, wcat_ref[:, o_kv:o_kr], preferred_element_type=F32), gkv_ref[...]).astype(BF16)
    kn = jnp.dot(kvn, wk_ref[...], preferred_element_type=F32)
    kr = jnp.dot(ub, wcat_ref[:, o_kr:o_f], preferred_element_type=F32)
    krr = kr[:, :HEAD_PAD] * cos_t.T + kr[:, HEAD_PAD:] * sin_t.T
    vt = lax.dot_general(wvt_ref[...], kvn, nt_dims, preferred_element_type=F32) + vones_ref[...]
    for h in range(N_HEADS):
        k_ref[0, h] = (kn[:, h * HEAD_PAD:(h + 1) * HEAD_PAD] + krr).astype(BF16)
        vt_ref[0, h, 0] = vt[h * HEAD_PAD:(h + 1) * HEAD_PAD].astype(BF16)

    fin = jnp.dot(ub, wcat_ref[:, o_f:o_g], preferred_element_type=F32).astype(BF16)
    zz = jnp.dot(fin, bd_ref[...], preferred_element_type=F32)
    zr_ref[0] = zz[:, :F_WIDTH].astype(BF16)
    zi_ref[0] = zz[:, F_WIDTH:].astype(BF16)

    zg = jnp.dot(ub, wcat_ref[:, o_g:], preferred_element_type=F32) + bg_ref[...]
    gate_ref[0] = jax.nn.sigmoid(zg).astype(BF16)


def _inproj(x, pos_row, sh1, sc1, gmix, wcat, gq, gkv, wqt, wk, wvt, vones, bd, bg, invf, tm):
    bsz, s, d = x.shape
    hp = N_HEADS * HEAD_PAD
    nt = s // tm
    const = lambda shape: pl.BlockSpec(shape, lambda b, i: (0,) * len(shape))
    tok = lambda w: pl.BlockSpec((1, tm, w), lambda b, i: (b, i, 0))
    per_b = pl.BlockSpec((1, 1, d), lambda b, i: (b, 0, 0))
    scale = float((NOPE + ROPE) ** -0.5 * np.log2(np.e))
    return pl.pallas_call(
        functools.partial(_inproj_kernel, scale=scale),
        grid=(bsz, nt),
        in_specs=[tok(d), pl.BlockSpec((1, 1, tm), lambda b, i: (b, 0, i)), per_b, per_b,
                  const((1, d)), const(wcat.shape), const((1, Q_LORA)), const((1, KV_LORA)),
                  const(wqt.shape), const(wk.shape), const(wvt.shape), const((hp, 1)),
                  const(bd.shape), const((1, bg.shape[1])), const((HEAD_PAD, 1))],
        out_specs=[pl.BlockSpec((1, N_HEADS, HEAD_PAD, tm), lambda b, i: (b, 0, 0, i)),
                   pl.BlockSpec((1, N_HEADS, tm, HEAD_PAD), lambda b, i: (b, 0, i, 0)),
                   pl.BlockSpec((1, N_HEADS, 1, HEAD_PAD, tm), lambda b, i: (b, 0, i, 0, 0)),
                   tok(F_WIDTH), tok(F_WIDTH), tok(2 * d)],
        out_shape=[jax.ShapeDtypeStruct((bsz, N_HEADS, HEAD_PAD, s), BF16),
                   jax.ShapeDtypeStruct((bsz, N_HEADS, s, HEAD_PAD), BF16),
                   jax.ShapeDtypeStruct((bsz, N_HEADS, nt, HEAD_PAD, tm), BF16),
                   jax.ShapeDtypeStruct((bsz, s, F_WIDTH), BF16),
                   jax.ShapeDtypeStruct((bsz, s, F_WIDTH), BF16),
                   jax.ShapeDtypeStruct((bsz, s, 2 * d), BF16)],
        compiler_params=_cparams(("parallel", "parallel")),
        name="inproj",
    )(x, pos_row, sh1, sc1, gmix, wcat, gq, gkv, wqt, wk, wvt, vones, bd, bg, invf)


def _attn_kernel(qt_ref, k_ref, vt_ref, o_ref, pa0_ref, pa1_ref, pb0_ref, pb1_ref, *, nk):
    tq = qt_ref.shape[3]
    tk = vt_ref.shape[4]
    qts = [qt_ref[0, hh] for hh in range(2)]
    pa_refs = (pa0_ref, pa1_ref)
    pb_refs = (pb0_ref, pb1_ref)

    def scores(c, hh):
        ss = []
        for i in range(tk // QK_ROWS):
            off = c * tk + i * QK_ROWS
            ss.append(jnp.dot(k_ref[0, hh, off:off + QK_ROWS, :], qts[hh],
                              preferred_element_type=F32))
        return ss

    def probs(ss, m, p_ref):
        m_new = m
        for s in ss:
            m_new = jnp.maximum(m_new, jnp.max(s, axis=0, keepdims=True))
        for i, s in enumerate(ss):
            p_ref[i * QK_ROWS:(i + 1) * QK_ROWS, :] = jnp.exp2(s - m_new).astype(BF16)
        return m_new, jnp.exp2(m - m_new)

    def accumulate(c, hh, acc, alpha, p_ref):
        return alpha * acc + jnp.dot(vt_ref[0, hh, c], p_ref[...], preferred_element_type=F32)

    hs = range(2)
    m = [jnp.full((1, tq), -jnp.inf, F32) for _ in hs]
    alpha = [None for _ in hs]
    accs = [jnp.zeros((HEAD_PAD, tq), F32) for _ in hs]
    for c in range(nk):
        cur, prev = (pa_refs, pb_refs) if c % 2 == 0 else (pb_refs, pa_refs)
        s = [scores(c, hh) for hh in hs]
        if c > 0:
            for hh in hs:
                accs[hh] = accumulate(c - 1, hh, accs[hh], alpha[hh], prev[hh])
        for hh in hs:
            m[hh], alpha[hh] = probs(s[hh], m[hh], cur[hh])
    last = pa_refs if (nk - 1) % 2 == 0 else pb_refs
    accs = [accumulate(nk - 1, hh, accs[hh], alpha[hh], last[hh]) for hh in hs]
    row = lax.broadcasted_iota(jnp.int32, (HEAD_PAD, 1), 0)
    out_t = jnp.where(row < V_DIM, accs[0] / accs[0][V_DIM:V_DIM + 1], accs[1] / accs[1][0:1])
    o_ref[0] = out_t.T.astype(BF16)


def _attention(qt, k, vt, tq):
    bsz, nh, s, _ = k.shape
    nk, tk = vt.shape[2], vt.shape[4]
    return pl.pallas_call(
        functools.partial(_attn_kernel, nk=nk),
        grid=(bsz, nh // 2, s // tq),
        in_specs=[pl.BlockSpec((1, 2, HEAD_PAD, tq), lambda b, j, i: (b, j, 0, i)),
                  pl.BlockSpec((1, 2, s, HEAD_PAD), lambda b, j, i: (b, j, 0, 0)),
                  pl.BlockSpec((1, 2, nk, HEAD_PAD, tk), lambda b, j, i: (b, j, 0, 0, 0))],
        out_specs=pl.BlockSpec((1, tq, HEAD_PAD), lambda b, j, i: (b, i, j)),
        out_shape=jax.ShapeDtypeStruct((bsz, s, nh * V_DIM), BF16),
        scratch_shapes=[pltpu.VMEM((tk, tq), BF16)] * 4,
        compiler_params=_cparams(("parallel", "parallel", "arbitrary")),
        name="attention",
    )(qt, k, vt)


def _fft_a_kernel(zr_ref, zi_ref, m1_ref, br_ref, bi_ref):
    n1 = zr_ref.shape[1]
    ts2 = m1_ref.shape[0]
    for j in range(ts2):
        sl = slice(j * F_WIDTH, (j + 1) * F_WIDTH)
        xx = jnp.concatenate([zr_ref[0, :, sl], zi_ref[0, :, sl]], axis=0)
        y = jnp.dot(m1_ref[j], xx, preferred_element_type=F32)
        br_ref[0, :, sl] = y[:n1].astype(BF16)
        bi_ref[0, :, sl] = y[n1:].astype(BF16)


def _fft_a(zr, zi, m1, ts2):
    bsz, s, _ = zr.shape
    n1 = s // FFT_N2
    zr2 = zr.reshape(bsz, n1, FFT_N2 * F_WIDTH)
    zi2 = zi.reshape(bsz, n1, FFT_N2 * F_WIDTH)
    blk = pl.BlockSpec((1, n1, ts2 * F_WIDTH), lambda b, j: (b, 0, j))
    return pl.pallas_call(
        _fft_a_kernel,
        grid=(bsz, FFT_N2 // ts2),
        in_specs=[blk, blk, pl.BlockSpec((ts2, 2 * n1, 2 * n1), lambda b, j: (j, 0, 0))],
        out_specs=[blk, blk],
        out_shape=[jax.ShapeDtypeStruct(zr2.shape, BF16)] * 2,
        compiler_params=_cparams(("parallel", "parallel")),
        name="fft_a",
    )(zr2, zi2, m1)


def _fft_b_kernel(br_ref, bi_ref, wb_ref, o_ref):
    tk1 = br_ref.shape[1]
    for j in range(tk1):
        xx = jnp.concatenate([br_ref[0, j], bi_ref[0, j]], axis=0)
        y = jnp.dot(wb_ref[...], xx, preferred_element_type=F32)
        o_ref[0, :, j * F_WIDTH:(j + 1) * F_WIDTH] = y.astype(BF16)


def _fft_b(br, bi, wb, tk1):
    bsz, n1, _ = br.shape
    br4 = br.reshape(bsz, n1, FFT_N2, F_WIDTH)
    bi4 = bi.reshape(bsz, n1, FFT_N2, F_WIDTH)
    blk = pl.BlockSpec((1, tk1, FFT_N2, F_WIDTH), lambda b, j: (b, j, 0, 0))
    out = pl.pallas_call(
        _fft_b_kernel,
        grid=(bsz, n1 // tk1),
        in_specs=[blk, blk, pl.BlockSpec((FFT_N2, 2 * FFT_N2), lambda b, j: (0, 0))],
        out_specs=pl.BlockSpec((1, FFT_N2, tk1 * F_WIDTH), lambda b, j: (b, 0, j)),
        out_shape=jax.ShapeDtypeStruct((bsz, FFT_N2, n1 * F_WIDTH), BF16),
        compiler_params=_cparams(("parallel", "parallel")),
        name="fft_b",
    )(br4, bi4, wb)
    return out.reshape(bsz, n1 * FFT_N2, F_WIDTH)


def _dft_constants(s):
    n1, n2 = s // FFT_N2, FFT_N2
    m = np.arange(F_GDIM)
    th = 2.0 * np.pi * np.outer(m, m) / F_GDIM
    eye = np.eye(F_GROUPS)
    bd = np.concatenate([np.kron(eye, np.cos(th)), -np.kron(eye, np.sin(th))], axis=1) / np.sqrt(F_GDIM)
    k1 = np.arange(n1)[None, :, None]
    s1 = np.arange(n1)[None, None, :]
    s2 = np.arange(n2)[:, None, None]
    ph = 2.0 * np.pi * ((k1 * (n2 * s1 + s2)) % s) / s
    cr, ci = np.cos(ph) / np.sqrt(n1), -np.sin(ph) / np.sqrt(n1)
    m1 = np.concatenate([np.concatenate([cr, -ci], axis=2), np.concatenate([ci, cr], axis=2)], axis=1)
    k2 = np.arange(n2)
    tb = 2.0 * np.pi * np.outer(k2, k2) / n2
    wb = np.concatenate([np.cos(tb), np.sin(tb)], axis=1) / np.sqrt(n2)
    return (jnp.asarray(bd, BF16), jnp.asarray(m1, BF16), jnp.asarray(wb, BF16))


def _merge_kernel(at_ref, fo_ref, gate_ref, x_ref, wba_ref, wbf_ref, wout_ref, gt1_ref, sh2_ref,
                  sc2_ref, gffn_ref, wr_ref, br_ref, h1_ref, v2_ref, route_ref, cnt_ref):
    d = x_ref.shape[-1]
    a = jnp.dot(at_ref[0], wba_ref[...], preferred_element_type=F32)
    fo = jnp.dot(fo_ref[0], wbf_ref[...], preferred_element_type=F32)
    merged = gate_ref[0, :, :d].astype(F32) * a + gate_ref[0, :, d:].astype(F32) * fo
    y = jnp.dot(merged.astype(BF16), wout_ref[...], preferred_element_type=F32)
    h1 = x_ref[0] + gt1_ref[0] * y
    h1_ref[0] = h1
    v2 = _rms(h1, gffn_ref[...]) * (1.0 + sc2_ref[0]) + sh2_ref[0]
    v_hi = v2.astype(BF16)
    v2_ref[0] = v_hi

    v_lo = (v2 - v_hi.astype(F32)).astype(BF16)
    parts = jnp.dot(jnp.concatenate([v_hi, v_lo], axis=1), wr_ref[...], preferred_element_type=F32)
    logit = parts[:, :LANES] + parts[:, LANES:] + br_ref[...]
    lane = lax.broadcasted_iota(jnp.int32, logit.shape, 1)
    neg = -jnp.inf
    big = jnp.int32(LANES)

    def softmax_masked(mask):
        z = jnp.where(mask, logit, neg)
        e = jnp.exp(z - jnp.max(z, axis=-1, keepdims=True))
        return e / jnp.sum(e, axis=-1, keepdims=True)

    def top1(p, mask):
        pm = jnp.where(mask, p, -1.0)
        best = jnp.max(pm, axis=-1, keepdims=True)
        idx = jnp.min(jnp.where(pm == best, lane, big), axis=-1, keepdims=True)
        return best, idx

    gmask = (lane >= N_EXPERTS) & (lane < N_EXPERTS + N_GROUPS)
    g_prob = softmax_masked(gmask)
    g_p, g_lane = top1(g_prob, gmask)
    e_lo = (g_lane - N_EXPERTS) * E_PER_GROUP
    emask = (lane >= e_lo) & (lane < e_lo + E_PER_GROUP)
    e_prob = softmax_masked(emask)
    p1, i1 = top1(e_prob, emask)
    p2, i2 = top1(e_prob, emask & (lane != i1))
    den = p1 + p2
    w1 = g_p * (p1 / den)
    w2 = g_p * (p2 / den)
    route_ref[0] = jnp.where(lane == 0, i1.astype(F32), jnp.where(lane == 1, i2.astype(F32),
                             jnp.where(lane == 2, w1, jnp.where(lane == 3, w2, 0.0))))
    cnt_ref[0, 0] = jnp.sum(((lane == i1) | (lane == i2)).astype(F32), axis=0, keepdims=True)


def _merge(attn, four, gates, x, wba, wbf, wout, gt1, sh2, sc2, gffn, wr, br, tm):
    bsz, s, d = x.shape
    const = lambda shape: pl.BlockSpec(shape, lambda b, i: (0,) * len(shape))
    tok = lambda w: pl.BlockSpec((1, tm, w), lambda b, i: (b, i, 0))
    per_b = pl.BlockSpec((1, 1, d), lambda b, i: (b, 0, 0))
    return pl.pallas_call(
        _merge_kernel,
        grid=(bsz, s // tm),
        in_specs=[tok(attn.shape[-1]), tok(F_WIDTH), tok(2 * d), tok(d), const(wba.shape),
                  const(wbf.shape), const(wout.shape), per_b, per_b, per_b, const((1, d)),
                  const(wr.shape), const((1, LANES))],
        out_specs=[tok(d), tok(d), tok(LANES),
                   pl.BlockSpec((1, 1, 1, LANES), lambda b, i: (b, i, 0, 0))],
        out_shape=[jax.ShapeDtypeStruct((bsz, s, d), F32), jax.ShapeDtypeStruct((bsz, s, d), BF16),
                   jax.ShapeDtypeStruct((bsz, s, LANES), F32),
                   jax.ShapeDtypeStruct((bsz, s // tm, 1, LANES), F32)],
        compiler_params=_cparams(("parallel", "parallel")),
        name="merge",
    )(attn, four, gates, x, wba, wbf, wout, gt1, sh2, sc2, gffn, wr, br)


GRAN = 16
MOE_ROW_TILE = 512


def _moe_metadata(cnt, tm):
    ntile = cnt.shape[0]

    def cumsum(a, axis):
        n = a.shape[axis]
        tri = jnp.asarray(np.triu(np.ones((n, n), np.float32)))
        af = a.astype(F32)
        out = (jnp.dot(af, tri, precision=lax.Precision.HIGHEST) if axis == a.ndim - 1
               else jnp.dot(tri.T, af, precision=lax.Precision.HIGHEST))
        return out.astype(jnp.int32)

    c = cnt[:, :N_EXPERTS].astype(jnp.int32)
    pc = (c + GRAN - 1) // GRAN * GRAN
    loff = cumsum(pc, 1) - pc
    tot = jnp.sum(pc, axis=0)
    tot_r = (tot + MOE_ROW_TILE - 1) // MOE_ROW_TILE * MOE_ROW_TILE
    ends = cumsum(tot_r[None, :], 1)[0]
    base = ends - tot_r
    goff = base[None, :] + cumsum(pc, 0) - pc
    cap_tiles = (2 * ntile * tm + ntile * N_EXPERTS * GRAN) // MOE_ROW_TILE + N_EXPERTS
    n_valid = (ends[-1] // MOE_ROW_TILE).astype(jnp.int32)
    tile_start = jnp.arange(cap_tiles, dtype=jnp.int32) * MOE_ROW_TILE
    tile_expert = jnp.minimum(jnp.sum((ends[None, :] <= tile_start[:, None]).astype(jnp.int32), axis=1),
                              N_EXPERTS - 1)
    flat = lambda a: a.reshape(-1).astype(jnp.int32)
    tail = tot_r - tot + jnp.where(jnp.arange(N_EXPERTS) == N_EXPERTS - 1,
                                   cap_tiles * MOE_ROW_TILE - ends[-1], 0)
    meta = dict(loff=flat(loff), goff=flat(goff), ngran=flat(pc // GRAN),
                ntot=flat(jnp.sum(pc, axis=1) // GRAN),
                tail_start=flat(base + tot), tail_n=flat(tail // GRAN),
                tile_expert=tile_expert, n_valid=n_valid.reshape(1))
    return meta, loff.astype(F32), cap_tiles


def _run_copies(i, loff_s, goff_s, ngran_s, make_copy):
    def per_expert(e, n_started):
        n = ngran_s[i * N_EXPERTS + e]
        lo = loff_s[i * N_EXPERTS + e]
        go = goff_s[i * N_EXPERTS + e]

        def per_granule(g, carry):
            make_copy(pl.multiple_of(lo + g * GRAN, GRAN), pl.multiple_of(go + g * GRAN, GRAN)).start()
            return carry

        lax.fori_loop(0, n, per_granule, 0)
        return n_started + n

    return lax.fori_loop(0, N_EXPERTS, per_expert, 0)


def _wait_copies(n, make_copy):
    def one(_, carry):
        make_copy(0, 0).wait()
        return carry
    lax.fori_loop(0, n, one, 0)


def _local_positions(route, loffv, ltri):
    lane = lax.broadcasted_iota(jnp.int32, route.shape, 1).astype(F32)
    o1 = lane == route[:, 0:1]
    o2 = lane == route[:, 1:2]
    before = jnp.dot(ltri, (o1 | o2).astype(BF16), preferred_element_type=F32)
    start = before + loffv
    pos1 = jnp.sum(jnp.where(o1, start, 0.0), axis=-1, keepdims=True)
    pos2 = jnp.sum(jnp.where(o2, start, 0.0), axis=-1, keepdims=True)
    return pos1, pos2


def _dispatch_kernel(loff_s, goff_s, ngran_s, ntot_s, tstart_s, tn_s, v2_ref, route_ref, loffv_ref,
                     ltri_ref, xs_ref, posw_ref, loc_ref, zero_ref, sem):
    i = pl.program_id(0)
    last = pl.num_programs(0) - 1
    slot = i % 2
    rows = loc_ref.shape[1]
    route = route_ref[...]
    pos1, pos2 = _local_positions(route, loffv_ref[0], ltri_ref[...])
    lane = lax.broadcasted_iota(jnp.int32, route.shape, 1)
    posw = jnp.where(lane == 0, pos1, jnp.where(lane == 1, pos2, route))
    posw_ref[...] = posw
    pos_t = posw.T
    riota = lax.broadcasted_iota(jnp.int32, (rows, 1), 0).astype(F32)
    onehot = ((riota == pos_t[0:1]) | (riota == pos_t[1:2])).astype(BF16)
    loc_ref[slot] = jnp.dot(onehot, v2_ref[...], preferred_element_type=F32).astype(BF16)

    def out_copy(s):
        return lambda lo, go: pltpu.make_async_copy(
            loc_ref.at[s, pl.ds(lo, GRAN)], xs_ref.at[pl.ds(go, GRAN)], sem.at[s])

    _run_copies(i, loff_s, goff_s, ngran_s, out_copy(slot))

    @pl.when(i > 0)
    def _():
        _wait_copies(ntot_s[i - 1], out_copy(1 - slot))

    @pl.when(i == last)
    def _():
        _wait_copies(ntot_s[i], out_copy(slot))
        zero_ref[...] = jnp.zeros(zero_ref.shape, BF16)

        def zero_copy(go):
            return pltpu.make_async_copy(zero_ref, xs_ref.at[pl.ds(go, GRAN)], sem.at[0])

        def per_expert(e, n_started):
            def per_granule(g, carry):
                zero_copy(pl.multiple_of(tstart_s[e] + g * GRAN, GRAN)).start()
                return carry
            lax.fori_loop(0, tn_s[e], per_granule, 0)
            return n_started + tn_s[e]

        nz = lax.fori_loop(0, N_EXPERTS, per_expert, 0)
        _wait_copies(nz, lambda lo, go: zero_copy(go))


def _dispatch(v2, route, meta, loffv, ltri, cap_tiles, tm, rows):
    t, d = v2.shape
    ntile = t // tm
    grid_spec = pltpu.PrefetchScalarGridSpec(
        num_scalar_prefetch=6,
        grid=(ntile,),
        in_specs=[pl.BlockSpec((tm, d), lambda i, *_: (i, 0)),
                  pl.BlockSpec((tm, LANES), lambda i, *_: (i, 0)),
                  pl.BlockSpec((1, 1, LANES), lambda i, *_: (i, 0, 0)),
                  pl.BlockSpec((tm, tm), lambda i, *_: (0, 0))],
        out_specs=[pl.BlockSpec(memory_space=pl.ANY),
                   pl.BlockSpec((tm, LANES), lambda i, *_: (i, 0))],
        scratch_shapes=[pltpu.VMEM((2, rows, d), BF16), pltpu.VMEM((GRAN, d), BF16),
                        pltpu.SemaphoreType.DMA((2,))],
    )
    return pl.pallas_call(
        _dispatch_kernel,
        grid_spec=grid_spec,
        out_shape=[jax.ShapeDtypeStruct((cap_tiles * MOE_ROW_TILE, d), BF16),
                   jax.ShapeDtypeStruct((t, LANES), F32)],
        compiler_params=_cparams(("arbitrary",)),
        name="dispatch",
    )(meta["loff"], meta["goff"], meta["ngran"], meta["ntot"], meta["tail_start"], meta["tail_n"],
      v2, route, loffv, ltri)


def _experts_kernel(texp_s, nvalid_s, x_ref, wgu_ref, wd_ref, y_ref):
    del texp_s

    @pl.when(pl.program_id(0) < nvalid_s[0])
    def _():
        hc = jnp.dot(x_ref[...], wgu_ref[0], preferred_element_type=F32)
        g = hc[:, :D_EXPERT]
        hid = (g * jax.nn.sigmoid(g) * hc[:, D_EXPERT:]).astype(BF16)
        y_ref[...] = jnp.dot(hid, wd_ref[0], preferred_element_type=F32).astype(BF16)

    @pl.when(pl.program_id(0) >= nvalid_s[0])
    def _():
        y_ref[...] = jnp.zeros(y_ref.shape, BF16)


def _experts(xs, wgu, wd, meta, cap_tiles):
    d = xs.shape[1]
    row = lambda j, texp, nv: (jnp.minimum(j, nv[0] - 1), 0)
    wsel = lambda j, texp, nv: (texp[jnp.minimum(j, nv[0] - 1)], 0, 0)
    grid_spec = pltpu.PrefetchScalarGridSpec(
        num_scalar_prefetch=2,
        grid=(cap_tiles,),
        in_specs=[pl.BlockSpec((MOE_ROW_TILE, d), row),
                  pl.BlockSpec((1, d, 2 * D_EXPERT), wsel),
                  pl.BlockSpec((1, D_EXPERT, d), wsel)],
        out_specs=pl.BlockSpec((MOE_ROW_TILE, d), lambda j, texp, nv: (j, 0)),
    )
    return pl.pallas_call(
        _experts_kernel,
        grid_spec=grid_spec,
        out_shape=jax.ShapeDtypeStruct(xs.shape, BF16),
        compiler_params=_cparams(("arbitrary",)),
        name="experts",
    )(meta["tile_expert"], meta["n_valid"], xs, wgu, wd)


def _combine_kernel(loff_s, goff_s, ngran_s, ntot_s, posw_ref, ys_ref, h1_ref, gt2_ref, shf_ref,
                    scf_ref, gfin_ref, o_ref, loc_ref, sem):
    i = pl.program_id(0)
    slot = i % 2
    rows = loc_ref.shape[1]

    def in_copy(s):
        return lambda lo, go: pltpu.make_async_copy(
            ys_ref.at[pl.ds(go, GRAN)], loc_ref.at[s, pl.ds(lo, GRAN)], sem.at[s])

    def fetch(tile, s):
        loc_ref[s] = jnp.zeros(loc_ref.shape[1:], BF16)
        _run_copies(tile, loff_s, goff_s, ngran_s, in_copy(s))

    @pl.when(i == 0)
    def _():
        fetch(0, 0)

    @pl.when(i + 1 < pl.num_programs(0))
    def _():
        fetch(i + 1, 1 - slot)

    _wait_copies(ntot_s[i], in_copy(slot))

    posw = posw_ref[...]
    riota = lax.broadcasted_iota(jnp.int32, (1, rows), 1).astype(F32)
    qw = (jnp.where(riota == posw[:, 0:1], posw[:, 2:3], 0.0)
          + jnp.where(riota == posw[:, 1:2], posw[:, 3:4], 0.0)).astype(BF16)
    moe = jnp.dot(qw, loc_ref[slot], preferred_element_type=F32)
    h2 = h1_ref[...] + gt2_ref[0] * moe
    o_ref[...] = _rms(h2, gfin_ref[...]) * (1.0 + scf_ref[0]) + shf_ref[0]


def _combine(posw, ys, h1, gt2, shf, scf, gfin, meta, tm, rows, tiles_per_batch):
    t, d = h1.shape
    per_b = pl.BlockSpec((1, 1, d), lambda i, *_: (i // tiles_per_batch, 0, 0))
    grid_spec = pltpu.PrefetchScalarGridSpec(
        num_scalar_prefetch=4,
        grid=(t // tm,),
        in_specs=[pl.BlockSpec((tm, LANES), lambda i, *_: (i, 0)),
                  pl.BlockSpec(memory_space=pl.ANY),
                  pl.BlockSpec((tm, d), lambda i, *_: (i, 0)),
                  per_b, per_b, per_b, pl.BlockSpec((1, d), lambda i, *_: (0, 0))],
        out_specs=pl.BlockSpec((tm, d), lambda i, *_: (i, 0)),
        scratch_shapes=[pltpu.VMEM((2, rows, d), BF16), pltpu.SemaphoreType.DMA((2,))],
    )
    return pl.pallas_call(
        _combine_kernel,
        grid_spec=grid_spec,
        out_shape=jax.ShapeDtypeStruct((t, d), F32),
        compiler_params=_cparams(("arbitrary",)),
        name="combine",
    )(meta["loff"], meta["goff"], meta["ngran"], meta["ntot"], posw, ys, h1, gt2, shf, scf, gfin)


def _rot_cols(w):
    half = ROPE // 2
    return jnp.concatenate([-w[..., half:], w[..., :half]], axis=-1)


def _pad_cols(w, lo, width):
    return jnp.pad(w, [(0, 0)] * (w.ndim - 1) + [(lo, width - lo - w.shape[-1])])


def _prep_weights(w_in, w_uq, w_uk, w_uv):
    o2, o3 = Q_LORA + KV_LORA, Q_LORA + KV_LORA + ROPE
    w_kr = w_in[:, o2:o3]
    wcat = jnp.concatenate([
        w_in[:, :o2],
        _pad_cols(w_kr, NOPE, HEAD_PAD), _pad_cols(_rot_cols(w_kr), NOPE, HEAD_PAD),
        w_in[:, o3:]], axis=1).astype(BF16)
    wq_a = _pad_cols(w_uq, 0, HEAD_PAD)
    wq_b = _pad_cols(_rot_cols(w_uq[..., NOPE:]), NOPE, HEAD_PAD)
    r = w_uq.shape[0]
    wqt = jnp.concatenate([wq_a.reshape(r, -1), wq_b.reshape(r, -1)], axis=1).T.astype(BF16)
    wk = _pad_cols(w_uk, 0, HEAD_PAD).reshape(r, -1).astype(BF16)
    odd = (jnp.arange(N_HEADS) % 2 == 1)[None, :, None]
    wv = jnp.where(odd, _pad_cols(w_uv, V_DIM, HEAD_PAD), _pad_cols(w_uv, 0, HEAD_PAD)).reshape(r, -1)
    wvt = wv.T.astype(BF16)
    ones = np.zeros((N_HEADS, HEAD_PAD), np.float32)
    ones[0::2, V_DIM] = 1.0
    ones[1::2, 0] = 1.0
    return wcat, wqt, wk, wvt, jnp.asarray(ones.reshape(-1, 1))


def kernel(x, c, positions, w_ada, b_ada, g_norm_mix, w_in, g_q_lat, g_kv_lat, w_uq, w_uk, w_uv, w_branch_attn, w_branch_fourier, b_gates, w_out, g_norm_ffn, w_router_group, b_router_group, w_router_expert, b_router_expert, w_expert_gate, w_expert_up, w_expert_down, w_ada_final, b_ada_final, g_norm_final):
    bsz, s, d = x.shape
    assert w_ada.shape[0] == 1 and s % FFT_N2 == 0
    tm = min(512, s)

    mod = _mods(c, w_ada[0], b_ada[0])
    fmod = _mods(c, w_ada_final, b_ada_final)
    sh1, sc1, gt1, sh2, sc2, gt2 = [m.reshape(bsz, 1, d) for m in jnp.split(mod, 6, axis=-1)]
    shf, scf = [m.reshape(bsz, 1, d) for m in jnp.split(fmod, 2, axis=-1)]

    wcat, wqt, wk, wvt, vones = _prep_weights(w_in[0], w_uq[0], w_uk[0], w_uv[0])
    bd, m1, wb = _dft_constants(s)
    inv_freq = ROPE_THETA ** (-jnp.arange(0, ROPE, 2, dtype=F32) / ROPE)
    invf = _pad_cols(jnp.concatenate([inv_freq, inv_freq])[None, :], NOPE, HEAD_PAD).reshape(HEAD_PAD, 1)

    qt, k, vt, zr, zi, gates = _inproj(
        x, positions.reshape(bsz, 1, s), sh1, sc1, g_norm_mix[0][None, :], wcat,
        g_q_lat[0][None, :], g_kv_lat[0][None, :], wqt, wk, wvt, vones, bd, b_gates[0][None, :],
        invf, tm)

    attn = _attention(qt, k, vt, tq=min(256, s))
    br, bi = _fft_a(zr, zi, m1, ts2=8)
    four = _fft_b(br, bi, wb, tk1=16)

    wr = jnp.concatenate([w_router_expert[0].reshape(d, N_EXPERTS), w_router_group[0]], axis=1)
    wr = _pad_cols(wr, 0, LANES)
    wr_hi = wr.astype(BF16)
    wr_lo = (wr - wr_hi.astype(F32)).astype(BF16)
    wr = jnp.concatenate([jnp.concatenate([wr_hi, wr_lo], axis=1),
                          jnp.concatenate([wr_hi, jnp.zeros_like(wr_lo)], axis=1)], axis=0)
    brt = _pad_cols(jnp.concatenate([b_router_expert[0].reshape(-1), b_router_group[0]])[None, :], 0, LANES)
    h1, v2, route, cnt = _merge(attn, four, gates, x, w_branch_attn[0].astype(BF16),
                                w_branch_fourier[0].astype(BF16), w_out[0].astype(BF16), gt1, sh2,
                                sc2, g_norm_ffn[0][None, :], wr, brt, tm)

    t = bsz * s
    meta, loffv, cap_tiles = _moe_metadata(cnt.reshape(-1, LANES), tm)
    rows = 2 * tm + N_EXPERTS * GRAN
    loffv = _pad_cols(loffv, 0, LANES).reshape(-1, 1, LANES)
    ltri = jnp.asarray(np.tril(np.ones((tm, tm), np.float32), -1), BF16)
    xs, posw = _dispatch(v2.reshape(t, d), route.reshape(t, LANES), meta, loffv, ltri, cap_tiles,
                         tm, rows)
    wgu = jnp.concatenate([w_expert_gate[0], w_expert_up[0]], axis=-1).astype(BF16)
    wd = w_expert_down[0].astype(BF16)
    ys = _experts(xs, wgu, wd, meta, cap_tiles)
    out = _combine(posw, ys, h1.reshape(t, d), gt2, shf, scf, g_norm_final[None, :], meta, tm, rows,
                   s // tm)
    return out.reshape(bsz, s, d)
```

```python
import functools

import numpy as np
import jax
import jax.numpy as jnp
from jax import lax
from jax.experimental import pallas as pl
from jax.experimental.pallas import tpu as pltpu

F32 = jnp.float32
BF16 = jnp.bfloat16

N_HEADS = 8
NOPE = 64
ROPE = 32
V_DIM = 64
HEAD_PAD = 128
V_ROWS = 80
Q_LORA = 256
KV_LORA = 256
ROPE_THETA = 10000.0
F_GROUPS = 8
F_GDIM = 64
F_WIDTH = F_GROUPS * F_GDIM
N_GROUPS = 4
E_PER_GROUP = 8
N_EXPERTS = N_GROUPS * E_PER_GROUP
D_EXPERT = 256
EPS = 1e-6
FFT_N2 = 64
LANES = 128
ATTN_DEPTH = 1
QK_ROWS = 256
VMEM_LIMIT = 56 * 1024 * 1024


def _cparams(sem):
    return pltpu.CompilerParams(dimension_semantics=sem, vmem_limit_bytes=VMEM_LIMIT)


def _mods_kernel(c_ref, w_ref, b_ref, o_ref):
    c = c_ref[...]
    ca = c * jax.nn.sigmoid(c)
    o_ref[...] = jnp.dot(ca, w_ref[...], preferred_element_type=F32,
                         precision=lax.Precision.HIGHEST) + b_ref[...]


def _mods(c, w, b):
    bsz, d = c.shape
    n = w.shape[1]
    tn = 1024
    return pl.pallas_call(
        _mods_kernel,
        grid=(n // tn,),
        in_specs=[pl.BlockSpec((bsz, d), lambda j: (0, 0)),
                  pl.BlockSpec((d, tn), lambda j: (0, j)),
                  pl.BlockSpec((1, tn), lambda j: (0, j))],
        out_specs=pl.BlockSpec((bsz, tn), lambda j: (0, j)),
        out_shape=jax.ShapeDtypeStruct((bsz, n), F32),
        compiler_params=_cparams(("arbitrary",)),
        name="mods",
    )(c, w, b.reshape(1, n))


def _rms(x, g):
    r = lax.rsqrt(jnp.mean(x * x, axis=-1, keepdims=True) + EPS)
    return x * r * g


def _inproj_kernel(x_ref, pos_ref, sh_ref, sc_ref, gmix_ref, wcat_ref, gq_ref, gkv_ref,
                   wqt_ref, wk_ref, wvt_ref, vones_ref, bd_ref, bg_ref, invf_ref,
                   qt_ref, k_ref, vt_ref, zr_ref, zi_ref, gate_ref, *, scale):
    hp = N_HEADS * HEAD_PAD
    nt_dims = (((1,), (1,)), ((), ()))
    u = _rms(x_ref[0], gmix_ref[...]) * (1.0 + sc_ref[0]) + sh_ref[0]
    ub = u.astype(BF16)

    o_q, o_kv, o_kr, o_f, o_g = 0, Q_LORA, Q_LORA + KV_LORA, Q_LORA + KV_LORA + 2 * HEAD_PAD, \
        Q_LORA + KV_LORA + 2 * HEAD_PAD + F_WIDTH

    ang_t = invf_ref[...] * pos_ref[0].astype(F32)
    cos_t = jnp.cos(ang_t)
    sin_t = jnp.sin(ang_t)

    ql = jnp.dot(ub, wcat_ref[:, o_q:o_kv], preferred_element_type=F32)
    kvl = jnp.dot(ub, wcat_ref[:, o_kv:o_kr], preferred_element_type=F32)
    kr = jnp.dot(ub, wcat_ref[:, o_kr:o_f], preferred_element_type=F32)
    fin = jnp.dot(ub, wcat_ref[:, o_f:o_g], preferred_element_type=F32).astype(BF16)
    zg = jnp.dot(ub, wcat_ref[:, o_g:], preferred_element_type=F32) + bg_ref[...]

    qn = _rms(ql, gq_ref[...]).astype(BF16)
    kvn = _rms(kvl, gkv_ref[...]).astype(BF16)
    zz = jnp.dot(fin, bd_ref[...], preferred_element_type=F32)
    qab_t = lax.dot_general(wqt_ref[...], qn, nt_dims, preferred_element_type=F32)
    kn = jnp.dot(kvn, wk_ref[...], preferred_element_type=F32)
    vt = lax.dot_general(wvt_ref[...], kvn, nt_dims, preferred_element_type=F32) + vones_ref[...]

    gate_ref[0] = jax.nn.sigmoid(zg).astype(BF16)
    zr_ref[0] = zz[:, :F_WIDTH].astype(BF16)
    zi_ref[0] = zz[:, F_WIDTH:].astype(BF16)
    cq = cos_t * scale
    sq = sin_t * scale
    for h in range(N_HEADS):
        a = qab_t[h * HEAD_PAD:(h + 1) * HEAD_PAD]
        b = qab_t[hp + h * HEAD_PAD:hp + (h + 1) * HEAD_PAD]
        qt_ref[0, h] = (a * cq + b * sq).astype(BF16)
    krr = kr[:, :HEAD_PAD] * cos_t.T + kr[:, HEAD_PAD:] * sin_t.T
    for h in range(N_HEADS):
        k_ref[0, h] = (kn[:, h * HEAD_PAD:(h + 1) * HEAD_PAD] + krr).astype(BF16)
        vt_ref[0, h, 0] = vt[h * V_ROWS:(h + 1) * V_ROWS].astype(BF16)


def _inproj(x, pos_row, sh1, sc1, gmix, wcat, gq, gkv, wqt, wk, wvt, vones, bd, bg, invf, tm):
    bsz, s, d = x.shape
    hp = N_HEADS * HEAD_PAD
    nt = s // tm
    const = lambda shape: pl.BlockSpec(shape, lambda b, i: (0,) * len(shape))
    tok = lambda w: pl.BlockSpec((1, tm, w), lambda b, i: (b, i, 0))
    per_b = pl.BlockSpec((1, 1, d), lambda b, i: (b, 0, 0))
    scale = float((NOPE + ROPE) ** -0.5 * np.log2(np.e))
    return pl.pallas_call(
        functools.partial(_inproj_kernel, scale=scale),
        grid=(bsz, nt),
        in_specs=[tok(d), pl.BlockSpec((1, 1, tm), lambda b, i: (b, 0, i)), per_b, per_b,
                  const((1, d)), const(wcat.shape), const((1, Q_LORA)), const((1, KV_LORA)),
                  const(wqt.shape), const(wk.shape), const(wvt.shape), const(vones.shape),
                  const(bd.shape), const((1, bg.shape[1])), const((HEAD_PAD, 1))],
        out_specs=[pl.BlockSpec((1, N_HEADS, HEAD_PAD, tm), lambda b, i: (b, 0, 0, i)),
                   pl.BlockSpec((1, N_HEADS, tm, HEAD_PAD), lambda b, i: (b, 0, i, 0)),
                   pl.BlockSpec((1, N_HEADS, 1, V_ROWS, tm), lambda b, i: (b, 0, i, 0, 0)),
                   tok(F_WIDTH), tok(F_WIDTH), tok(2 * d)],
        out_shape=[jax.ShapeDtypeStruct((bsz, N_HEADS, HEAD_PAD, s), BF16),
                   jax.ShapeDtypeStruct((bsz, N_HEADS, s, HEAD_PAD), BF16),
                   jax.ShapeDtypeStruct((bsz, N_HEADS, nt, V_ROWS, tm), BF16),
                   jax.ShapeDtypeStruct((bsz, s, F_WIDTH), BF16),
                   jax.ShapeDtypeStruct((bsz, s, F_WIDTH), BF16),
                   jax.ShapeDtypeStruct((bsz, s, 2 * d), BF16)],
        compiler_params=_cparams(("parallel", "parallel")),
        name="inproj",
    )(x, pos_row, sh1, sc1, gmix, wcat, gq, gkv, wqt, wk, wvt, vones, bd, bg, invf)


def _attn_kernel(qt_ref, k_ref, vt_ref, o_ref, *p_refs, nk):
    tq = qt_ref.shape[3]
    tk = vt_ref.shape[4]
    qts = [qt_ref[0, hh] for hh in range(2)]
    nbuf = ATTN_DEPTH + 1

    def scores(c, hh):
        ss = []
        for i in range(tk // QK_ROWS):
            off = c * tk + i * QK_ROWS
            ss.append(jnp.dot(k_ref[0, hh, off:off + QK_ROWS, :], qts[hh],
                              preferred_element_type=F32))
        return ss

    def probs(ss, m, p_ref):
        m_new = m
        for s in ss:
            m_new = jnp.maximum(m_new, jnp.max(s, axis=0, keepdims=True))
        for i, s in enumerate(ss):
            p_ref[i * QK_ROWS:(i + 1) * QK_ROWS, :] = jnp.exp2(s - m_new).astype(BF16)
        return m_new, jnp.exp2(m - m_new)

    def accumulate(c, hh, acc, alpha, p_ref):
        return alpha * acc + jnp.dot(vt_ref[0, hh, c], p_ref[...], preferred_element_type=F32)

    hs = range(2)
    buf = lambda c, hh: p_refs[hh * nbuf + c % nbuf]
    m = [jnp.full((1, tq), -jnp.inf, F32) for _ in hs]
    alpha = [[None] * nk for _ in hs]
    accs = [jnp.zeros((V_ROWS, tq), F32) for _ in hs]
    for c in range(nk + ATTN_DEPTH):
        if c < nk:
            s = [scores(c, hh) for hh in hs]
        if c >= ATTN_DEPTH:
            for hh in hs:
                accs[hh] = accumulate(c - ATTN_DEPTH, hh, accs[hh], alpha[hh][c - ATTN_DEPTH],
                                      buf(c - ATTN_DEPTH, hh))
        if c < nk:
            for hh in hs:
                m[hh], alpha[hh][c] = probs(s[hh], m[hh], buf(c, hh))
    out_t = jnp.concatenate([a[:V_DIM] / a[V_DIM:V_DIM + 1] for a in accs], axis=0)
    o_ref[0] = out_t.T.astype(BF16)


def _attention(qt, k, vt, tq):
    bsz, nh, s, _ = k.shape
    nk, tk = vt.shape[2], vt.shape[4]
    return pl.pallas_call(
        functools.partial(_attn_kernel, nk=nk),
        grid=(bsz, nh // 2, s // tq),
        in_specs=[pl.BlockSpec((1, 2, HEAD_PAD, tq), lambda b, j, i: (b, j, 0, i)),
                  pl.BlockSpec((1, 2, s, HEAD_PAD), lambda b, j, i: (b, j, 0, 0)),
                  pl.BlockSpec((1, 2, nk, V_ROWS, tk), lambda b, j, i: (b, j, 0, 0, 0))],
        out_specs=pl.BlockSpec((1, tq, HEAD_PAD), lambda b, j, i: (b, i, j)),
        out_shape=jax.ShapeDtypeStruct((bsz, s, nh * V_DIM), BF16),
        scratch_shapes=[pltpu.VMEM((tk, tq), BF16)] * (2 * (ATTN_DEPTH + 1)),
        compiler_params=_cparams(("parallel", "parallel", "arbitrary")),
        name="attention",
    )(qt, k, vt)


def _fft_a_kernel(zr_ref, zi_ref, m1_ref, br_ref, bi_ref):
    n1 = zr_ref.shape[1]
    ts2 = m1_ref.shape[0]
    for j in range(ts2):
        sl = slice(j * F_WIDTH, (j + 1) * F_WIDTH)
        xx = jnp.concatenate([zr_ref[0, :, sl], zi_ref[0, :, sl]], axis=0)
        y = jnp.dot(m1_ref[j], xx, preferred_element_type=F32)
        br_ref[0, :, sl] = y[:n1].astype(BF16)
        bi_ref[0, :, sl] = y[n1:].astype(BF16)


def _fft_a(zr, zi, m1, ts2):
    bsz, s, _ = zr.shape
    n1 = s // FFT_N2
    zr2 = zr.reshape(bsz, n1, FFT_N2 * F_WIDTH)
    zi2 = zi.reshape(bsz, n1, FFT_N2 * F_WIDTH)
    blk = pl.BlockSpec((1, n1, ts2 * F_WIDTH), lambda b, j: (b, 0, j))
    return pl.pallas_call(
        _fft_a_kernel,
        grid=(bsz, FFT_N2 // ts2),
        in_specs=[blk, blk, pl.BlockSpec((ts2, 2 * n1, 2 * n1), lambda b, j: (j, 0, 0))],
        out_specs=[blk, blk],
        out_shape=[jax.ShapeDtypeStruct(zr2.shape, BF16)] * 2,
        compiler_params=_cparams(("parallel", "parallel")),
        name="fft_a",
    )(zr2, zi2, m1)


def _fft_b_kernel(br_ref, bi_ref, wb_ref, o_ref):
    tk1 = br_ref.shape[1]
    for j in range(tk1):
        xx = jnp.concatenate([br_ref[0, j], bi_ref[0, j]], axis=0)
        y = jnp.dot(wb_ref[...], xx, preferred_element_type=F32)
        o_ref[0, :, j * F_WIDTH:(j + 1) * F_WIDTH] = y.astype(BF16)


def _fft_b(br, bi, wb, tk1):
    bsz, n1, _ = br.shape
    br4 = br.reshape(bsz, n1, FFT_N2, F_WIDTH)
    bi4 = bi.reshape(bsz, n1, FFT_N2, F_WIDTH)
    blk = pl.BlockSpec((1, tk1, FFT_N2, F_WIDTH), lambda b, j: (b, j, 0, 0))
    out = pl.pallas_call(
        _fft_b_kernel,
        grid=(bsz, n1 // tk1),
        in_specs=[blk, blk, pl.BlockSpec((FFT_N2, 2 * FFT_N2), lambda b, j: (0, 0))],
        out_specs=pl.BlockSpec((1, FFT_N2, tk1 * F_WIDTH), lambda b, j: (b, 0, j)),
        out_shape=jax.ShapeDtypeStruct((bsz, FFT_N2, n1 * F_WIDTH), BF16),
        compiler_params=_cparams(("parallel", "parallel")),
        name="fft_b",
    )(br4, bi4, wb)
    return out.reshape(bsz, n1 * FFT_N2, F_WIDTH)


def _dft_constants(s):
    n1, n2 = s // FFT_N2, FFT_N2
    m = np.arange(F_GDIM)
    th = 2.0 * np.pi * np.outer(m, m) / F_GDIM
    eye = np.eye(F_GROUPS)
    bd = np.concatenate([np.kron(eye, np.cos(th)), -np.kron(eye, np.sin(th))], axis=1) / np.sqrt(F_GDIM)
    k1 = np.arange(n1)[None, :, None]
    s1 = np.arange(n1)[None, None, :]
    s2 = np.arange(n2)[:, None, None]
    ph = 2.0 * np.pi * ((k1 * (n2 * s1 + s2)) % s) / s
    cr, ci = np.cos(ph) / np.sqrt(n1), -np.sin(ph) / np.sqrt(n1)
    m1 = np.concatenate([np.concatenate([cr, -ci], axis=2), np.concatenate([ci, cr], axis=2)], axis=1)
    k2 = np.arange(n2)
    tb = 2.0 * np.pi * np.outer(k2, k2) / n2
    wb = np.concatenate([np.cos(tb), np.sin(tb)], axis=1) / np.sqrt(n2)
    return (jnp.asarray(bd, BF16), jnp.asarray(m1, BF16), jnp.asarray(wb, BF16))


def _merge_kernel(at_ref, fo_ref, gate_ref, x_ref, wba_ref, wbf_ref, wout_ref, gt1_ref, sh2_ref,
                  sc2_ref, gffn_ref, wr_ref, br_ref, h1_ref, v2_ref, route_ref, cnt_ref):
    d = x_ref.shape[-1]
    a = jnp.dot(at_ref[0], wba_ref[...], preferred_element_type=F32)
    fo = jnp.dot(fo_ref[0], wbf_ref[...], preferred_element_type=F32)
    merged = gate_ref[0, :, :d].astype(F32) * a + gate_ref[0, :, d:].astype(F32) * fo
    y = jnp.dot(merged.astype(BF16), wout_ref[...], preferred_element_type=F32)
    h1 = x_ref[0] + gt1_ref[0] * y
    h1_ref[0] = h1
    v2 = _rms(h1, gffn_ref[...]) * (1.0 + sc2_ref[0]) + sh2_ref[0]
    v_hi = v2.astype(BF16)
    v2_ref[0] = v_hi

    v_lo = (v2 - v_hi.astype(F32)).astype(BF16)
    parts = jnp.dot(jnp.concatenate([v_hi, v_lo], axis=1), wr_ref[...], preferred_element_type=F32)
    logit = parts[:, :LANES] + parts[:, LANES:] + br_ref[...]
    lane = lax.broadcasted_iota(jnp.int32, logit.shape, 1)
    neg = -jnp.inf
    big = jnp.int32(LANES)

    def softmax_masked(mask):
        z = jnp.where(mask, logit, neg)
        e = jnp.exp(z - jnp.max(z, axis=-1, keepdims=True))
        return e / jnp.sum(e, axis=-1, keepdims=True)

    def top1(p, mask):
        pm = jnp.where(mask, p, -1.0)
        best = jnp.max(pm, axis=-1, keepdims=True)
        idx = jnp.min(jnp.where(pm == best, lane, big), axis=-1, keepdims=True)
        return best, idx

    gmask = (lane >= N_EXPERTS) & (lane < N_EXPERTS + N_GROUPS)
    g_prob = softmax_masked(gmask)
    g_p, g_lane = top1(g_prob, gmask)
    e_lo = (g_lane - N_EXPERTS) * E_PER_GROUP
    emask = (lane >= e_lo) & (lane < e_lo + E_PER_GROUP)
    e_prob = softmax_masked(emask)
    p1, i1 = top1(e_prob, emask)
    p2, i2 = top1(e_prob, emask & (lane != i1))
    den = p1 + p2
    w1 = g_p * (p1 / den)
    w2 = g_p * (p2 / den)
    route_ref[0] = jnp.where(lane == 0, i1.astype(F32), jnp.where(lane == 1, i2.astype(F32),
                             jnp.where(lane == 2, w1, jnp.where(lane == 3, w2, 0.0))))
    cnt_ref[0, 0] = jnp.sum(((lane == i1) | (lane == i2)).astype(F32), axis=0, keepdims=True)


def _merge(attn, four, gates, x, wba, wbf, wout, gt1, sh2, sc2, gffn, wr, br, tm):
    bsz, s, d = x.shape
    const = lambda shape: pl.BlockSpec(shape, lambda b, i: (0,) * len(shape))
    tok = lambda w: pl.BlockSpec((1, tm, w), lambda b, i: (b, i, 0))
    per_b = pl.BlockSpec((1, 1, d), lambda b, i: (b, 0, 0))
    return pl.pallas_call(
        _merge_kernel,
        grid=(bsz, s // tm),
        in_specs=[tok(attn.shape[-1]), tok(F_WIDTH), tok(2 * d), tok(d), const(wba.shape),
                  const(wbf.shape), const(wout.shape), per_b, per_b, per_b, const((1, d)),
                  const(wr.shape), const((1, LANES))],
        out_specs=[tok(d), tok(d), tok(LANES),
                   pl.BlockSpec((1, 1, 1, LANES), lambda b, i: (b, i, 0, 0))],
        out_shape=[jax.ShapeDtypeStruct((bsz, s, d), F32), jax.ShapeDtypeStruct((bsz, s, d), BF16),
                   jax.ShapeDtypeStruct((bsz, s, LANES), F32),
                   jax.ShapeDtypeStruct((bsz, s // tm, 1, LANES), F32)],
        compiler_params=_cparams(("parallel", "parallel")),
        name="merge",
    )(attn, four, gates, x, wba, wbf, wout, gt1, sh2, sc2, gffn, wr, br)


GRAN = 16
MOE_ROW_TILE = 512


def _moe_metadata(cnt, tm):
    ntile = cnt.shape[0]

    def cumsum(a, axis):
        n = a.shape[axis]
        tri = jnp.asarray(np.triu(np.ones((n, n), np.float32)))
        af = a.astype(F32)
        out = (jnp.dot(af, tri, precision=lax.Precision.HIGHEST) if axis == a.ndim - 1
               else jnp.dot(tri.T, af, precision=lax.Precision.HIGHEST))
        return out.astype(jnp.int32)

    c = cnt[:, :N_EXPERTS].astype(jnp.int32)
    pc = (c + GRAN - 1) // GRAN * GRAN
    loff = cumsum(pc, 1) - pc
    tot = jnp.sum(pc, axis=0)
    tot_r = (tot + MOE_ROW_TILE - 1) // MOE_ROW_TILE * MOE_ROW_TILE
    ends = cumsum(tot_r[None, :], 1)[0]
    base = ends - tot_r
    goff = base[None, :] + cumsum(pc, 0) - pc
    cap_tiles = (2 * ntile * tm + ntile * N_EXPERTS * GRAN) // MOE_ROW_TILE + N_EXPERTS
    n_valid = (ends[-1] // MOE_ROW_TILE).astype(jnp.int32)
    tile_start = jnp.arange(cap_tiles, dtype=jnp.int32) * MOE_ROW_TILE
    tile_expert = jnp.minimum(jnp.sum((ends[None, :] <= tile_start[:, None]).astype(jnp.int32), axis=1),
                              N_EXPERTS - 1)
    flat = lambda a: a.reshape(-1).astype(jnp.int32)
    tail = tot_r - tot + jnp.where(jnp.arange(N_EXPERTS) == N_EXPERTS - 1,
                                   cap_tiles * MOE_ROW_TILE - ends[-1], 0)
    meta = dict(loff=flat(loff), goff=flat(goff), ngran=flat(pc // GRAN),
                ntot=flat(jnp.sum(pc, axis=1) // GRAN),
                tail_start=flat(base + tot), tail_n=flat(tail // GRAN),
                tile_expert=tile_expert, n_valid=n_valid.reshape(1))
    return meta, loff.astype(F32), cap_tiles


def _run_copies(i, loff_s, goff_s, ngran_s, make_copy):
    def per_expert(e, n_started):
        n = ngran_s[i * N_EXPERTS + e]
        lo = loff_s[i * N_EXPERTS + e]
        go = goff_s[i * N_EXPERTS + e]

        def per_granule(g, carry):
            make_copy(pl.multiple_of(lo + g * GRAN, GRAN), pl.multiple_of(go + g * GRAN, GRAN)).start()
            return carry

        lax.fori_loop(0, n, per_granule, 0)
        return n_started + n

    return lax.fori_loop(0, N_EXPERTS, per_expert, 0)


def _wait_copies(n, make_copy):
    def one(_, carry):
        make_copy(0, 0).wait()
        return carry
    lax.fori_loop(0, n, one, 0)


def _local_positions(route, loffv, ltri):
    lane = lax.broadcasted_iota(jnp.int32, route.shape, 1).astype(F32)
    o1 = lane == route[:, 0:1]
    o2 = lane == route[:, 1:2]
    before = jnp.dot(ltri, (o1 | o2).astype(BF16), preferred_element_type=F32)
    start = before + loffv
    pos1 = jnp.sum(jnp.where(o1, start, 0.0), axis=-1, keepdims=True)
    pos2 = jnp.sum(jnp.where(o2, start, 0.0), axis=-1, keepdims=True)
    return pos1, pos2


def _dispatch_kernel(loff_s, goff_s, ngran_s, ntot_s, tstart_s, tn_s, v2_ref, route_ref, loffv_ref,
                     ltri_ref, xs_ref, posw_ref, loc_ref, zero_ref, sem):
    i = pl.program_id(0)
    last = pl.num_programs(0) - 1
    slot = i % 2
    rows = loc_ref.shape[1]
    route = route_ref[...]
    pos1, pos2 = _local_positions(route, loffv_ref[0], ltri_ref[...])
    lane = lax.broadcasted_iota(jnp.int32, route.shape, 1)
    posw = jnp.where(lane == 0, pos1, jnp.where(lane == 1, pos2, route))
    posw_ref[...] = posw
    pos_t = posw.T
    riota = lax.broadcasted_iota(jnp.int32, (rows, 1), 0).astype(F32)
    onehot = ((riota == pos_t[0:1]) | (riota == pos_t[1:2])).astype(BF16)
    loc_ref[slot] = jnp.dot(onehot, v2_ref[...], preferred_element_type=F32).astype(BF16)

    def out_copy(s):
        return lambda lo, go: pltpu.make_async_copy(
            loc_ref.at[s, pl.ds(lo, GRAN)], xs_ref.at[pl.ds(go, GRAN)], sem.at[s])

    _run_copies(i, loff_s, goff_s, ngran_s, out_copy(slot))

    @pl.when(i > 0)
    def _():
        _wait_copies(ntot_s[i - 1], out_copy(1 - slot))

    @pl.when(i == last)
    def _():
        _wait_copies(ntot_s[i], out_copy(slot))
        zero_ref[...] = jnp.zeros(zero_ref.shape, BF16)

        def zero_copy(go):
            return pltpu.make_async_copy(zero_ref, xs_ref.at[pl.ds(go, GRAN)], sem.at[0])

        def per_expert(e, n_started):
            def per_granule(g, carry):
                zero_copy(pl.multiple_of(tstart_s[e] + g * GRAN, GRAN)).start()
                return carry
            lax.fori_loop(0, tn_s[e], per_granule, 0)
            return n_started + tn_s[e]

        nz = lax.fori_loop(0, N_EXPERTS, per_expert, 0)
        _wait_copies(nz, lambda lo, go: zero_copy(go))


def _dispatch(v2, route, meta, loffv, ltri, cap_tiles, tm, rows):
    t, d = v2.shape
    ntile = t // tm
    grid_spec = pltpu.PrefetchScalarGridSpec(
        num_scalar_prefetch=6,
        grid=(ntile,),
        in_specs=[pl.BlockSpec((tm, d), lambda i, *_: (i, 0)),
                  pl.BlockSpec((tm, LANES), lambda i, *_: (i, 0)),
                  pl.BlockSpec((1, 1, LANES), lambda i, *_: (i, 0, 0)),
                  pl.BlockSpec((tm, tm), lambda i, *_: (0, 0))],
        out_specs=[pl.BlockSpec(memory_space=pl.ANY),
                   pl.BlockSpec((tm, LANES), lambda i, *_: (i, 0))],
        scratch_shapes=[pltpu.VMEM((2, rows, d), BF16), pltpu.VMEM((GRAN, d), BF16),
                        pltpu.SemaphoreType.DMA((2,))],
    )
    return pl.pallas_call(
        _dispatch_kernel,
        grid_spec=grid_spec,
        out_shape=[jax.ShapeDtypeStruct((cap_tiles * MOE_ROW_TILE, d), BF16),
                   jax.ShapeDtypeStruct((t, LANES), F32)],
        compiler_params=_cparams(("arbitrary",)),
        name="dispatch",
    )(meta["loff"], meta["goff"], meta["ngran"], meta["ntot"], meta["tail_start"], meta["tail_n"],
      v2, route, loffv, ltri)


def _experts_kernel(texp_s, nvalid_s, x_ref, wgu_ref, wd_ref, y_ref):
    del texp_s

    @pl.when(pl.program_id(0) < nvalid_s[0])
    def _():
        hc = jnp.dot(x_ref[...], wgu_ref[0], preferred_element_type=F32)
        g = hc[:, :D_EXPERT]
        hid = (g * jax.nn.sigmoid(g) * hc[:, D_EXPERT:]).astype(BF16)
        y_ref[...] = jnp.dot(hid, wd_ref[0], preferred_element_type=F32).astype(BF16)

    @pl.when(pl.program_id(0) >= nvalid_s[0])
    def _():
        y_ref[...] = jnp.zeros(y_ref.shape, BF16)


def _experts(xs, wgu, wd, meta, cap_tiles):
    d = xs.shape[1]
    row = lambda j, texp, nv: (jnp.minimum(j, nv[0] - 1), 0)
    wsel = lambda j, texp, nv: (texp[jnp.minimum(j, nv[0] - 1)], 0, 0)
    grid_spec = pltpu.PrefetchScalarGridSpec(
        num_scalar_prefetch=2,
        grid=(cap_tiles,),
        in_specs=[pl.BlockSpec((MOE_ROW_TILE, d), row),
                  pl.BlockSpec((1, d, 2 * D_EXPERT), wsel),
                  pl.BlockSpec((1, D_EXPERT, d), wsel)],
        out_specs=pl.BlockSpec((MOE_ROW_TILE, d), lambda j, texp, nv: (j, 0)),
    )
    return pl.pallas_call(
        _experts_kernel,
        grid_spec=grid_spec,
        out_shape=jax.ShapeDtypeStruct(xs.shape, BF16),
        compiler_params=_cparams(("arbitrary",)),
        name="experts",
    )(meta["tile_expert"], meta["n_valid"], xs, wgu, wd)


def _combine_kernel(loff_s, goff_s, ngran_s, ntot_s, posw_ref, ys_ref, h1_ref, gt2_ref, shf_ref,
                    scf_ref, gfin_ref, o_ref, loc_ref, sem):
    i = pl.program_id(0)
    slot = i % 2
    rows = loc_ref.shape[1]

    def in_copy(s):
        return lambda lo, go: pltpu.make_async_copy(
            ys_ref.at[pl.ds(go, GRAN)], loc_ref.at[s, pl.ds(lo, GRAN)], sem.at[s])

    def fetch(tile, s):
        loc_ref[s] = jnp.zeros(loc_ref.shape[1:], BF16)
        _run_copies(tile, loff_s, goff_s, ngran_s, in_copy(s))

    @pl.when(i == 0)
    def _():
        fetch(0, 0)

    @pl.when(i + 1 < pl.num_programs(0))
    def _():
        fetch(i + 1, 1 - slot)

    _wait_copies(ntot_s[i], in_copy(slot))

    posw = posw_ref[...]
    riota = lax.broadcasted_iota(jnp.int32, (1, rows), 1).astype(F32)
    qw = (jnp.where(riota == posw[:, 0:1], posw[:, 2:3], 0.0)
          + jnp.where(riota == posw[:, 1:2], posw[:, 3:4], 0.0)).astype(BF16)
    moe = jnp.dot(qw, loc_ref[slot], preferred_element_type=F32)
    h2 = h1_ref[...] + gt2_ref[0] * moe
    o_ref[...] = _rms(h2, gfin_ref[...]) * (1.0 + scf_ref[0]) + shf_ref[0]


def _combine(posw, ys, h1, gt2, shf, scf, gfin, meta, tm, rows, tiles_per_batch):
    t, d = h1.shape
    per_b = pl.BlockSpec((1, 1, d), lambda i, *_: (i // tiles_per_batch, 0, 0))
    grid_spec = pltpu.PrefetchScalarGridSpec(
        num_scalar_prefetch=4,
        grid=(t // tm,),
        in_specs=[pl.BlockSpec((tm, LANES), lambda i, *_: (i, 0)),
                  pl.BlockSpec(memory_space=pl.ANY),
                  pl.BlockSpec((tm, d), lambda i, *_: (i, 0)),
                  per_b, per_b, per_b, pl.BlockSpec((1, d), lambda i, *_: (0, 0))],
        out_specs=pl.BlockSpec((tm, d), lambda i, *_: (i, 0)),
        scratch_shapes=[pltpu.VMEM((2, rows, d), BF16), pltpu.SemaphoreType.DMA((2,))],
    )
    return pl.pallas_call(
        _combine_kernel,
        grid_spec=grid_spec,
        out_shape=jax.ShapeDtypeStruct((t, d), F32),
        compiler_params=_cparams(("arbitrary",)),
        name="combine",
    )(meta["loff"], meta["goff"], meta["ngran"], meta["ntot"], posw, ys, h1, gt2, shf, scf, gfin)


def _rot_cols(w):
    half = ROPE // 2
    return jnp.concatenate([-w[..., half:], w[..., :half]], axis=-1)


def _pad_cols(w, lo, width):
    return jnp.pad(w, [(0, 0)] * (w.ndim - 1) + [(lo, width - lo - w.shape[-1])])


def _prep_weights(w_in, w_uq, w_uk, w_uv):
    o2, o3 = Q_LORA + KV_LORA, Q_LORA + KV_LORA + ROPE
    w_kr = w_in[:, o2:o3]
    wcat = jnp.concatenate([
        w_in[:, :o2],
        _pad_cols(w_kr, NOPE, HEAD_PAD), _pad_cols(_rot_cols(w_kr), NOPE, HEAD_PAD),
        w_in[:, o3:]], axis=1).astype(BF16)
    wq_a = _pad_cols(w_uq, 0, HEAD_PAD)
    wq_b = _pad_cols(_rot_cols(w_uq[..., NOPE:]), NOPE, HEAD_PAD)
    r = w_uq.shape[0]
    wqt = jnp.concatenate([wq_a.reshape(r, -1), wq_b.reshape(r, -1)], axis=1).T.astype(BF16)
    wk = _pad_cols(w_uk, 0, HEAD_PAD).reshape(r, -1).astype(BF16)
    wvt = _pad_cols(w_uv, 0, V_ROWS).reshape(r, -1).T.astype(BF16)
    ones = np.zeros((N_HEADS, V_ROWS), np.float32)
    ones[:, V_DIM] = 1.0
    return wcat, wqt, wk, wvt, jnp.asarray(ones.reshape(-1, 1))


def kernel(x, c, positions, w_ada, b_ada, g_norm_mix, w_in, g_q_lat, g_kv_lat, w_uq, w_uk, w_uv, w_branch_attn, w_branch_fourier, b_gates, w_out, g_norm_ffn, w_router_group, b_router_group, w_router_expert, b_router_expert, w_expert_gate, w_expert_up, w_expert_down, w_ada_final, b_ada_final, g_norm_final):
    bsz, s, d = x.shape
    assert w_ada.shape[0] == 1 and s % FFT_N2 == 0
    tm = min(512, s)

    mod = _mods(c, w_ada[0], b_ada[0])
    fmod = _mods(c, w_ada_final, b_ada_final)
    sh1, sc1, gt1, sh2, sc2, gt2 = [m.reshape(bsz, 1, d) for m in jnp.split(mod, 6, axis=-1)]
    shf, scf = [m.reshape(bsz, 1, d) for m in jnp.split(fmod, 2, axis=-1)]

    wcat, wqt, wk, wvt, vones = _prep_weights(w_in[0], w_uq[0], w_uk[0], w_uv[0])
    bd, m1, wb = _dft_constants(s)
    inv_freq = ROPE_THETA ** (-jnp.arange(0, ROPE, 2, dtype=F32) / ROPE)
    invf = _pad_cols(jnp.concatenate([inv_freq, inv_freq])[None, :], NOPE, HEAD_PAD).reshape(HEAD_PAD, 1)

    qt, k, vt, zr, zi, gates = _inproj(
        x, positions.reshape(bsz, 1, s), sh1, sc1, g_norm_mix[0][None, :], wcat,
        g_q_lat[0][None, :], g_kv_lat[0][None, :], wqt, wk, wvt, vones, bd, b_gates[0][None, :],
        invf, tm)

    attn = _attention(qt, k, vt, tq=min(256, s))
    br, bi = _fft_a(zr, zi, m1, ts2=8)
    four = _fft_b(br, bi, wb, tk1=16)

    wr = jnp.concatenate([w_router_expert[0].reshape(d, N_EXPERTS), w_router_group[0]], axis=1)
    wr = _pad_cols(wr, 0, LANES)
    wr_top = lax.bitcast_convert_type(
        lax.bitcast_convert_type(wr, jnp.uint32) & jnp.uint32(0xFFFF0000), F32)
    wr_hi = wr_top.astype(BF16)
    wr_lo = (wr - wr_top).astype(BF16)
    wr = jnp.concatenate([jnp.concatenate([wr_hi, wr_lo], axis=1),
                          jnp.concatenate([wr_hi, jnp.zeros_like(wr_lo)], axis=1)], axis=0)
    brt = _pad_cols(jnp.concatenate([b_router_expert[0].reshape(-1), b_router_group[0]])[None, :], 0, LANES)
    h1, v2, route, cnt = _merge(attn, four, gates, x, w_branch_attn[0].astype(BF16),
                                w_branch_fourier[0].astype(BF16), w_out[0].astype(BF16), gt1, sh2,
                                sc2, g_norm_ffn[0][None, :], wr, brt, tm)

    t = bsz * s
    meta, loffv, cap_tiles = _moe_metadata(cnt.reshape(-1, LANES), tm)
    rows = 2 * tm + N_EXPERTS * GRAN
    loffv = _pad_cols(loffv, 0, LANES).reshape(-1, 1, LANES)
    ltri = jnp.asarray(np.tril(np.ones((tm, tm), np.float32), -1), BF16)
    xs, posw = _dispatch(v2.reshape(t, d), route.reshape(t, LANES), meta, loffv, ltri, cap_tiles,
                         tm, rows)
    wgu = jnp.concatenate([w_expert_gate[0], w_expert_up[0]], axis=-1).astype(BF16)
    wd = w_expert_down[0].astype(BF16)
    ys = _experts(xs, wgu, wd, meta, cap_tiles)
    out = _combine(posw, ys, h1.reshape(t, d), gt2, shf, scf, g_norm_final[None, :], meta, tm, rows,
                   s // tm)
    return out.reshape(bsz, s, d)
```

```python
import functools

import numpy as np
import jax
import jax.numpy as jnp
from jax import lax
from jax.experimental import pallas as pl
from jax.experimental.pallas import tpu as pltpu

F32 = jnp.float32
BF16 = jnp.bfloat16

N_HEADS = 8
NOPE = 64
ROPE = 32
V_DIM = 64
HEAD_PAD = 128
V_ROWS = 80
Q_LORA = 256
KV_LORA = 256
ROPE_THETA = 10000.0
F_GROUPS = 8
F_GDIM = 64
F_WIDTH = F_GROUPS * F_GDIM
N_GROUPS = 4
E_PER_GROUP = 8
N_EXPERTS = N_GROUPS * E_PER_GROUP
D_EXPERT = 256
EPS = 1e-6
FFT_N2 = 64
LANES = 128
MERGE_ROWS = 128
ATTN_DEPTH = 2
QK_ROWS = 256
VMEM_LIMIT = 56 * 1024 * 1024


def _cparams(sem):
    return pltpu.CompilerParams(dimension_semantics=sem, vmem_limit_bytes=VMEM_LIMIT)


def _mods_kernel(c_ref, w_ref, b_ref, o_ref):
    c = c_ref[...]
    ca = c * jax.nn.sigmoid(c)
    o_ref[...] = jnp.dot(ca, w_ref[...], preferred_element_type=F32,
                         precision=lax.Precision.HIGHEST) + b_ref[...]


def _mods(c, w, b):
    bsz, d = c.shape
    n = w.shape[1]
    tn = 1024
    return pl.pallas_call(
        _mods_kernel,
        grid=(n // tn,),
        in_specs=[pl.BlockSpec((bsz, d), lambda j: (0, 0)),
                  pl.BlockSpec((d, tn), lambda j: (0, j)),
                  pl.BlockSpec((1, tn), lambda j: (0, j))],
        out_specs=pl.BlockSpec((bsz, tn), lambda j: (0, j)),
        out_shape=jax.ShapeDtypeStruct((bsz, n), F32),
        compiler_params=_cparams(("arbitrary",)),
        name="mods",
    )(c, w, b.reshape(1, n))


def _rms(x, g):
    r = lax.rsqrt(jnp.mean(x * x, axis=-1, keepdims=True) + EPS)
    return x * r * g


def _inproj_kernel(x_ref, pos_ref, sh_ref, sc_ref, gmix_ref, wcat_ref, gq_ref, gkv_ref,
                   wqt_ref, wk_ref, wvt_ref, vones_ref, bd_ref, bg_ref, invf_ref,
                   qt_ref, k_ref, vt_ref, zr_ref, zi_ref, gate_ref, *, scale):
    hp = N_HEADS * HEAD_PAD
    nt_dims = (((1,), (1,)), ((), ()))
    u = _rms(x_ref[0], gmix_ref[...]) * (1.0 + sc_ref[0]) + sh_ref[0]
    ub = u.astype(BF16)

    o_q, o_kv, o_kr, o_f, o_g = 0, Q_LORA, Q_LORA + KV_LORA, Q_LORA + KV_LORA + 2 * HEAD_PAD, \
        Q_LORA + KV_LORA + 2 * HEAD_PAD + F_WIDTH

    ang_t = invf_ref[...] * pos_ref[0].astype(F32)
    cos_t = jnp.cos(ang_t)
    sin_t = jnp.sin(ang_t)

    ql = jnp.dot(ub, wcat_ref[:, o_q:o_kv], preferred_element_type=F32)
    kvl = jnp.dot(ub, wcat_ref[:, o_kv:o_kr], preferred_element_type=F32)
    kr = jnp.dot(ub, wcat_ref[:, o_kr:o_f], preferred_element_type=F32)
    fin = jnp.dot(ub, wcat_ref[:, o_f:o_g], preferred_element_type=F32).astype(BF16)
    zg = jnp.dot(ub, wcat_ref[:, o_g:], preferred_element_type=F32) + bg_ref[...]

    qn = _rms(ql, gq_ref[...]).astype(BF16)
    kvn = _rms(kvl, gkv_ref[...]).astype(BF16)
    zz = jnp.dot(fin, bd_ref[...], preferred_element_type=F32)
    qab_t = lax.dot_general(wqt_ref[...], qn, nt_dims, preferred_element_type=F32)
    kn = jnp.dot(kvn, wk_ref[...], preferred_element_type=F32)
    vt = lax.dot_general(wvt_ref[...], kvn, nt_dims, preferred_element_type=F32) + vones_ref[...]

    gate_ref[0] = jax.nn.sigmoid(zg).astype(BF16)
    zr_ref[0] = zz[:, :F_WIDTH].astype(BF16)
    zi_ref[0] = zz[:, F_WIDTH:].astype(BF16)
    cq = cos_t * scale
    sq = sin_t * scale
    for h in range(N_HEADS):
        a = qab_t[h * HEAD_PAD:(h + 1) * HEAD_PAD]
        b = qab_t[hp + h * HEAD_PAD:hp + (h + 1) * HEAD_PAD]
        qt_ref[0, h] = (a * cq + b * sq).astype(BF16)
    krr = kr[:, :HEAD_PAD] * cos_t.T + kr[:, HEAD_PAD:] * sin_t.T
    for h in range(N_HEADS):
        k_ref[0, h] = (kn[:, h * HEAD_PAD:(h + 1) * HEAD_PAD] + krr).astype(BF16)
        vt_ref[0, h, 0] = vt[h * V_ROWS:(h + 1) * V_ROWS].astype(BF16)


def _inproj(x, pos_row, sh1, sc1, gmix, wcat, gq, gkv, wqt, wk, wvt, vones, bd, bg, invf, tm):
    bsz, s, d = x.shape
    hp = N_HEADS * HEAD_PAD
    nt = s // tm
    const = lambda shape: pl.BlockSpec(shape, lambda b, i: (0,) * len(shape))
    tok = lambda w: pl.BlockSpec((1, tm, w), lambda b, i: (b, i, 0))
    per_b = pl.BlockSpec((1, 1, d), lambda b, i: (b, 0, 0))
    scale = float((NOPE + ROPE) ** -0.5 * np.log2(np.e))
    return pl.pallas_call(
        functools.partial(_inproj_kernel, scale=scale),
        grid=(bsz, nt),
        in_specs=[tok(d), pl.BlockSpec((1, 1, tm), lambda b, i: (b, 0, i)), per_b, per_b,
                  const((1, d)), const(wcat.shape), const((1, Q_LORA)), const((1, KV_LORA)),
                  const(wqt.shape), const(wk.shape), const(wvt.shape), const(vones.shape),
                  const(bd.shape), const((1, bg.shape[1])), const((HEAD_PAD, 1))],
        out_specs=[pl.BlockSpec((1, N_HEADS, HEAD_PAD, tm), lambda b, i: (b, 0, 0, i)),
                   pl.BlockSpec((1, N_HEADS, tm, HEAD_PAD), lambda b, i: (b, 0, i, 0)),
                   pl.BlockSpec((1, N_HEADS, 1, V_ROWS, tm), lambda b, i: (b, 0, i, 0, 0)),
                   tok(F_WIDTH), tok(F_WIDTH), tok(2 * d)],
        out_shape=[jax.ShapeDtypeStruct((bsz, N_HEADS, HEAD_PAD, s), BF16),
                   jax.ShapeDtypeStruct((bsz, N_HEADS, s, HEAD_PAD), BF16),
                   jax.ShapeDtypeStruct((bsz, N_HEADS, nt, V_ROWS, tm), BF16),
                   jax.ShapeDtypeStruct((bsz, s, F_WIDTH), BF16),
                   jax.ShapeDtypeStruct((bsz, s, F_WIDTH), BF16),
                   jax.ShapeDtypeStruct((bsz, s, 2 * d), BF16)],
        compiler_params=_cparams(("parallel", "parallel")),
        name="inproj",
    )(x, pos_row, sh1, sc1, gmix, wcat, gq, gkv, wqt, wk, wvt, vones, bd, bg, invf)


def _attn_kernel(qt_ref, k_ref, vt_ref, o_ref, *scratch, nk):
    tq = qt_ref.shape[3]
    tk = vt_ref.shape[4]
    qts = [qt_ref[0, hh] for hh in range(2)]
    nbuf = ATTN_DEPTH + 1
    p_refs, s_refs = scratch[:2 * nbuf], scratch[2 * nbuf:]

    def scores(c, hh):
        s_ref = s_refs[hh * 2 + c % 2]
        for i in range(tk // QK_ROWS):
            off = c * tk + i * QK_ROWS
            s_ref[i * QK_ROWS:(i + 1) * QK_ROWS, :] = jnp.dot(
                k_ref[0, hh, off:off + QK_ROWS, :], qts[hh], preferred_element_type=F32)
        return s_ref

    def probs(s_ref, m, p_ref):
        m_new = jnp.maximum(m, jnp.max(s_ref[...], axis=0, keepdims=True))
        p_ref[...] = jnp.exp2(s_ref[...] - m_new).astype(BF16)
        return m_new, jnp.exp2(m - m_new)

    def accumulate(c, hh, acc, alpha, p_ref):
        return alpha * acc + jnp.dot(vt_ref[0, hh, c], p_ref[...], preferred_element_type=F32)

    hs = range(2)
    buf = lambda c, hh: p_refs[hh * nbuf + c % nbuf]
    m = [jnp.full((1, tq), -jnp.inf, F32) for _ in hs]
    alpha = [[None] * nk for _ in hs]
    accs = [jnp.zeros((V_ROWS, tq), F32) for _ in hs]
    for c in range(nk + ATTN_DEPTH):
        if c < nk:
            s = [scores(c, hh) for hh in hs]
        if c >= ATTN_DEPTH:
            for hh in hs:
                accs[hh] = accumulate(c - ATTN_DEPTH, hh, accs[hh], alpha[hh][c - ATTN_DEPTH],
                                      buf(c - ATTN_DEPTH, hh))
        if c < nk:
            for hh in hs:
                m[hh], alpha[hh][c] = probs(s[hh], m[hh], buf(c, hh))
    out_t = jnp.concatenate([a[:V_DIM] / a[V_DIM:V_DIM + 1] for a in accs], axis=0)
    o_ref[0] = out_t.T.astype(BF16)


def _attention(qt, k, vt, tq):
    bsz, nh, s, _ = k.shape
    nk, tk = vt.shape[2], vt.shape[4]
    return pl.pallas_call(
        functools.partial(_attn_kernel, nk=nk),
        grid=(bsz, nh // 2, s // tq),
        in_specs=[pl.BlockSpec((1, 2, HEAD_PAD, tq), lambda b, j, i: (b, j, 0, i)),
                  pl.BlockSpec((1, 2, s, HEAD_PAD), lambda b, j, i: (b, j, 0, 0)),
                  pl.BlockSpec((1, 2, nk, V_ROWS, tk), lambda b, j, i: (b, j, 0, 0, 0))],
        out_specs=pl.BlockSpec((1, tq, HEAD_PAD), lambda b, j, i: (b, i, j)),
        out_shape=jax.ShapeDtypeStruct((bsz, s, nh * V_DIM), BF16),
        scratch_shapes=[pltpu.VMEM((tk, tq), BF16)] * (2 * (ATTN_DEPTH + 1))
        + [pltpu.VMEM((tk, tq), F32)] * 4,
        compiler_params=_cparams(("parallel", "parallel", "arbitrary")),
        name="attention",
    )(qt, k, vt)


def _fft_a_kernel(zr_ref, zi_ref, m1_ref, br_ref, bi_ref):
    n1 = zr_ref.shape[1]
    ts2 = m1_ref.shape[0]
    for j in range(ts2):
        sl = slice(j * F_WIDTH, (j + 1) * F_WIDTH)
        xx = jnp.concatenate([zr_ref[0, :, sl], zi_ref[0, :, sl]], axis=0)
        y = jnp.dot(m1_ref[j], xx, preferred_element_type=F32)
        br_ref[0, :, sl] = y[:n1].astype(BF16)
        bi_ref[0, :, sl] = y[n1:].astype(BF16)


def _fft_a(zr, zi, m1, ts2):
    bsz, s, _ = zr.shape
    n1 = s // FFT_N2
    zr2 = zr.reshape(bsz, n1, FFT_N2 * F_WIDTH)
    zi2 = zi.reshape(bsz, n1, FFT_N2 * F_WIDTH)
    blk = pl.BlockSpec((1, n1, ts2 * F_WIDTH), lambda b, j: (b, 0, j))
    return pl.pallas_call(
        _fft_a_kernel,
        grid=(bsz, FFT_N2 // ts2),
        in_specs=[blk, blk, pl.BlockSpec((ts2, 2 * n1, 2 * n1), lambda b, j: (j, 0, 0))],
        out_specs=[blk, blk],
        out_shape=[jax.ShapeDtypeStruct(zr2.shape, BF16)] * 2,
        compiler_params=_cparams(("parallel", "parallel")),
        name="fft_a",
    )(zr2, zi2, m1)


def _fft_b_kernel(br_ref, bi_ref, wb_ref, o_ref):
    tk1 = br_ref.shape[1]
    for j in range(tk1):
        xx = jnp.concatenate([br_ref[0, j], bi_ref[0, j]], axis=0)
        y = jnp.dot(wb_ref[...], xx, preferred_element_type=F32)
        o_ref[0, :, j * F_WIDTH:(j + 1) * F_WIDTH] = y.astype(BF16)


def _fft_b(br, bi, wb, tk1):
    bsz, n1, _ = br.shape
    br4 = br.reshape(bsz, n1, FFT_N2, F_WIDTH)
    bi4 = bi.reshape(bsz, n1, FFT_N2, F_WIDTH)
    blk = pl.BlockSpec((1, tk1, FFT_N2, F_WIDTH), lambda b, j: (b, j, 0, 0))
    out = pl.pallas_call(
        _fft_b_kernel,
        grid=(bsz, n1 // tk1),
        in_specs=[blk, blk, pl.BlockSpec((FFT_N2, 2 * FFT_N2), lambda b, j: (0, 0))],
        out_specs=pl.BlockSpec((1, FFT_N2, tk1 * F_WIDTH), lambda b, j: (b, 0, j)),
        out_shape=jax.ShapeDtypeStruct((bsz, FFT_N2, n1 * F_WIDTH), BF16),
        compiler_params=_cparams(("parallel", "parallel")),
        name="fft_b",
    )(br4, bi4, wb)
    return out.reshape(bsz, n1 * FFT_N2, F_WIDTH)


def _dft_constants(s):
    n1, n2 = s // FFT_N2, FFT_N2
    m = np.arange(F_GDIM)
    th = 2.0 * np.pi * np.outer(m, m) / F_GDIM
    eye = np.eye(F_GROUPS)
    bd = np.concatenate([np.kron(eye, np.cos(th)), -np.kron(eye, np.sin(th))], axis=1) / np.sqrt(F_GDIM)
    k1 = np.arange(n1)[None, :, None]
    s1 = np.arange(n1)[None, None, :]
    s2 = np.arange(n2)[:, None, None]
    ph = 2.0 * np.pi * ((k1 * (n2 * s1 + s2)) % s) / s
    cr, ci = np.cos(ph) / np.sqrt(n1), -np.sin(ph) / np.sqrt(n1)
    m1 = np.concatenate([np.concatenate([cr, -ci], axis=2), np.concatenate([ci, cr], axis=2)], axis=1)
    k2 = np.arange(n2)
    tb = 2.0 * np.pi * np.outer(k2, k2) / n2
    wb = np.concatenate([np.cos(tb), np.sin(tb)], axis=1) / np.sqrt(n2)
    return (jnp.asarray(bd, BF16), jnp.asarray(m1, BF16), jnp.asarray(wb, BF16))


def _route(parts, bias):
    logit = parts[:, :LANES] + parts[:, LANES:] + bias
    lane = lax.broadcasted_iota(jnp.int32, logit.shape, 1)
    neg = -jnp.inf
    big = jnp.int32(LANES)

    def softmax_masked(mask):
        z = jnp.where(mask, logit, neg)
        e = jnp.exp(z - jnp.max(z, axis=-1, keepdims=True))
        return e / jnp.sum(e, axis=-1, keepdims=True)

    def top1(p, mask):
        pm = jnp.where(mask, p, -1.0)
        best = jnp.max(pm, axis=-1, keepdims=True)
        idx = jnp.min(jnp.where(pm == best, lane, big), axis=-1, keepdims=True)
        return best, idx

    gmask = (lane >= N_EXPERTS) & (lane < N_EXPERTS + N_GROUPS)
    g_prob = softmax_masked(gmask)
    g_p, g_lane = top1(g_prob, gmask)
    e_lo = (g_lane - N_EXPERTS) * E_PER_GROUP
    emask = (lane >= e_lo) & (lane < e_lo + E_PER_GROUP)
    e_prob = softmax_masked(emask)
    p1, i1 = top1(e_prob, emask)
    p2, i2 = top1(e_prob, emask & (lane != i1))
    den = p1 + p2
    w1 = g_p * (p1 / den)
    w2 = g_p * (p2 / den)
    route = jnp.where(lane == 0, i1.astype(F32), jnp.where(lane == 1, i2.astype(F32),
                      jnp.where(lane == 2, w1, jnp.where(lane == 3, w2, 0.0))))
    cnt = jnp.sum(((lane == i1) | (lane == i2)).astype(F32), axis=0, keepdims=True)
    return route, cnt


def _merge_kernel(at_ref, fo_ref, gate_ref, x_ref, wba_ref, wbf_ref, wout_ref, gt1_ref, sh2_ref,
                  sc2_ref, gffn_ref, wr_ref, br_ref, h1_ref, v2_ref, route_ref, cnt_ref):
    d = x_ref.shape[-1]
    tm = x_ref.shape[1]
    blocks = [slice(i * MERGE_ROWS, (i + 1) * MERGE_ROWS) for i in range(tm // MERGE_ROWS)]
    a = [jnp.dot(at_ref[0, r, :], wba_ref[...], preferred_element_type=F32) for r in blocks]
    fo = [jnp.dot(fo_ref[0, r, :], wbf_ref[...], preferred_element_type=F32) for r in blocks]
    y = []
    for r, ai, fi in zip(blocks, a, fo):
        merged = gate_ref[0, r, :d].astype(F32) * ai + gate_ref[0, r, d:].astype(F32) * fi
        y.append(jnp.dot(merged.astype(BF16), wout_ref[...], preferred_element_type=F32))
    parts = []
    for r, yi in zip(blocks, y):
        h1 = x_ref[0, r, :] + gt1_ref[0] * yi
        h1_ref[0, r, :] = h1
        v2 = _rms(h1, gffn_ref[...]) * (1.0 + sc2_ref[0]) + sh2_ref[0]
        v_hi = v2.astype(BF16)
        v2_ref[0, r, :] = v_hi
        v_lo = (v2 - v_hi.astype(F32)).astype(BF16)
        parts.append(jnp.dot(jnp.concatenate([v_hi, v_lo], axis=1), wr_ref[...],
                             preferred_element_type=F32))
    cnt = None
    for r, pi in zip(blocks, parts):
        route, c = _route(pi, br_ref[...])
        route_ref[0, r, :] = route
        cnt = c if cnt is None else cnt + c
    cnt_ref[0, 0] = cnt


def _merge(attn, four, gates, x, wba, wbf, wout, gt1, sh2, sc2, gffn, wr, br, tm):
    bsz, s, d = x.shape
    const = lambda shape: pl.BlockSpec(shape, lambda b, i: (0,) * len(shape))
    tok = lambda w: pl.BlockSpec((1, tm, w), lambda b, i: (b, i, 0))
    per_b = pl.BlockSpec((1, 1, d), lambda b, i: (b, 0, 0))
    return pl.pallas_call(
        _merge_kernel,
        grid=(bsz, s // tm),
        in_specs=[tok(attn.shape[-1]), tok(F_WIDTH), tok(2 * d), tok(d), const(wba.shape),
                  const(wbf.shape), const(wout.shape), per_b, per_b, per_b, const((1, d)),
                  const(wr.shape), const((1, LANES))],
        out_specs=[tok(d), tok(d), tok(LANES),
                   pl.BlockSpec((1, 1, 1, LANES), lambda b, i: (b, i, 0, 0))],
        out_shape=[jax.ShapeDtypeStruct((bsz, s, d), F32), jax.ShapeDtypeStruct((bsz, s, d), BF16),
                   jax.ShapeDtypeStruct((bsz, s, LANES), F32),
                   jax.ShapeDtypeStruct((bsz, s // tm, 1, LANES), F32)],
        compiler_params=_cparams(("parallel", "parallel")),
        name="merge",
    )(attn, four, gates, x, wba, wbf, wout, gt1, sh2, sc2, gffn, wr, br)


GRAN = 16
MOE_ROW_TILE = 512


def _moe_metadata(cnt, tm):
    ntile = cnt.shape[0]

    def cumsum(a, axis):
        n = a.shape[axis]
        tri = jnp.asarray(np.triu(np.ones((n, n), np.float32)))
        af = a.astype(F32)
        out = (jnp.dot(af, tri, precision=lax.Precision.HIGHEST) if axis == a.ndim - 1
               else jnp.dot(tri.T, af, precision=lax.Precision.HIGHEST))
        return out.astype(jnp.int32)

    c = cnt[:, :N_EXPERTS].astype(jnp.int32)
    pc = (c + GRAN - 1) // GRAN * GRAN
    loff = cumsum(pc, 1) - pc
    tot = jnp.sum(pc, axis=0)
    tot_r = (tot + MOE_ROW_TILE - 1) // MOE_ROW_TILE * MOE_ROW_TILE
    ends = cumsum(tot_r[None, :], 1)[0]
    base = ends - tot_r
    goff = base[None, :] + cumsum(pc, 0) - pc
    cap_tiles = (2 * ntile * tm + ntile * N_EXPERTS * GRAN) // MOE_ROW_TILE + N_EXPERTS
    n_valid = (ends[-1] // MOE_ROW_TILE).astype(jnp.int32)
    tile_start = jnp.arange(cap_tiles, dtype=jnp.int32) * MOE_ROW_TILE
    tile_expert = jnp.minimum(jnp.sum((ends[None, :] <= tile_start[:, None]).astype(jnp.int32), axis=1),
                              N_EXPERTS - 1)
    flat = lambda a: a.reshape(-1).astype(jnp.int32)
    tail = tot_r - tot + jnp.where(jnp.arange(N_EXPERTS) == N_EXPERTS - 1,
                                   cap_tiles * MOE_ROW_TILE - ends[-1], 0)
    meta = dict(loff=flat(loff), goff=flat(goff), ngran=flat(pc // GRAN),
                ntot=flat(jnp.sum(pc, axis=1) // GRAN),
                tail_start=flat(base + tot), tail_n=flat(tail // GRAN),
                tile_expert=tile_expert, n_valid=n_valid.reshape(1))
    return meta, loff.astype(F32), cap_tiles


def _run_copies(i, loff_s, goff_s, ngran_s, make_copy):
    def per_expert(e, n_started):
        n = ngran_s[i * N_EXPERTS + e]
        lo = loff_s[i * N_EXPERTS + e]
        go = goff_s[i * N_EXPERTS + e]

        def per_granule(g, carry):
            make_copy(pl.multiple_of(lo + g * GRAN, GRAN), pl.multiple_of(go + g * GRAN, GRAN)).start()
            return carry

        lax.fori_loop(0, n, per_granule, 0)
        return n_started + n

    return lax.fori_loop(0, N_EXPERTS, per_expert, 0)


def _wait_copies(n, make_copy):
    def one(_, carry):
        make_copy(0, 0).wait()
        return carry
    lax.fori_loop(0, n, one, 0)


def _local_positions(route, loffv, ltri):
    lane = lax.broadcasted_iota(jnp.int32, route.shape, 1).astype(F32)
    o1 = lane == route[:, 0:1]
    o2 = lane == route[:, 1:2]
    before = jnp.dot(ltri, (o1 | o2).astype(BF16), preferred_element_type=F32)
    start = before + loffv
    pos1 = jnp.sum(jnp.where(o1, start, 0.0), axis=-1, keepdims=True)
    pos2 = jnp.sum(jnp.where(o2, start, 0.0), axis=-1, keepdims=True)
    return pos1, pos2


def _dispatch_kernel(loff_s, goff_s, ngran_s, ntot_s, tstart_s, tn_s, v2_ref, route_ref, loffv_ref,
                     ltri_ref, xs_ref, posw_ref, loc_ref, zero_ref, sem):
    i = pl.program_id(0)
    last = pl.num_programs(0) - 1
    slot = i % 2
    rows = loc_ref.shape[1]
    route = route_ref[...]
    pos1, pos2 = _local_positions(route, loffv_ref[0], ltri_ref[...])
    lane = lax.broadcasted_iota(jnp.int32, route.shape, 1)
    posw = jnp.where(lane == 0, pos1, jnp.where(lane == 1, pos2, route))
    posw_ref[...] = posw
    pos_t = posw.T
    riota = lax.broadcasted_iota(jnp.int32, (rows, 1), 0).astype(F32)
    onehot = ((riota == pos_t[0:1]) | (riota == pos_t[1:2])).astype(BF16)
    loc_ref[slot] = jnp.dot(onehot, v2_ref[...], preferred_element_type=F32).astype(BF16)

    def out_copy(s):
        return lambda lo, go: pltpu.make_async_copy(
            loc_ref.at[s, pl.ds(lo, GRAN)], xs_ref.at[pl.ds(go, GRAN)], sem.at[s])

    _run_copies(i, loff_s, goff_s, ngran_s, out_copy(slot))

    @pl.when(i > 0)
    def _():
        _wait_copies(ntot_s[i - 1], out_copy(1 - slot))

    @pl.when(i == last)
    def _():
        _wait_copies(ntot_s[i], out_copy(slot))
        zero_ref[...] = jnp.zeros(zero_ref.shape, BF16)

        def zero_copy(go):
            return pltpu.make_async_copy(zero_ref, xs_ref.at[pl.ds(go, GRAN)], sem.at[0])

        def per_expert(e, n_started):
            def per_granule(g, carry):
                zero_copy(pl.multiple_of(tstart_s[e] + g * GRAN, GRAN)).start()
                return carry
            lax.fori_loop(0, tn_s[e], per_granule, 0)
            return n_started + tn_s[e]

        nz = lax.fori_loop(0, N_EXPERTS, per_expert, 0)
        _wait_copies(nz, lambda lo, go: zero_copy(go))


def _dispatch(v2, route, meta, loffv, ltri, cap_tiles, tm, rows):
    t, d = v2.shape
    ntile = t // tm
    grid_spec = pltpu.PrefetchScalarGridSpec(
        num_scalar_prefetch=6,
        grid=(ntile,),
        in_specs=[pl.BlockSpec((tm, d), lambda i, *_: (i, 0)),
                  pl.BlockSpec((tm, LANES), lambda i, *_: (i, 0)),
                  pl.BlockSpec((1, 1, LANES), lambda i, *_: (i, 0, 0)),
                  pl.BlockSpec((tm, tm), lambda i, *_: (0, 0))],
        out_specs=[pl.BlockSpec(memory_space=pl.ANY),
                   pl.BlockSpec((tm, LANES), lambda i, *_: (i, 0))],
        scratch_shapes=[pltpu.VMEM((2, rows, d), BF16), pltpu.VMEM((GRAN, d), BF16),
                        pltpu.SemaphoreType.DMA((2,))],
    )
    return pl.pallas_call(
        _dispatch_kernel,
        grid_spec=grid_spec,
        out_shape=[jax.ShapeDtypeStruct((cap_tiles * MOE_ROW_TILE, d), BF16),
                   jax.ShapeDtypeStruct((t, LANES), F32)],
        compiler_params=_cparams(("arbitrary",)),
        name="dispatch",
    )(meta["loff"], meta["goff"], meta["ngran"], meta["ntot"], meta["tail_start"], meta["tail_n"],
      v2, route, loffv, ltri)


def _experts_kernel(texp_s, nvalid_s, x_ref, wgu_ref, wd_ref, y_ref):
    del texp_s

    @pl.when(pl.program_id(0) < nvalid_s[0])
    def _():
        hc = jnp.dot(x_ref[...], wgu_ref[0], preferred_element_type=F32)
        g = hc[:, :D_EXPERT]
        hid = (g * jax.nn.sigmoid(g) * hc[:, D_EXPERT:]).astype(BF16)
        y_ref[...] = jnp.dot(hid, wd_ref[0], preferred_element_type=F32).astype(BF16)

    @pl.when(pl.program_id(0) >= nvalid_s[0])
    def _():
        y_ref[...] = jnp.zeros(y_ref.shape, BF16)


def _experts(xs, wgu, wd, meta, cap_tiles):
    d = xs.shape[1]
    row = lambda j, texp, nv: (jnp.minimum(j, nv[0] - 1), 0)
    wsel = lambda j, texp, nv: (texp[jnp.minimum(j, nv[0] - 1)], 0, 0)
    grid_spec = pltpu.PrefetchScalarGridSpec(
        num_scalar_prefetch=2,
        grid=(cap_tiles,),
        in_specs=[pl.BlockSpec((MOE_ROW_TILE, d), row),
                  pl.BlockSpec((1, d, 2 * D_EXPERT), wsel),
                  pl.BlockSpec((1, D_EXPERT, d), wsel)],
        out_specs=pl.BlockSpec((MOE_ROW_TILE, d), lambda j, texp, nv: (j, 0)),
    )
    return pl.pallas_call(
        _experts_kernel,
        grid_spec=grid_spec,
        out_shape=jax.ShapeDtypeStruct(xs.shape, BF16),
        compiler_params=_cparams(("arbitrary",)),
        name="experts",
    )(meta["tile_expert"], meta["n_valid"], xs, wgu, wd)


def _combine_kernel(loff_s, goff_s, ngran_s, ntot_s, posw_ref, ys_ref, h1_ref, gt2_ref, shf_ref,
                    scf_ref, gfin_ref, o_ref, loc_ref, sem):
    i = pl.program_id(0)
    slot = i % 2
    rows = loc_ref.shape[1]

    def in_copy(s):
        return lambda lo, go: pltpu.make_async_copy(
            ys_ref.at[pl.ds(go, GRAN)], loc_ref.at[s, pl.ds(lo, GRAN)], sem.at[s])

    def fetch(tile, s):
        loc_ref[s] = jnp.zeros(loc_ref.shape[1:], BF16)
        _run_copies(tile, loff_s, goff_s, ngran_s, in_copy(s))

    @pl.when(i == 0)
    def _():
        fetch(0, 0)

    @pl.when(i + 1 < pl.num_programs(0))
    def _():
        fetch(i + 1, 1 - slot)

    _wait_copies(ntot_s[i], in_copy(slot))

    posw = posw_ref[...]
    riota = lax.broadcasted_iota(jnp.int32, (1, rows), 1).astype(F32)
    qw = (jnp.where(riota == posw[:, 0:1], posw[:, 2:3], 0.0)
          + jnp.where(riota == posw[:, 1:2], posw[:, 3:4], 0.0)).astype(BF16)
    moe = jnp.dot(qw, loc_ref[slot], preferred_element_type=F32)
    h2 = h1_ref[...] + gt2_ref[0] * moe
    o_ref[...] = _rms(h2, gfin_ref[...]) * (1.0 + scf_ref[0]) + shf_ref[0]


def _combine(posw, ys, h1, gt2, shf, scf, gfin, meta, tm, rows, tiles_per_batch):
    t, d = h1.shape
    per_b = pl.BlockSpec((1, 1, d), lambda i, *_: (i // tiles_per_batch, 0, 0))
    grid_spec = pltpu.PrefetchScalarGridSpec(
        num_scalar_prefetch=4,
        grid=(t // tm,),
        in_specs=[pl.BlockSpec((tm, LANES), lambda i, *_: (i, 0)),
                  pl.BlockSpec(memory_space=pl.ANY),
                  pl.BlockSpec((tm, d), lambda i, *_: (i, 0)),
                  per_b, per_b, per_b, pl.BlockSpec((1, d), lambda i, *_: (0, 0))],
        out_specs=pl.BlockSpec((tm, d), lambda i, *_: (i, 0)),
        scratch_shapes=[pltpu.VMEM((2, rows, d), BF16), pltpu.SemaphoreType.DMA((2,))],
    )
    return pl.pallas_call(
        _combine_kernel,
        grid_spec=grid_spec,
        out_shape=jax.ShapeDtypeStruct((t, d), F32),
        compiler_params=_cparams(("arbitrary",)),
        name="combine",
    )(meta["loff"], meta["goff"], meta["ngran"], meta["ntot"], posw, ys, h1, gt2, shf, scf, gfin)


def _rot_cols(w):
    half = ROPE // 2
    return jnp.concatenate([-w[..., half:], w[..., :half]], axis=-1)


def _pad_cols(w, lo, width):
    return jnp.pad(w, [(0, 0)] * (w.ndim - 1) + [(lo, width - lo - w.shape[-1])])


def _prep_weights(w_in, w_uq, w_uk, w_uv):
    o2, o3 = Q_LORA + KV_LORA, Q_LORA + KV_LORA + ROPE
    w_kr = w_in[:, o2:o3]
    wcat = jnp.concatenate([
        w_in[:, :o2],
        _pad_cols(w_kr, NOPE, HEAD_PAD), _pad_cols(_rot_cols(w_kr), NOPE, HEAD_PAD),
        w_in[:, o3:]], axis=1).astype(BF16)
    wq_a = _pad_cols(w_uq, 0, HEAD_PAD)
    wq_b = _pad_cols(_rot_cols(w_uq[..., NOPE:]), NOPE, HEAD_PAD)
    r = w_uq.shape[0]
    wqt = jnp.concatenate([wq_a.reshape(r, -1), wq_b.reshape(r, -1)], axis=1).T.astype(BF16)
    wk = _pad_cols(w_uk, 0, HEAD_PAD).reshape(r, -1).astype(BF16)
    wvt = _pad_cols(w_uv, 0, V_ROWS).reshape(r, -1).T.astype(BF16)
    ones = np.zeros((N_HEADS, V_ROWS), np.float32)
    ones[:, V_DIM] = 1.0
    return wcat, wqt, wk, wvt, jnp.asarray(ones.reshape(-1, 1))


def kernel(x, c, positions, w_ada, b_ada, g_norm_mix, w_in, g_q_lat, g_kv_lat, w_uq, w_uk, w_uv, w_branch_attn, w_branch_fourier, b_gates, w_out, g_norm_ffn, w_router_group, b_router_group, w_router_expert, b_router_expert, w_expert_gate, w_expert_up, w_expert_down, w_ada_final, b_ada_final, g_norm_final):
    bsz, s, d = x.shape
    assert w_ada.shape[0] == 1 and s % FFT_N2 == 0
    tm = min(512, s)

    mod = _mods(c, w_ada[0], b_ada[0])
    fmod = _mods(c, w_ada_final, b_ada_final)
    sh1, sc1, gt1, sh2, sc2, gt2 = [m.reshape(bsz, 1, d) for m in jnp.split(mod, 6, axis=-1)]
    shf, scf = [m.reshape(bsz, 1, d) for m in jnp.split(fmod, 2, axis=-1)]

    wcat, wqt, wk, wvt, vones = _prep_weights(w_in[0], w_uq[0], w_uk[0], w_uv[0])
    bd, m1, wb = _dft_constants(s)
    inv_freq = ROPE_THETA ** (-jnp.arange(0, ROPE, 2, dtype=F32) / ROPE)
    invf = _pad_cols(jnp.concatenate([inv_freq, inv_freq])[None, :], NOPE, HEAD_PAD).reshape(HEAD_PAD, 1)

    qt, k, vt, zr, zi, gates = _inproj(
        x, positions.reshape(bsz, 1, s), sh1, sc1, g_norm_mix[0][None, :], wcat,
        g_q_lat[0][None, :], g_kv_lat[0][None, :], wqt, wk, wvt, vones, bd, b_gates[0][None, :],
        invf, tm)

    attn = _attention(qt, k, vt, tq=min(256, s))
    br, bi = _fft_a(zr, zi, m1, ts2=8)
    four = _fft_b(br, bi, wb, tk1=16)

    wr = jnp.concatenate([w_router_expert[0].reshape(d, N_EXPERTS), w_router_group[0]], axis=1)
    wr = _pad_cols(wr, 0, LANES)
    wr_top = lax.bitcast_convert_type(
        lax.bitcast_convert_type(wr, jnp.uint32) & jnp.uint32(0xFFFF0000), F32)
    wr_hi = wr_top.astype(BF16)
    wr_lo = (wr - wr_top).astype(BF16)
    wr = jnp.concatenate([jnp.concatenate([wr_hi, wr_lo], axis=1),
                          jnp.concatenate([wr_hi, jnp.zeros_like(wr_lo)], axis=1)], axis=0)
    brt = _pad_cols(jnp.concatenate([b_router_expert[0].reshape(-1), b_router_group[0]])[None, :], 0, LANES)
    h1, v2, route, cnt = _merge(attn, four, gates, x, w_branch_attn[0].astype(BF16),
                                w_branch_fourier[0].astype(BF16), w_out[0].astype(BF16), gt1, sh2,
                                sc2, g_norm_ffn[0][None, :], wr, brt, tm)

    t = bsz * s
    meta, loffv, cap_tiles = _moe_metadata(cnt.reshape(-1, LANES), tm)
    rows = 2 * tm + N_EXPERTS * GRAN
    loffv = _pad_cols(loffv, 0, LANES).reshape(-1, 1, LANES)
    ltri = jnp.asarray(np.tril(np.ones((tm, tm), np.float32), -1), BF16)
    xs, posw = _dispatch(v2.reshape(t, d), route.reshape(t, LANES), meta, loffv, ltri, cap_tiles,
                         tm, rows)
    wgu = jnp.concatenate([w_expert_gate[0], w_expert_up[0]], axis=-1).astype(BF16)
    wd = w_expert_down[0].astype(BF16)
    ys = _experts(xs, wgu, wd, meta, cap_tiles)
    out = _combine(posw, ys, h1.reshape(t, d), gt2, shf, scf, g_norm_final[None, :], meta, tm, rows,
                   s // tm)
    return out.reshape(bsz, s, d)
```

```python
import functools

import numpy as np
import jax
import jax.numpy as jnp
from jax import lax
from jax.experimental import pallas as pl
from jax.experimental.pallas import tpu as pltpu

F32 = jnp.float32
BF16 = jnp.bfloat16

N_HEADS = 8
NOPE = 64
ROPE = 32
V_DIM = 64
HEAD_PAD = 128
V_ROWS = 80
Q_LORA = 256
KV_LORA = 256
ROPE_THETA = 10000.0
F_GROUPS = 8
F_GDIM = 64
F_WIDTH = F_GROUPS * F_GDIM
N_GROUPS = 4
E_PER_GROUP = 8
N_EXPERTS = N_GROUPS * E_PER_GROUP
D_EXPERT = 256
EPS = 1e-6
FFT_N2 = 64
LANES = 128
MERGE_ROWS = 128
ATTN_CENTER_LAG = 2
ATTN_DEPTH = 1
QK_ROWS = 256
VMEM_LIMIT = 56 * 1024 * 1024


def _cparams(sem):
    return pltpu.CompilerParams(dimension_semantics=sem, vmem_limit_bytes=VMEM_LIMIT)


def _mods_kernel(c_ref, w_ref, b_ref, o_ref):
    c = c_ref[...]
    ca = c * jax.nn.sigmoid(c)
    o_ref[...] = jnp.dot(ca, w_ref[...], preferred_element_type=F32,
                         precision=lax.Precision.HIGHEST) + b_ref[...]


def _mods(c, w, b):
    bsz, d = c.shape
    n = w.shape[1]
    tn = 1024
    return pl.pallas_call(
        _mods_kernel,
        grid=(n // tn,),
        in_specs=[pl.BlockSpec((bsz, d), lambda j: (0, 0)),
                  pl.BlockSpec((d, tn), lambda j: (0, j)),
                  pl.BlockSpec((1, tn), lambda j: (0, j))],
        out_specs=pl.BlockSpec((bsz, tn), lambda j: (0, j)),
        out_shape=jax.ShapeDtypeStruct((bsz, n), F32),
        compiler_params=_cparams(("arbitrary",)),
        name="mods",
    )(c, w, b.reshape(1, n))


def _rms(x, g):
    r = lax.rsqrt(jnp.mean(x * x, axis=-1, keepdims=True) + EPS)
    return x * r * g


def _inproj_kernel(x_ref, pos_ref, sh_ref, sc_ref, gmix_ref, wcat_ref, gq_ref, gkv_ref,
                   wqt_ref, wk_ref, wvt_ref, vones_ref, bd_ref, bg_ref, invf_ref,
                   qt_ref, k_ref, vt_ref, zr_ref, zi_ref, gate_ref, *, scale):
    hp = N_HEADS * HEAD_PAD
    nt_dims = (((1,), (1,)), ((), ()))
    u = _rms(x_ref[0], gmix_ref[...]) * (1.0 + sc_ref[0]) + sh_ref[0]
    ub = u.astype(BF16)

    o_q, o_kv, o_kr, o_f, o_g = 0, Q_LORA, Q_LORA + KV_LORA, Q_LORA + KV_LORA + 2 * HEAD_PAD, \
        Q_LORA + KV_LORA + 2 * HEAD_PAD + F_WIDTH

    ang_t = invf_ref[...] * pos_ref[0].astype(F32)
    cos_t = jnp.cos(ang_t)
    sin_t = jnp.sin(ang_t)

    ql = jnp.dot(ub, wcat_ref[:, o_q:o_kv], preferred_element_type=F32)
    kvl = jnp.dot(ub, wcat_ref[:, o_kv:o_kr], preferred_element_type=F32)
    kr = jnp.dot(ub, wcat_ref[:, o_kr:o_f], preferred_element_type=F32)
    fin = jnp.dot(ub, wcat_ref[:, o_f:o_g], preferred_element_type=F32).astype(BF16)
    zg = jnp.dot(ub, wcat_ref[:, o_g:], preferred_element_type=F32) + bg_ref[...]

    qn = _rms(ql, gq_ref[...]).astype(BF16)
    kvn = _rms(kvl, gkv_ref[...]).astype(BF16)
    zz = jnp.dot(fin, bd_ref[...], preferred_element_type=F32)
    qab_t = lax.dot_general(wqt_ref[...], qn, nt_dims, preferred_element_type=F32)
    kn = jnp.dot(kvn, wk_ref[...], preferred_element_type=F32)
    vt = lax.dot_general(wvt_ref[...], kvn, nt_dims, preferred_element_type=F32) + vones_ref[...]

    gate_ref[0] = jax.nn.sigmoid(zg).astype(BF16)
    zr_ref[0] = zz[:, :F_WIDTH].astype(BF16)
    zi_ref[0] = zz[:, F_WIDTH:].astype(BF16)
    cq = cos_t * scale
    sq = sin_t * scale
    for h in range(N_HEADS):
        a = qab_t[h * HEAD_PAD:(h + 1) * HEAD_PAD]
        b = qab_t[hp + h * HEAD_PAD:hp + (h + 1) * HEAD_PAD]
        qt_ref[0, h] = (a * cq + b * sq).astype(BF16)
    krr = kr[:, :HEAD_PAD] * cos_t.T + kr[:, HEAD_PAD:] * sin_t.T
    for h in range(N_HEADS):
        k_ref[0, h] = (kn[:, h * HEAD_PAD:(h + 1) * HEAD_PAD] + krr).astype(BF16)
        vt_ref[0, h, 0] = vt[h * V_ROWS:(h + 1) * V_ROWS].astype(BF16)


def _inproj(x, pos_row, sh1, sc1, gmix, wcat, gq, gkv, wqt, wk, wvt, vones, bd, bg, invf, tm):
    bsz, s, d = x.shape
    hp = N_HEADS * HEAD_PAD
    nt = s // tm
    const = lambda shape: pl.BlockSpec(shape, lambda b, i: (0,) * len(shape))
    tok = lambda w: pl.BlockSpec((1, tm, w), lambda b, i: (b, i, 0))
    per_b = pl.BlockSpec((1, 1, d), lambda b, i: (b, 0, 0))
    scale = float((NOPE + ROPE) ** -0.5 * np.log2(np.e))
    return pl.pallas_call(
        functools.partial(_inproj_kernel, scale=scale),
        grid=(bsz, nt),
        in_specs=[tok(d), pl.BlockSpec((1, 1, tm), lambda b, i: (b, 0, i)), per_b, per_b,
                  const((1, d)), const(wcat.shape), const((1, Q_LORA)), const((1, KV_LORA)),
                  const(wqt.shape), const(wk.shape), const(wvt.shape), const(vones.shape),
                  const(bd.shape), const((1, bg.shape[1])), const((HEAD_PAD, 1))],
        out_specs=[pl.BlockSpec((1, N_HEADS, HEAD_PAD, tm), lambda b, i: (b, 0, 0, i)),
                   pl.BlockSpec((1, N_HEADS, tm, HEAD_PAD), lambda b, i: (b, 0, i, 0)),
                   pl.BlockSpec((1, N_HEADS, 1, V_ROWS, tm), lambda b, i: (b, 0, i, 0, 0)),
                   tok(F_WIDTH), tok(F_WIDTH), tok(2 * d)],
        out_shape=[jax.ShapeDtypeStruct((bsz, N_HEADS, HEAD_PAD, s), BF16),
                   jax.ShapeDtypeStruct((bsz, N_HEADS, s, HEAD_PAD), BF16),
                   jax.ShapeDtypeStruct((bsz, N_HEADS, nt, V_ROWS, tm), BF16),
                   jax.ShapeDtypeStruct((bsz, s, F_WIDTH), BF16),
                   jax.ShapeDtypeStruct((bsz, s, F_WIDTH), BF16),
                   jax.ShapeDtypeStruct((bsz, s, 2 * d), BF16)],
        compiler_params=_cparams(("parallel", "parallel")),
        name="inproj",
    )(x, pos_row, sh1, sc1, gmix, wcat, gq, gkv, wqt, wk, wvt, vones, bd, bg, invf)


def _attn_kernel(qt_ref, k_ref, vt_ref, o_ref, *scratch, nk):
    tq = qt_ref.shape[3]
    tk = vt_ref.shape[4]
    qts = [qt_ref[0, hh] for hh in range(2)]
    nbuf = ATTN_DEPTH + 1
    p_refs, d_refs, s0_refs = scratch[:2 * nbuf], scratch[2 * nbuf:2 * nbuf + 4], scratch[2 * nbuf + 4:]
    blocks = [(i * QK_ROWS, (i + 1) * QK_ROWS) for i in range(tk // QK_ROWS)]

    def sub_scores(c, hh, lo, hi):
        return jnp.dot(k_ref[0, hh, c * tk + lo:c * tk + hi, :], qts[hh], preferred_element_type=F32)

    def first_chunk(hh, p_ref):
        s_ref = s0_refs[hh]
        for lo, hi in blocks:
            s_ref[lo:hi, :] = sub_scores(0, hh, lo, hi)
        m0 = jnp.max(s_ref[...], axis=0, keepdims=True)
        p_ref[...] = jnp.exp2(s_ref[...] - m0).astype(BF16)
        return m0

    def centered_scores(c, hh, center):
        d_ref = d_refs[hh * 2 + c % 2]
        dmax = None
        for lo, hi in blocks:
            d = sub_scores(c, hh, lo, hi) - center
            cm = jnp.max(d, axis=0, keepdims=True)
            dmax = cm if dmax is None else jnp.maximum(dmax, cm)
            d_ref[lo:hi, :] = d.astype(BF16)
        return d_ref, dmax

    def probs(d_ref, dmax, center, m, p_ref):
        m_new = jnp.maximum(m, center + dmax)
        p_ref[...] = jnp.exp2(d_ref[...] - (m_new - center).astype(BF16))
        return m_new, jnp.exp2(m - m_new)

    def accumulate(c, hh, acc, alpha, p_ref):
        pv = jnp.dot(vt_ref[0, hh, c], p_ref[...], preferred_element_type=F32)
        return pv if acc is None else alpha * acc + pv

    hs = range(2)
    buf = lambda c, hh: p_refs[hh * nbuf + c % nbuf]
    m_hist = [[None] * nk for _ in hs]
    alpha = [[None] * nk for _ in hs]
    accs = [None for _ in hs]
    for c in range(nk + ATTN_DEPTH):
        if 0 < c < nk:
            centers = [m_hist[hh][max(c - ATTN_CENTER_LAG, 0)] for hh in hs]
            ds = [centered_scores(c, hh, centers[hh]) for hh in hs]
        if c == 0:
            for hh in hs:
                m_hist[hh][0] = first_chunk(hh, buf(0, hh))
        if c >= ATTN_DEPTH:
            for hh in hs:
                accs[hh] = accumulate(c - ATTN_DEPTH, hh, accs[hh], alpha[hh][c - ATTN_DEPTH],
                                      buf(c - ATTN_DEPTH, hh))
        if 0 < c < nk:
            for hh in hs:
                m_hist[hh][c], alpha[hh][c] = probs(ds[hh][0], ds[hh][1], centers[hh],
                                                    m_hist[hh][c - 1], buf(c, hh))
    out_t = jnp.concatenate([a[:V_DIM] / a[V_DIM:V_DIM + 1] for a in accs], axis=0)
    o_ref[0] = out_t.T.astype(BF16)


def _attention(qt, k, vt, tq):
    bsz, nh, s, _ = k.shape
    nk, tk = vt.shape[2], vt.shape[4]
    return pl.pallas_call(
        functools.partial(_attn_kernel, nk=nk),
        grid=(bsz, nh // 2, s // tq),
        in_specs=[pl.BlockSpec((1, 2, HEAD_PAD, tq), lambda b, j, i: (b, j, 0, i)),
                  pl.BlockSpec((1, 2, s, HEAD_PAD), lambda b, j, i: (b, j, 0, 0)),
                  pl.BlockSpec((1, 2, nk, V_ROWS, tk), lambda b, j, i: (b, j, 0, 0, 0))],
        out_specs=pl.BlockSpec((1, tq, HEAD_PAD), lambda b, j, i: (b, i, j)),
        out_shape=jax.ShapeDtypeStruct((bsz, s, nh * V_DIM), BF16),
        scratch_shapes=[pltpu.VMEM((tk, tq), BF16)] * (2 * (ATTN_DEPTH + 1) + 4)
        + [pltpu.VMEM((tk, tq), F32)] * 2,
        compiler_params=_cparams(("parallel", "parallel", "arbitrary")),
        name="attention",
    )(qt, k, vt)


def _fft_a_kernel(zr_ref, zi_ref, m1_ref, br_ref, bi_ref):
    n1 = zr_ref.shape[1]
    ts2 = m1_ref.shape[0]
    for j in range(ts2):
        sl = slice(j * F_WIDTH, (j + 1) * F_WIDTH)
        xx = jnp.concatenate([zr_ref[0, :, sl], zi_ref[0, :, sl]], axis=0)
        y = jnp.dot(m1_ref[j], xx, preferred_element_type=F32)
        br_ref[0, :, sl] = y[:n1].astype(BF16)
        bi_ref[0, :, sl] = y[n1:].astype(BF16)


def _fft_a(zr, zi, m1, ts2):
    bsz, s, _ = zr.shape
    n1 = s // FFT_N2
    zr2 = zr.reshape(bsz, n1, FFT_N2 * F_WIDTH)
    zi2 = zi.reshape(bsz, n1, FFT_N2 * F_WIDTH)
    blk = pl.BlockSpec((1, n1, ts2 * F_WIDTH), lambda b, j: (b, 0, j))
    return pl.pallas_call(
        _fft_a_kernel,
        grid=(bsz, FFT_N2 // ts2),
        in_specs=[blk, blk, pl.BlockSpec((ts2, 2 * n1, 2 * n1), lambda b, j: (j, 0, 0))],
        out_specs=[blk, blk],
        out_shape=[jax.ShapeDtypeStruct(zr2.shape, BF16)] * 2,
        compiler_params=_cparams(("parallel", "parallel")),
        name="fft_a",
    )(zr2, zi2, m1)


def _fft_b_kernel(br_ref, bi_ref, wb_ref, o_ref):
    tk1 = br_ref.shape[1]
    for j in range(tk1):
        xx = jnp.concatenate([br_ref[0, j], bi_ref[0, j]], axis=0)
        y = jnp.dot(wb_ref[...], xx, preferred_element_type=F32)
        o_ref[0, :, j * F_WIDTH:(j + 1) * F_WIDTH] = y.astype(BF16)


def _fft_b(br, bi, wb, tk1):
    bsz, n1, _ = br.shape
    br4 = br.reshape(bsz, n1, FFT_N2, F_WIDTH)
    bi4 = bi.reshape(bsz, n1, FFT_N2, F_WIDTH)
    blk = pl.BlockSpec((1, tk1, FFT_N2, F_WIDTH), lambda b, j: (b, j, 0, 0))
    out = pl.pallas_call(
        _fft_b_kernel,
        grid=(bsz, n1 // tk1),
        in_specs=[blk, blk, pl.BlockSpec((FFT_N2, 2 * FFT_N2), lambda b, j: (0, 0))],
        out_specs=pl.BlockSpec((1, FFT_N2, tk1 * F_WIDTH), lambda b, j: (b, 0, j)),
        out_shape=jax.ShapeDtypeStruct((bsz, FFT_N2, n1 * F_WIDTH), BF16),
        compiler_params=_cparams(("parallel", "parallel")),
        name="fft_b",
    )(br4, bi4, wb)
    return out.reshape(bsz, n1 * FFT_N2, F_WIDTH)


def _dft_constants(s):
    n1, n2 = s // FFT_N2, FFT_N2
    m = np.arange(F_GDIM)
    th = 2.0 * np.pi * np.outer(m, m) / F_GDIM
    eye = np.eye(F_GROUPS)
    bd = np.concatenate([np.kron(eye, np.cos(th)), -np.kron(eye, np.sin(th))], axis=1) / np.sqrt(F_GDIM)
    k1 = np.arange(n1)[None, :, None]
    s1 = np.arange(n1)[None, None, :]
    s2 = np.arange(n2)[:, None, None]
    ph = 2.0 * np.pi * ((k1 * (n2 * s1 + s2)) % s) / s
    cr, ci = np.cos(ph) / np.sqrt(n1), -np.sin(ph) / np.sqrt(n1)
    m1 = np.concatenate([np.concatenate([cr, -ci], axis=2), np.concatenate([ci, cr], axis=2)], axis=1)
    k2 = np.arange(n2)
    tb = 2.0 * np.pi * np.outer(k2, k2) / n2
    wb = np.concatenate([np.cos(tb), np.sin(tb)], axis=1) / np.sqrt(n2)
    return (jnp.asarray(bd, BF16), jnp.asarray(m1, BF16), jnp.asarray(wb, BF16))


def _route(parts, bias):
    logit = parts[:, :LANES] + parts[:, LANES:] + bias
    lane = lax.broadcasted_iota(jnp.int32, logit.shape, 1)
    neg = -jnp.inf
    big = jnp.int32(LANES)

    def softmax_masked(mask):
        z = jnp.where(mask, logit, neg)
        e = jnp.exp(z - jnp.max(z, axis=-1, keepdims=True))
        return e / jnp.sum(e, axis=-1, keepdims=True)

    def top1(p, mask):
        pm = jnp.where(mask, p, -1.0)
        best = jnp.max(pm, axis=-1, keepdims=True)
        idx = jnp.min(jnp.where(pm == best, lane, big), axis=-1, keepdims=True)
        return best, idx

    gmask = (lane >= N_EXPERTS) & (lane < N_EXPERTS + N_GROUPS)
    g_prob = softmax_masked(gmask)
    g_p, g_lane = top1(g_prob, gmask)
    e_lo = (g_lane - N_EXPERTS) * E_PER_GROUP
    emask = (lane >= e_lo) & (lane < e_lo + E_PER_GROUP)
    e_prob = softmax_masked(emask)
    p1, i1 = top1(e_prob, emask)
    p2, i2 = top1(e_prob, emask & (lane != i1))
    den = p1 + p2
    w1 = g_p * (p1 / den)
    w2 = g_p * (p2 / den)
    route = jnp.where(lane == 0, i1.astype(F32), jnp.where(lane == 1, i2.astype(F32),
                      jnp.where(lane == 2, w1, jnp.where(lane == 3, w2, 0.0))))
    cnt = jnp.sum(((lane == i1) | (lane == i2)).astype(F32), axis=0, keepdims=True)
    return route, cnt


def _merge_kernel(at_ref, fo_ref, gate_ref, x_ref, wba_ref, wbf_ref, wout_ref, gt1_ref, sh2_ref,
                  sc2_ref, gffn_ref, wr_ref, br_ref, h1_ref, v2_ref, route_ref, cnt_ref):
    d = x_ref.shape[-1]
    tm = x_ref.shape[1]
    blocks = [slice(i * MERGE_ROWS, (i + 1) * MERGE_ROWS) for i in range(tm // MERGE_ROWS)]
    a = [jnp.dot(at_ref[0, r, :], wba_ref[...], preferred_element_type=F32) for r in blocks]
    fo = [jnp.dot(fo_ref[0, r, :], wbf_ref[...], preferred_element_type=F32) for r in blocks]
    y = []
    for r, ai, fi in zip(blocks, a, fo):
        merged = gate_ref[0, r, :d].astype(F32) * ai + gate_ref[0, r, d:].astype(F32) * fi
        y.append(jnp.dot(merged.astype(BF16), wout_ref[...], preferred_element_type=F32))
    parts = []
    for r, yi in zip(blocks, y):
        h1 = x_ref[0, r, :] + gt1_ref[0] * yi
        h1_ref[0, r, :] = h1
        v2 = _rms(h1, gffn_ref[...]) * (1.0 + sc2_ref[0]) + sh2_ref[0]
        v_hi = v2.astype(BF16)
        v2_ref[0, r, :] = v_hi
        v_lo = (v2 - v_hi.astype(F32)).astype(BF16)
        parts.append(jnp.dot(jnp.concatenate([v_hi, v_lo], axis=1), wr_ref[...],
                             preferred_element_type=F32))
    cnt = None
    for r, pi in zip(blocks, parts):
        route, c = _route(pi, br_ref[...])
        route_ref[0, r, :] = route
        cnt = c if cnt is None else cnt + c
    cnt_ref[0, 0] = cnt


def _merge(attn, four, gates, x, wba, wbf, wout, gt1, sh2, sc2, gffn, wr, br, tm):
    bsz, s, d = x.shape
    const = lambda shape: pl.BlockSpec(shape, lambda b, i: (0,) * len(shape))
    tok = lambda w: pl.BlockSpec((1, tm, w), lambda b, i: (b, i, 0))
    per_b = pl.BlockSpec((1, 1, d), lambda b, i: (b, 0, 0))
    return pl.pallas_call(
        _merge_kernel,
        grid=(bsz, s // tm),
        in_specs=[tok(attn.shape[-1]), tok(F_WIDTH), tok(2 * d), tok(d), const(wba.shape),
                  const(wbf.shape), const(wout.shape), per_b, per_b, per_b, const((1, d)),
                  const(wr.shape), const((1, LANES))],
        out_specs=[tok(d), tok(d), tok(LANES),
                   pl.BlockSpec((1, 1, 1, LANES), lambda b, i: (b, i, 0, 0))],
        out_shape=[jax.ShapeDtypeStruct((bsz, s, d), F32), jax.ShapeDtypeStruct((bsz, s, d), BF16),
                   jax.ShapeDtypeStruct((bsz, s, LANES), F32),
                   jax.ShapeDtypeStruct((bsz, s // tm, 1, LANES), F32)],
        compiler_params=_cparams(("parallel", "parallel")),
        name="merge",
    )(attn, four, gates, x, wba, wbf, wout, gt1, sh2, sc2, gffn, wr, br)


GRAN = 16
MOE_ROW_TILE = 512


def _moe_metadata(cnt, tm):
    ntile = cnt.shape[0]

    def cumsum(a, axis):
        n = a.shape[axis]
        tri = jnp.asarray(np.triu(np.ones((n, n), np.float32)))
        af = a.astype(F32)
        out = (jnp.dot(af, tri, precision=lax.Precision.HIGHEST) if axis == a.ndim - 1
               else jnp.dot(tri.T, af, precision=lax.Precision.HIGHEST))
        return out.astype(jnp.int32)

    c = cnt[:, :N_EXPERTS].astype(jnp.int32)
    pc = (c + GRAN - 1) // GRAN * GRAN
    loff = cumsum(pc, 1) - pc
    tot = jnp.sum(pc, axis=0)
    tot_r = (tot + MOE_ROW_TILE - 1) // MOE_ROW_TILE * MOE_ROW_TILE
    ends = cumsum(tot_r[None, :], 1)[0]
    base = ends - tot_r
    goff = base[None, :] + cumsum(pc, 0) - pc
    cap_tiles = (2 * ntile * tm + ntile * N_EXPERTS * GRAN) // MOE_ROW_TILE + N_EXPERTS
    n_valid = (ends[-1] // MOE_ROW_TILE).astype(jnp.int32)
    tile_start = jnp.arange(cap_tiles, dtype=jnp.int32) * MOE_ROW_TILE
    tile_expert = jnp.minimum(jnp.sum((ends[None, :] <= tile_start[:, None]).astype(jnp.int32), axis=1),
                              N_EXPERTS - 1)
    flat = lambda a: a.reshape(-1).astype(jnp.int32)
    tail = tot_r - tot + jnp.where(jnp.arange(N_EXPERTS) == N_EXPERTS - 1,
                                   cap_tiles * MOE_ROW_TILE - ends[-1], 0)
    meta = dict(loff=flat(loff), goff=flat(goff), ngran=flat(pc // GRAN),
                ntot=flat(jnp.sum(pc, axis=1) // GRAN),
                tail_start=flat(base + tot), tail_n=flat(tail // GRAN),
                tile_expert=tile_expert, n_valid=n_valid.reshape(1))
    return meta, loff.astype(F32), cap_tiles


def _run_copies(i, loff_s, goff_s, ngran_s, make_copy):
    def per_expert(e, n_started):
        n = ngran_s[i * N_EXPERTS + e]
        lo = loff_s[i * N_EXPERTS + e]
        go = goff_s[i * N_EXPERTS + e]

        def per_granule(g, carry):
            make_copy(pl.multiple_of(lo + g * GRAN, GRAN), pl.multiple_of(go + g * GRAN, GRAN)).start()
            return carry

        lax.fori_loop(0, n, per_granule, 0)
        return n_started + n

    return lax.fori_loop(0, N_EXPERTS, per_expert, 0)


def _wait_copies(n, make_copy):
    def one(_, carry):
        make_copy(0, 0).wait()
        return carry
    lax.fori_loop(0, n, one, 0)


def _local_positions(route, loffv, ltri):
    lane = lax.broadcasted_iota(jnp.int32, route.shape, 1).astype(F32)
    o1 = lane == route[:, 0:1]
    o2 = lane == route[:, 1:2]
    before = jnp.dot(ltri, (o1 | o2).astype(BF16), preferred_element_type=F32)
    start = before + loffv
    pos1 = jnp.sum(jnp.where(o1, start, 0.0), axis=-1, keepdims=True)
    pos2 = jnp.sum(jnp.where(o2, start, 0.0), axis=-1, keepdims=True)
    return pos1, pos2


def _dispatch_kernel(loff_s, goff_s, ngran_s, ntot_s, tstart_s, tn_s, v2_ref, route_ref, loffv_ref,
                     ltri_ref, xs_ref, posw_ref, loc_ref, zero_ref, sem):
    i = pl.program_id(0)
    last = pl.num_programs(0) - 1
    slot = i % 2
    rows = loc_ref.shape[1]
    route = route_ref[...]
    pos1, pos2 = _local_positions(route, loffv_ref[0], ltri_ref[...])
    lane = lax.broadcasted_iota(jnp.int32, route.shape, 1)
    posw = jnp.where(lane == 0, pos1, jnp.where(lane == 1, pos2, route))
    posw_ref[...] = posw
    pos_t = posw.T
    riota = lax.broadcasted_iota(jnp.int32, (rows, 1), 0).astype(F32)
    onehot = ((riota == pos_t[0:1]) | (riota == pos_t[1:2])).astype(BF16)
    loc_ref[slot] = jnp.dot(onehot, v2_ref[...], preferred_element_type=F32).astype(BF16)

    def out_copy(s):
        return lambda lo, go: pltpu.make_async_copy(
            loc_ref.at[s, pl.ds(lo, GRAN)], xs_ref.at[pl.ds(go, GRAN)], sem.at[s])

    _run_copies(i, loff_s, goff_s, ngran_s, out_copy(slot))

    @pl.when(i > 0)
    def _():
        _wait_copies(ntot_s[i - 1], out_copy(1 - slot))

    @pl.when(i == last)
    def _():
        _wait_copies(ntot_s[i], out_copy(slot))
        zero_ref[...] = jnp.zeros(zero_ref.shape, BF16)

        def zero_copy(go):
            return pltpu.make_async_copy(zero_ref, xs_ref.at[pl.ds(go, GRAN)], sem.at[0])

        def per_expert(e, n_started):
            def per_granule(g, carry):
                zero_copy(pl.multiple_of(tstart_s[e] + g * GRAN, GRAN)).start()
                return carry
            lax.fori_loop(0, tn_s[e], per_granule, 0)
            return n_started + tn_s[e]

        nz = lax.fori_loop(0, N_EXPERTS, per_expert, 0)
        _wait_copies(nz, lambda lo, go: zero_copy(go))


def _dispatch(v2, route, meta, loffv, ltri, cap_tiles, tm, rows):
    t, d = v2.shape
    ntile = t // tm
    grid_spec = pltpu.PrefetchScalarGridSpec(
        num_scalar_prefetch=6,
        grid=(ntile,),
        in_specs=[pl.BlockSpec((tm, d), lambda i, *_: (i, 0)),
                  pl.BlockSpec((tm, LANES), lambda i, *_: (i, 0)),
                  pl.BlockSpec((1, 1, LANES), lambda i, *_: (i, 0, 0)),
                  pl.BlockSpec((tm, tm), lambda i, *_: (0, 0))],
        out_specs=[pl.BlockSpec(memory_space=pl.ANY),
                   pl.BlockSpec((tm, LANES), lambda i, *_: (i, 0))],
        scratch_shapes=[pltpu.VMEM((2, rows, d), BF16), pltpu.VMEM((GRAN, d), BF16),
                        pltpu.SemaphoreType.DMA((2,))],
    )
    return pl.pallas_call(
        _dispatch_kernel,
        grid_spec=grid_spec,
        out_shape=[jax.ShapeDtypeStruct((cap_tiles * MOE_ROW_TILE, d), BF16),
                   jax.ShapeDtypeStruct((t, LANES), F32)],
        compiler_params=_cparams(("arbitrary",)),
        name="dispatch",
    )(meta["loff"], meta["goff"], meta["ngran"], meta["ntot"], meta["tail_start"], meta["tail_n"],
      v2, route, loffv, ltri)


def _experts_kernel(texp_s, nvalid_s, x_ref, wgu_ref, wd_ref, y_ref):
    del texp_s

    @pl.when(pl.program_id(0) < nvalid_s[0])
    def _():
        hc = jnp.dot(x_ref[...], wgu_ref[0], preferred_element_type=F32)
        g = hc[:, :D_EXPERT]
        hid = (g * jax.nn.sigmoid(g) * hc[:, D_EXPERT:]).astype(BF16)
        y_ref[...] = jnp.dot(hid, wd_ref[0], preferred_element_type=F32).astype(BF16)

    @pl.when(pl.program_id(0) >= nvalid_s[0])
    def _():
        y_ref[...] = jnp.zeros(y_ref.shape, BF16)


def _experts(xs, wgu, wd, meta, cap_tiles):
    d = xs.shape[1]
    row = lambda j, texp, nv: (jnp.minimum(j, nv[0] - 1), 0)
    wsel = lambda j, texp, nv: (texp[jnp.minimum(j, nv[0] - 1)], 0, 0)
    grid_spec = pltpu.PrefetchScalarGridSpec(
        num_scalar_prefetch=2,
        grid=(cap_tiles,),
        in_specs=[pl.BlockSpec((MOE_ROW_TILE, d), row),
                  pl.BlockSpec((1, d, 2 * D_EXPERT), wsel),
                  pl.BlockSpec((1, D_EXPERT, d), wsel)],
        out_specs=pl.BlockSpec((MOE_ROW_TILE, d), lambda j, texp, nv: (j, 0)),
    )
    return pl.pallas_call(
        _experts_kernel,
        grid_spec=grid_spec,
        out_shape=jax.ShapeDtypeStruct(xs.shape, BF16),
        compiler_params=_cparams(("arbitrary",)),
        name="experts",
    )(meta["tile_expert"], meta["n_valid"], xs, wgu, wd)


def _combine_kernel(loff_s, goff_s, ngran_s, ntot_s, posw_ref, ys_ref, h1_ref, gt2_ref, shf_ref,
                    scf_ref, gfin_ref, o_ref, loc_ref, sem):
    i = pl.program_id(0)
    slot = i % 2
    rows = loc_ref.shape[1]

    def in_copy(s):
        return lambda lo, go: pltpu.make_async_copy(
            ys_ref.at[pl.ds(go, GRAN)], loc_ref.at[s, pl.ds(lo, GRAN)], sem.at[s])

    def fetch(tile, s):
        loc_ref[s] = jnp.zeros(loc_ref.shape[1:], BF16)
        _run_copies(tile, loff_s, goff_s, ngran_s, in_copy(s))

    @pl.when(i == 0)
    def _():
        fetch(0, 0)

    @pl.when(i + 1 < pl.num_programs(0))
    def _():
        fetch(i + 1, 1 - slot)

    _wait_copies(ntot_s[i], in_copy(slot))

    posw = posw_ref[...]
    riota = lax.broadcasted_iota(jnp.int32, (1, rows), 1).astype(F32)
    qw = (jnp.where(riota == posw[:, 0:1], posw[:, 2:3], 0.0)
          + jnp.where(riota == posw[:, 1:2], posw[:, 3:4], 0.0)).astype(BF16)
    moe = jnp.dot(qw, loc_ref[slot], preferred_element_type=F32)
    h2 = h1_ref[...] + gt2_ref[0] * moe
    o_ref[...] = _rms(h2, gfin_ref[...]) * (1.0 + scf_ref[0]) + shf_ref[0]


def _combine(posw, ys, h1, gt2, shf, scf, gfin, meta, tm, rows, tiles_per_batch):
    t, d = h1.shape
    per_b = pl.BlockSpec((1, 1, d), lambda i, *_: (i // tiles_per_batch, 0, 0))
    grid_spec = pltpu.PrefetchScalarGridSpec(
        num_scalar_prefetch=4,
        grid=(t // tm,),
        in_specs=[pl.BlockSpec((tm, LANES), lambda i, *_: (i, 0)),
                  pl.BlockSpec(memory_space=pl.ANY),
                  pl.BlockSpec((tm, d), lambda i, *_: (i, 0)),
                  per_b, per_b, per_b, pl.BlockSpec((1, d), lambda i, *_: (0, 0))],
        out_specs=pl.BlockSpec((tm, d), lambda i, *_: (i, 0)),
        scratch_shapes=[pltpu.VMEM((2, rows, d), BF16), pltpu.SemaphoreType.DMA((2,))],
    )
    return pl.pallas_call(
        _combine_kernel,
        grid_spec=grid_spec,
        out_shape=jax.ShapeDtypeStruct((t, d), F32),
        compiler_params=_cparams(("arbitrary",)),
        name="combine",
    )(meta["loff"], meta["goff"], meta["ngran"], meta["ntot"], posw, ys, h1, gt2, shf, scf, gfin)


def _rot_cols(w):
    half = ROPE // 2
    return jnp.concatenate([-w[..., half:], w[..., :half]], axis=-1)


def _pad_cols(w, lo, width):
    return jnp.pad(w, [(0, 0)] * (w.ndim - 1) + [(lo, width - lo - w.shape[-1])])


def _prep_weights(w_in, w_uq, w_uk, w_uv):
    o2, o3 = Q_LORA + KV_LORA, Q_LORA + KV_LORA + ROPE
    w_kr = w_in[:, o2:o3]
    wcat = jnp.concatenate([
        w_in[:, :o2],
        _pad_cols(w_kr, NOPE, HEAD_PAD), _pad_cols(_rot_cols(w_kr), NOPE, HEAD_PAD),
        w_in[:, o3:]], axis=1).astype(BF16)
    wq_a = _pad_cols(w_uq, 0, HEAD_PAD)
    wq_b = _pad_cols(_rot_cols(w_uq[..., NOPE:]), NOPE, HEAD_PAD)
    r = w_uq.shape[0]
    wqt = jnp.concatenate([wq_a.reshape(r, -1), wq_b.reshape(r, -1)], axis=1).T.astype(BF16)
    wk = _pad_cols(w_uk, 0, HEAD_PAD).reshape(r, -1).astype(BF16)
    wvt = _pad_cols(w_uv, 0, V_ROWS).reshape(r, -1).T.astype(BF16)
    ones = np.zeros((N_HEADS, V_ROWS), np.float32)
    ones[:, V_DIM] = 1.0
    return wcat, wqt, wk, wvt, jnp.asarray(ones.reshape(-1, 1))


def kernel(x, c, positions, w_ada, b_ada, g_norm_mix, w_in, g_q_lat, g_kv_lat, w_uq, w_uk, w_uv, w_branch_attn, w_branch_fourier, b_gates, w_out, g_norm_ffn, w_router_group, b_router_group, w_router_expert, b_router_expert, w_expert_gate, w_expert_up, w_expert_down, w_ada_final, b_ada_final, g_norm_final):
    bsz, s, d = x.shape
    assert w_ada.shape[0] == 1 and s % FFT_N2 == 0
    tm = min(512, s)

    mod = _mods(c, w_ada[0], b_ada[0])
    fmod = _mods(c, w_ada_final, b_ada_final)
    sh1, sc1, gt1, sh2, sc2, gt2 = [m.reshape(bsz, 1, d) for m in jnp.split(mod, 6, axis=-1)]
    shf, scf = [m.reshape(bsz, 1, d) for m in jnp.split(fmod, 2, axis=-1)]

    wcat, wqt, wk, wvt, vones = _prep_weights(w_in[0], w_uq[0], w_uk[0], w_uv[0])
    bd, m1, wb = _dft_constants(s)
    inv_freq = ROPE_THETA ** (-jnp.arange(0, ROPE, 2, dtype=F32) / ROPE)
    invf = _pad_cols(jnp.concatenate([inv_freq, inv_freq])[None, :], NOPE, HEAD_PAD).reshape(HEAD_PAD, 1)

    qt, k, vt, zr, zi, gates = _inproj(
        x, positions.reshape(bsz, 1, s), sh1, sc1, g_norm_mix[0][None, :], wcat,
        g_q_lat[0][None, :], g_kv_lat[0][None, :], wqt, wk, wvt, vones, bd, b_gates[0][None, :],
        invf, tm)

    attn = _attention(qt, k, vt, tq=min(256, s))
    br, bi = _fft_a(zr, zi, m1, ts2=8)
    four = _fft_b(br, bi, wb, tk1=16)

    wr = jnp.concatenate([w_router_expert[0].reshape(d, N_EXPERTS), w_router_group[0]], axis=1)
    wr = _pad_cols(wr, 0, LANES)
    wr_top = lax.bitcast_convert_type(
        lax.bitcast_convert_type(wr, jnp.uint32) & jnp.uint32(0xFFFF0000), F32)
    wr_hi = wr_top.astype(BF16)
    wr_lo = (wr - wr_top).astype(BF16)
    wr = jnp.concatenate([jnp.concatenate([wr_hi, wr_lo], axis=1),
                          jnp.concatenate([wr_hi, jnp.zeros_like(wr_lo)], axis=1)], axis=0)
    brt = _pad_cols(jnp.concatenate([b_router_expert[0].reshape(-1), b_router_group[0]])[None, :], 0, LANES)
    h1, v2, route, cnt = _merge(attn, four, gates, x, w_branch_attn[0].astype(BF16),
                                w_branch_fourier[0].astype(BF16), w_out[0].astype(BF16), gt1, sh2,
                                sc2, g_norm_ffn[0][None, :], wr, brt, tm)

    t = bsz * s
    meta, loffv, cap_tiles = _moe_metadata(cnt.reshape(-1, LANES), tm)
    rows = 2 * tm + N_EXPERTS * GRAN
    loffv = _pad_cols(loffv, 0, LANES).reshape(-1, 1, LANES)
    ltri = jnp.asarray(np.tril(np.ones((tm, tm), np.float32), -1), BF16)
    xs, posw = _dispatch(v2.reshape(t, d), route.reshape(t, LANES), meta, loffv, ltri, cap_tiles,
                         tm, rows)
    wgu = jnp.concatenate([w_expert_gate[0], w_expert_up[0]], axis=-1).astype(BF16)
    wd = w_expert_down[0].astype(BF16)
    ys = _experts(xs, wgu, wd, meta, cap_tiles)
    out = _combine(posw, ys, h1.reshape(t, d), gt2, shf, scf, g_norm_final[None, :], meta, tm, rows,
                   s // tm)
    return out.reshape(bsz, s, d)
```

```python
import functools

import numpy as np
import jax
import jax.numpy as jnp
from jax import lax
from jax.experimental import pallas as pl
from jax.experimental.pallas import tpu as pltpu

F32 = jnp.float32
BF16 = jnp.bfloat16

N_HEADS = 8
NOPE = 64
ROPE = 32
V_DIM = 64
HEAD_PAD = 128
V_ROWS = 128
Q_LORA = 256
KV_LORA = 256
ROPE_THETA = 10000.0
F_GROUPS = 8
F_GDIM = 64
F_WIDTH = F_GROUPS * F_GDIM
N_GROUPS = 4
E_PER_GROUP = 8
N_EXPERTS = N_GROUPS * E_PER_GROUP
D_EXPERT = 256
EPS = 1e-6
FFT_N2 = 64
LANES = 128
MERGE_ROWS = 128
ATTN_DEPTH = 1
QK_ROWS = 256
VMEM_LIMIT = 56 * 1024 * 1024


def _cparams(sem):
    return pltpu.CompilerParams(dimension_semantics=sem, vmem_limit_bytes=VMEM_LIMIT)


def _mods_kernel(c_ref, w_ref, b_ref, o_ref):
    c = c_ref[...]
    ca = c * jax.nn.sigmoid(c)
    o_ref[...] = jnp.dot(ca, w_ref[...], preferred_element_type=F32,
                         precision=lax.Precision.HIGHEST) + b_ref[...]


def _mods(c, w, b):
    bsz, d = c.shape
    n = w.shape[1]
    tn = 1024
    return pl.pallas_call(
        _mods_kernel,
        grid=(n // tn,),
        in_specs=[pl.BlockSpec((bsz, d), lambda j: (0, 0)),
                  pl.BlockSpec((d, tn), lambda j: (0, j)),
                  pl.BlockSpec((1, tn), lambda j: (0, j))],
        out_specs=pl.BlockSpec((bsz, tn), lambda j: (0, j)),
        out_shape=jax.ShapeDtypeStruct((bsz, n), F32),
        compiler_params=_cparams(("arbitrary",)),
        name="mods",
    )(c, w, b.reshape(1, n))


def _rms(x, g):
    r = lax.rsqrt(jnp.mean(x * x, axis=-1, keepdims=True) + EPS)
    return x * r * g


def _inproj_kernel(x_ref, pos_ref, sh_ref, sc_ref, gmix_ref, wcat_ref, gq_ref, gkv_ref,
                   wqt_ref, wk_ref, wvt_ref, vones_ref, bd_ref, bg_ref, invf_ref,
                   qt_ref, k_ref, vt_ref, zr_ref, zi_ref, gate_ref, *, scale):
    hp = N_HEADS * HEAD_PAD
    nt_dims = (((1,), (1,)), ((), ()))
    u = _rms(x_ref[0], gmix_ref[...]) * (1.0 + sc_ref[0]) + sh_ref[0]
    ub = u.astype(BF16)

    o_q, o_kv, o_kr, o_f, o_g = 0, Q_LORA, Q_LORA + KV_LORA, Q_LORA + KV_LORA + 2 * HEAD_PAD, \
        Q_LORA + KV_LORA + 2 * HEAD_PAD + F_WIDTH

    ang_t = invf_ref[...] * pos_ref[0].astype(F32)
    cos_t = jnp.cos(ang_t)
    sin_t = jnp.sin(ang_t)

    ql = jnp.dot(ub, wcat_ref[:, o_q:o_kv], preferred_element_type=F32)
    kvl = jnp.dot(ub, wcat_ref[:, o_kv:o_kr], preferred_element_type=F32)
    kr = jnp.dot(ub, wcat_ref[:, o_kr:o_f], preferred_element_type=F32)
    fin = jnp.dot(ub, wcat_ref[:, o_f:o_g], preferred_element_type=F32).astype(BF16)
    zg = jnp.dot(ub, wcat_ref[:, o_g:], preferred_element_type=F32) + bg_ref[...]

    qn = _rms(ql, gq_ref[...]).astype(BF16)
    kvn = _rms(kvl, gkv_ref[...]).astype(BF16)
    zz = jnp.dot(fin, bd_ref[...], preferred_element_type=F32)
    qab_t = lax.dot_general(wqt_ref[...], qn, nt_dims, preferred_element_type=F32)
    kn = jnp.dot(kvn, wk_ref[...], preferred_element_type=F32)
    vt = lax.dot_general(wvt_ref[...], kvn, nt_dims, preferred_element_type=F32) + vones_ref[...]

    gate_ref[0] = jax.nn.sigmoid(zg).astype(BF16)
    zr_ref[0] = zz[:, :F_WIDTH].astype(BF16)
    zi_ref[0] = zz[:, F_WIDTH:].astype(BF16)
    cq = cos_t * scale
    sq = sin_t * scale
    for h in range(N_HEADS):
        a = qab_t[h * HEAD_PAD:(h + 1) * HEAD_PAD]
        b = qab_t[hp + h * HEAD_PAD:hp + (h + 1) * HEAD_PAD]
        qt_ref[0, h] = (a * cq + b * sq).astype(BF16)
    krr = kr[:, :HEAD_PAD] * cos_t.T + kr[:, HEAD_PAD:] * sin_t.T
    for h in range(N_HEADS):
        k_ref[0, h] = (kn[:, h * HEAD_PAD:(h + 1) * HEAD_PAD] + krr).astype(BF16)
        vt_ref[0, h, 0] = vt[h * V_ROWS:(h + 1) * V_ROWS].astype(BF16)


def _inproj(x, pos_row, sh1, sc1, gmix, wcat, gq, gkv, wqt, wk, wvt, vones, bd, bg, invf, tm):
    bsz, s, d = x.shape
    hp = N_HEADS * HEAD_PAD
    nt = s // tm
    const = lambda shape: pl.BlockSpec(shape, lambda b, i: (0,) * len(shape))
    tok = lambda w: pl.BlockSpec((1, tm, w), lambda b, i: (b, i, 0))
    per_b = pl.BlockSpec((1, 1, d), lambda b, i: (b, 0, 0))
    scale = float((NOPE + ROPE) ** -0.5 * np.log2(np.e))
    return pl.pallas_call(
        functools.partial(_inproj_kernel, scale=scale),
        grid=(bsz, nt),
        in_specs=[tok(d), pl.BlockSpec((1, 1, tm), lambda b, i: (b, 0, i)), per_b, per_b,
                  const((1, d)), const(wcat.shape), const((1, Q_LORA)), const((1, KV_LORA)),
                  const(wqt.shape), const(wk.shape), const(wvt.shape), const(vones.shape),
                  const(bd.shape), const((1, bg.shape[1])), const((HEAD_PAD, 1))],
        out_specs=[pl.BlockSpec((1, N_HEADS, HEAD_PAD, tm), lambda b, i: (b, 0, 0, i)),
                   pl.BlockSpec((1, N_HEADS, tm, HEAD_PAD), lambda b, i: (b, 0, i, 0)),
                   pl.BlockSpec((1, N_HEADS, 1, V_ROWS, tm), lambda b, i: (b, 0, i, 0, 0)),
                   tok(F_WIDTH), tok(F_WIDTH), tok(2 * d)],
        out_shape=[jax.ShapeDtypeStruct((bsz, N_HEADS, HEAD_PAD, s), BF16),
                   jax.ShapeDtypeStruct((bsz, N_HEADS, s, HEAD_PAD), BF16),
                   jax.ShapeDtypeStruct((bsz, N_HEADS, nt, V_ROWS, tm), BF16),
                   jax.ShapeDtypeStruct((bsz, s, F_WIDTH), BF16),
                   jax.ShapeDtypeStruct((bsz, s, F_WIDTH), BF16),
                   jax.ShapeDtypeStruct((bsz, s, 2 * d), BF16)],
        compiler_params=_cparams(("parallel", "parallel")),
        name="inproj",
    )(x, pos_row, sh1, sc1, gmix, wcat, gq, gkv, wqt, wk, wvt, vones, bd, bg, invf)


def _attn_kernel(qt_ref, k_ref, vt_ref, o_ref, *p_refs, nk):
    tq = qt_ref.shape[3]
    tk = vt_ref.shape[4]
    qts = [qt_ref[0, hh] for hh in range(2)]
    nbuf = ATTN_DEPTH + 1

    def scores(c, hh):
        ss = []
        for i in range(tk // QK_ROWS):
            off = c * tk + i * QK_ROWS
            ss.append(jnp.dot(k_ref[0, hh, off:off + QK_ROWS, :], qts[hh],
                              preferred_element_type=F32))
        return ss

    def probs(ss, m, p_ref):
        m_new = m
        for s in ss:
            m_new = jnp.maximum(m_new, jnp.max(s, axis=0, keepdims=True))
        for i, s in enumerate(ss):
            p_ref[i * QK_ROWS:(i + 1) * QK_ROWS, :] = jnp.exp2(s - m_new).astype(BF16)
        return m_new, jnp.exp2(m - m_new)

    def accumulate(c, hh, acc, alpha, p_ref):
        return alpha * acc + jnp.dot(vt_ref[0, hh, c], p_ref[...], preferred_element_type=F32)

    hs = range(2)
    buf = lambda c, hh: p_refs[hh * nbuf + c % nbuf]
    m = [jnp.full((1, tq), -jnp.inf, F32) for _ in hs]
    alpha = [[None] * nk for _ in hs]
    accs = [jnp.zeros((V_ROWS, tq), F32) for _ in hs]
    for c in range(nk + ATTN_DEPTH):
        if c < nk:
            s = [scores(c, hh) for hh in hs]
        if c >= ATTN_DEPTH:
            for hh in hs:
                accs[hh] = accumulate(c - ATTN_DEPTH, hh, accs[hh], alpha[hh][c - ATTN_DEPTH],
                                      buf(c - ATTN_DEPTH, hh))
        if c < nk:
            for hh in hs:
                m[hh], alpha[hh][c] = probs(s[hh], m[hh], buf(c, hh))
    out_t = jnp.concatenate([a[:V_DIM] / a[V_DIM:V_DIM + 1] for a in accs], axis=0)
    o_ref[0] = out_t.T.astype(BF16)


def _attention(qt, k, vt, tq):
    bsz, nh, s, _ = k.shape
    nk, tk = vt.shape[2], vt.shape[4]
    return pl.pallas_call(
        functools.partial(_attn_kernel, nk=nk),
        grid=(bsz, nh // 2, s // tq),
        in_specs=[pl.BlockSpec((1, 2, HEAD_PAD, tq), lambda b, j, i: (b, j, 0, i)),
                  pl.BlockSpec((1, 2, s, HEAD_PAD), lambda b, j, i: (b, j, 0, 0)),
                  pl.BlockSpec((1, 2, nk, V_ROWS, tk), lambda b, j, i: (b, j, 0, 0, 0))],
        out_specs=pl.BlockSpec((1, tq, HEAD_PAD), lambda b, j, i: (b, i, j)),
        out_shape=jax.ShapeDtypeStruct((bsz, s, nh * V_DIM), BF16),
        scratch_shapes=[pltpu.VMEM((tk, tq), BF16)] * (2 * (ATTN_DEPTH + 1)),
        compiler_params=_cparams(("parallel", "parallel", "arbitrary")),
        name="attention",
    )(qt, k, vt)


def _fft_a_kernel(zr_ref, zi_ref, m1_ref, br_ref, bi_ref):
    n1 = zr_ref.shape[1]
    ts2 = m1_ref.shape[0]
    for j in range(ts2):
        sl = slice(j * F_WIDTH, (j + 1) * F_WIDTH)
        xx = jnp.concatenate([zr_ref[0, :, sl], zi_ref[0, :, sl]], axis=0)
        y = jnp.dot(m1_ref[j], xx, preferred_element_type=F32)
        br_ref[0, :, sl] = y[:n1].astype(BF16)
        bi_ref[0, :, sl] = y[n1:].astype(BF16)


def _fft_a(zr, zi, m1, ts2):
    bsz, s, _ = zr.shape
    n1 = s // FFT_N2
    zr2 = zr.reshape(bsz, n1, FFT_N2 * F_WIDTH)
    zi2 = zi.reshape(bsz, n1, FFT_N2 * F_WIDTH)
    blk = pl.BlockSpec((1, n1, ts2 * F_WIDTH), lambda b, j: (b, 0, j))
    return pl.pallas_call(
        _fft_a_kernel,
        grid=(bsz, FFT_N2 // ts2),
        in_specs=[blk, blk, pl.BlockSpec((ts2, 2 * n1, 2 * n1), lambda b, j: (j, 0, 0))],
        out_specs=[blk, blk],
        out_shape=[jax.ShapeDtypeStruct(zr2.shape, BF16)] * 2,
        compiler_params=_cparams(("parallel", "parallel")),
        name="fft_a",
    )(zr2, zi2, m1)


def _fft_b_kernel(br_ref, bi_ref, wb_ref, o_ref):
    tk1 = br_ref.shape[1]
    for j in range(tk1):
        xx = jnp.concatenate([br_ref[0, j], bi_ref[0, j]], axis=0)
        y = jnp.dot(wb_ref[...], xx, preferred_element_type=F32)
        o_ref[0, :, j * F_WIDTH:(j + 1) * F_WIDTH] = y.astype(BF16)


def _fft_b(br, bi, wb, tk1):
    bsz, n1, _ = br.shape
    br4 = br.reshape(bsz, n1, FFT_N2, F_WIDTH)
    bi4 = bi.reshape(bsz, n1, FFT_N2, F_WIDTH)
    blk = pl.BlockSpec((1, tk1, FFT_N2, F_WIDTH), lambda b, j: (b, j, 0, 0))
    out = pl.pallas_call(
        _fft_b_kernel,
        grid=(bsz, n1 // tk1),
        in_specs=[blk, blk, pl.BlockSpec((FFT_N2, 2 * FFT_N2), lambda b, j: (0, 0))],
        out_specs=pl.BlockSpec((1, FFT_N2, tk1 * F_WIDTH), lambda b, j: (b, 0, j)),
        out_shape=jax.ShapeDtypeStruct((bsz, FFT_N2, n1 * F_WIDTH), BF16),
        compiler_params=_cparams(("parallel", "parallel")),
        name="fft_b",
    )(br4, bi4, wb)
    return out.reshape(bsz, n1 * FFT_N2, F_WIDTH)


def _dft_constants(s):
    n1, n2 = s // FFT_N2, FFT_N2
    m = np.arange(F_GDIM)
    th = 2.0 * np.pi * np.outer(m, m) / F_GDIM
    eye = np.eye(F_GROUPS)
    bd = np.concatenate([np.kron(eye, np.cos(th)), -np.kron(eye, np.sin(th))], axis=1) / np.sqrt(F_GDIM)
    k1 = np.arange(n1)[None, :, None]
    s1 = np.arange(n1)[None, None, :]
    s2 = np.arange(n2)[:, None, None]
    ph = 2.0 * np.pi * ((k1 * (n2 * s1 + s2)) % s) / s
    cr, ci = np.cos(ph) / np.sqrt(n1), -np.sin(ph) / np.sqrt(n1)
    m1 = np.concatenate([np.concatenate([cr, -ci], axis=2), np.concatenate([ci, cr], axis=2)], axis=1)
    k2 = np.arange(n2)
    tb = 2.0 * np.pi * np.outer(k2, k2) / n2
    wb = np.concatenate([np.cos(tb), np.sin(tb)], axis=1) / np.sqrt(n2)
    return (jnp.asarray(bd, BF16), jnp.asarray(m1, BF16), jnp.asarray(wb, BF16))


def _route(parts, bias):
    logit = parts[:, :LANES] + parts[:, LANES:] + bias
    lane = lax.broadcasted_iota(jnp.int32, logit.shape, 1)
    neg = -jnp.inf
    big = jnp.int32(LANES)

    def softmax_masked(mask):
        z = jnp.where(mask, logit, neg)
        e = jnp.exp(z - jnp.max(z, axis=-1, keepdims=True))
        return e / jnp.sum(e, axis=-1, keepdims=True)

    def top1(p, mask):
        pm = jnp.where(mask, p, -1.0)
        best = jnp.max(pm, axis=-1, keepdims=True)
        idx = jnp.min(jnp.where(pm == best, lane, big), axis=-1, keepdims=True)
        return best, idx

    gmask = (lane >= N_EXPERTS) & (lane < N_EXPERTS + N_GROUPS)
    g_prob = softmax_masked(gmask)
    g_p, g_lane = top1(g_prob, gmask)
    e_lo = (g_lane - N_EXPERTS) * E_PER_GROUP
    emask = (lane >= e_lo) & (lane < e_lo + E_PER_GROUP)
    e_prob = softmax_masked(emask)
    p1, i1 = top1(e_prob, emask)
    p2, i2 = top1(e_prob, emask & (lane != i1))
    den = p1 + p2
    w1 = g_p * (p1 / den)
    w2 = g_p * (p2 / den)
    route = jnp.where(lane == 0, i1.astype(F32), jnp.where(lane == 1, i2.astype(F32),
                      jnp.where(lane == 2, w1, jnp.where(lane == 3, w2, 0.0))))
    cnt = jnp.sum(((lane == i1) | (lane == i2)).astype(F32), axis=0, keepdims=True)
    return route, cnt


def _merge_kernel(at_ref, fo_ref, gate_ref, x_ref, wba_ref, wbf_ref, wout_ref, gt1_ref, sh2_ref,
                  sc2_ref, gffn_ref, wr_ref, br_ref, h1_ref, v2_ref, route_ref, cnt_ref):
    d = x_ref.shape[-1]
    tm = x_ref.shape[1]
    blocks = [slice(i * MERGE_ROWS, (i + 1) * MERGE_ROWS) for i in range(tm // MERGE_ROWS)]
    a = [jnp.dot(at_ref[0, r, :], wba_ref[...], preferred_element_type=F32) for r in blocks]
    fo = [jnp.dot(fo_ref[0, r, :], wbf_ref[...], preferred_element_type=F32) for r in blocks]
    y = []
    for r, ai, fi in zip(blocks, a, fo):
        merged = gate_ref[0, r, :d].astype(F32) * ai + gate_ref[0, r, d:].astype(F32) * fi
        y.append(jnp.dot(merged.astype(BF16), wout_ref[...], preferred_element_type=F32))
    parts = []
    for r, yi in zip(blocks, y):
        h1 = x_ref[0, r, :] + gt1_ref[0] * yi
        h1_ref[0, r, :] = h1
        v2 = _rms(h1, gffn_ref[...]) * (1.0 + sc2_ref[0]) + sh2_ref[0]
        v_hi = v2.astype(BF16)
        v2_ref[0, r, :] = v_hi
        v_lo = (v2 - v_hi.astype(F32)).astype(BF16)
        parts.append(jnp.dot(jnp.concatenate([v_hi, v_lo], axis=1), wr_ref[...],
                             preferred_element_type=F32))
    cnt = None
    for r, pi in zip(blocks, parts):
        route, c = _route(pi, br_ref[...])
        route_ref[0, r, :] = route
        cnt = c if cnt is None else cnt + c
    cnt_ref[0, 0] = cnt


def _merge(attn, four, gates, x, wba, wbf, wout, gt1, sh2, sc2, gffn, wr, br, tm):
    bsz, s, d = x.shape
    const = lambda shape: pl.BlockSpec(shape, lambda b, i: (0,) * len(shape))
    tok = lambda w: pl.BlockSpec((1, tm, w), lambda b, i: (b, i, 0))
    per_b = pl.BlockSpec((1, 1, d), lambda b, i: (b, 0, 0))
    return pl.pallas_call(
        _merge_kernel,
        grid=(bsz, s // tm),
        in_specs=[tok(attn.shape[-1]), tok(F_WIDTH), tok(2 * d), tok(d), const(wba.shape),
                  const(wbf.shape), const(wout.shape), per_b, per_b, per_b, const((1, d)),
                  const(wr.shape), const((1, LANES))],
        out_specs=[tok(d), tok(d), tok(LANES),
                   pl.BlockSpec((1, 1, 1, LANES), lambda b, i: (b, i, 0, 0))],
        out_shape=[jax.ShapeDtypeStruct((bsz, s, d), F32), jax.ShapeDtypeStruct((bsz, s, d), BF16),
                   jax.ShapeDtypeStruct((bsz, s, LANES), F32),
                   jax.ShapeDtypeStruct((bsz, s // tm, 1, LANES), F32)],
        compiler_params=_cparams(("parallel", "parallel")),
        name="merge",
    )(attn, four, gates, x, wba, wbf, wout, gt1, sh2, sc2, gffn, wr, br)


GRAN = 16
MOE_ROW_TILE = 512


def _moe_metadata(cnt, tm):
    ntile = cnt.shape[0]

    def cumsum(a, axis):
        n = a.shape[axis]
        tri = jnp.asarray(np.triu(np.ones((n, n), np.float32)))
        af = a.astype(F32)
        out = (jnp.dot(af, tri, precision=lax.Precision.HIGHEST) if axis == a.ndim - 1
               else jnp.dot(tri.T, af, precision=lax.Precision.HIGHEST))
        return out.astype(jnp.int32)

    c = cnt[:, :N_EXPERTS].astype(jnp.int32)
    pc = (c + GRAN - 1) // GRAN * GRAN
    loff = cumsum(pc, 1) - pc
    tot = jnp.sum(pc, axis=0)
    tot_r = (tot + MOE_ROW_TILE - 1) // MOE_ROW_TILE * MOE_ROW_TILE
    ends = cumsum(tot_r[None, :], 1)[0]
    base = ends - tot_r
    goff = base[None, :] + cumsum(pc, 0) - pc
    cap_tiles = (2 * ntile * tm + ntile * N_EXPERTS * GRAN) // MOE_ROW_TILE + N_EXPERTS
    n_valid = (ends[-1] // MOE_ROW_TILE).astype(jnp.int32)
    tile_start = jnp.arange(cap_tiles, dtype=jnp.int32) * MOE_ROW_TILE
    tile_expert = jnp.minimum(jnp.sum((ends[None, :] <= tile_start[:, None]).astype(jnp.int32), axis=1),
                              N_EXPERTS - 1)
    flat = lambda a: a.reshape(-1).astype(jnp.int32)
    tail = tot_r - tot + jnp.where(jnp.arange(N_EXPERTS) == N_EXPERTS - 1,
                                   cap_tiles * MOE_ROW_TILE - ends[-1], 0)
    meta = dict(loff=flat(loff), goff=flat(goff), ngran=flat(pc // GRAN),
                ntot=flat(jnp.sum(pc, axis=1) // GRAN),
                tail_start=flat(base + tot), tail_n=flat(tail // GRAN),
                tile_expert=tile_expert, n_valid=n_valid.reshape(1))
    return meta, loff.astype(F32), cap_tiles


def _run_copies(i, loff_s, goff_s, ngran_s, make_copy):
    def per_expert(e, n_started):
        n = ngran_s[i * N_EXPERTS + e]
        lo = loff_s[i * N_EXPERTS + e]
        go = goff_s[i * N_EXPERTS + e]

        def per_granule(g, carry):
            make_copy(pl.multiple_of(lo + g * GRAN, GRAN), pl.multiple_of(go + g * GRAN, GRAN)).start()
            return carry

        lax.fori_loop(0, n, per_granule, 0)
        return n_started + n

    return lax.fori_loop(0, N_EXPERTS, per_expert, 0)


def _wait_copies(n, make_copy):
    def one(_, carry):
        make_copy(0, 0).wait()
        return carry
    lax.fori_loop(0, n, one, 0)


def _local_positions(route, loffv, ltri):
    lane = lax.broadcasted_iota(jnp.int32, route.shape, 1).astype(F32)
    o1 = lane == route[:, 0:1]
    o2 = lane == route[:, 1:2]
    before = jnp.dot(ltri, (o1 | o2).astype(BF16), preferred_element_type=F32)
    start = before + loffv
    pos1 = jnp.sum(jnp.where(o1, start, 0.0), axis=-1, keepdims=True)
    pos2 = jnp.sum(jnp.where(o2, start, 0.0), axis=-1, keepdims=True)
    return pos1, pos2


def _dispatch_kernel(loff_s, goff_s, ngran_s, ntot_s, tstart_s, tn_s, v2_ref, route_ref, loffv_ref,
                     ltri_ref, xs_ref, posw_ref, loc_ref, zero_ref, sem):
    i = pl.program_id(0)
    last = pl.num_programs(0) - 1
    slot = i % 2
    rows = loc_ref.shape[1]
    route = route_ref[...]
    pos1, pos2 = _local_positions(route, loffv_ref[0], ltri_ref[...])
    lane = lax.broadcasted_iota(jnp.int32, route.shape, 1)
    posw = jnp.where(lane == 0, pos1, jnp.where(lane == 1, pos2, route))
    posw_ref[...] = posw
    pos_t = posw.T
    riota = lax.broadcasted_iota(jnp.int32, (rows, 1), 0).astype(F32)
    onehot = ((riota == pos_t[0:1]) | (riota == pos_t[1:2])).astype(BF16)
    loc_ref[slot] = jnp.dot(onehot, v2_ref[...], preferred_element_type=F32).astype(BF16)

    def out_copy(s):
        return lambda lo, go: pltpu.make_async_copy(
            loc_ref.at[s, pl.ds(lo, GRAN)], xs_ref.at[pl.ds(go, GRAN)], sem.at[s])

    _run_copies(i, loff_s, goff_s, ngran_s, out_copy(slot))

    @pl.when(i > 0)
    def _():
        _wait_copies(ntot_s[i - 1], out_copy(1 - slot))

    @pl.when(i == last)
    def _():
        _wait_copies(ntot_s[i], out_copy(slot))
        zero_ref[...] = jnp.zeros(zero_ref.shape, BF16)

        def zero_copy(go):
            return pltpu.make_async_copy(zero_ref, xs_ref.at[pl.ds(go, GRAN)], sem.at[0])

        def per_expert(e, n_started):
            def per_granule(g, carry):
                zero_copy(pl.multiple_of(tstart_s[e] + g * GRAN, GRAN)).start()
                return carry
            lax.fori_loop(0, tn_s[e], per_granule, 0)
            return n_started + tn_s[e]

        nz = lax.fori_loop(0, N_EXPERTS, per_expert, 0)
        _wait_copies(nz, lambda lo, go: zero_copy(go))


def _dispatch(v2, route, meta, loffv, ltri, cap_tiles, tm, rows):
    t, d = v2.shape
    ntile = t // tm
    grid_spec = pltpu.PrefetchScalarGridSpec(
        num_scalar_prefetch=6,
        grid=(ntile,),
        in_specs=[pl.BlockSpec((tm, d), lambda i, *_: (i, 0)),
                  pl.BlockSpec((tm, LANES), lambda i, *_: (i, 0)),
                  pl.BlockSpec((1, 1, LANES), lambda i, *_: (i, 0, 0)),
                  pl.BlockSpec((tm, tm), lambda i, *_: (0, 0))],
        out_specs=[pl.BlockSpec(memory_space=pl.ANY),
                   pl.BlockSpec((tm, LANES), lambda i, *_: (i, 0))],
        scratch_shapes=[pltpu.VMEM((2, rows, d), BF16), pltpu.VMEM((GRAN, d), BF16),
                        pltpu.SemaphoreType.DMA((2,))],
    )
    return pl.pallas_call(
        _dispatch_kernel,
        grid_spec=grid_spec,
        out_shape=[jax.ShapeDtypeStruct((cap_tiles * MOE_ROW_TILE, d), BF16),
                   jax.ShapeDtypeStruct((t, LANES), F32)],
        compiler_params=_cparams(("arbitrary",)),
        name="dispatch",
    )(meta["loff"], meta["goff"], meta["ngran"], meta["ntot"], meta["tail_start"], meta["tail_n"],
      v2, route, loffv, ltri)


def _experts_kernel(texp_s, nvalid_s, x_ref, wgu_ref, wd_ref, y_ref):
    del texp_s

    @pl.when(pl.program_id(0) < nvalid_s[0])
    def _():
        hc = jnp.dot(x_ref[...], wgu_ref[0], preferred_element_type=F32)
        g = hc[:, :D_EXPERT]
        hid = (g * jax.nn.sigmoid(g) * hc[:, D_EXPERT:]).astype(BF16)
        y_ref[...] = jnp.dot(hid, wd_ref[0], preferred_element_type=F32).astype(BF16)

    @pl.when(pl.program_id(0) >= nvalid_s[0])
    def _():
        y_ref[...] = jnp.zeros(y_ref.shape, BF16)


def _experts(xs, wgu, wd, meta, cap_tiles):
    d = xs.shape[1]
    row = lambda j, texp, nv: (jnp.minimum(j, nv[0] - 1), 0)
    wsel = lambda j, texp, nv: (texp[jnp.minimum(j, nv[0] - 1)], 0, 0)
    grid_spec = pltpu.PrefetchScalarGridSpec(
        num_scalar_prefetch=2,
        grid=(cap_tiles,),
        in_specs=[pl.BlockSpec((MOE_ROW_TILE, d), row),
                  pl.BlockSpec((1, d, 2 * D_EXPERT), wsel),
                  pl.BlockSpec((1, D_EXPERT, d), wsel)],
        out_specs=pl.BlockSpec((MOE_ROW_TILE, d), lambda j, texp, nv: (j, 0)),
    )
    return pl.pallas_call(
        _experts_kernel,
        grid_spec=grid_spec,
        out_shape=jax.ShapeDtypeStruct(xs.shape, BF16),
        compiler_params=_cparams(("arbitrary",)),
        name="experts",
    )(meta["tile_expert"], meta["n_valid"], xs, wgu, wd)


def _combine_kernel(loff_s, goff_s, ngran_s, ntot_s, posw_ref, ys_ref, h1_ref, gt2_ref, shf_ref,
                    scf_ref, gfin_ref, o_ref, loc_ref, sem):
    i = pl.program_id(0)
    slot = i % 2
    rows = loc_ref.shape[1]

    def in_copy(s):
        return lambda lo, go: pltpu.make_async_copy(
            ys_ref.at[pl.ds(go, GRAN)], loc_ref.at[s, pl.ds(lo, GRAN)], sem.at[s])

    def fetch(tile, s):
        loc_ref[s] = jnp.zeros(loc_ref.shape[1:], BF16)
        _run_copies(tile, loff_s, goff_s, ngran_s, in_copy(s))

    @pl.when(i == 0)
    def _():
        fetch(0, 0)

    @pl.when(i + 1 < pl.num_programs(0))
    def _():
        fetch(i + 1, 1 - slot)

    _wait_copies(ntot_s[i], in_copy(slot))

    posw = posw_ref[...]
    riota = lax.broadcasted_iota(jnp.int32, (1, rows), 1).astype(F32)
    qw = (jnp.where(riota == posw[:, 0:1], posw[:, 2:3], 0.0)
          + jnp.where(riota == posw[:, 1:2], posw[:, 3:4], 0.0)).astype(BF16)
    moe = jnp.dot(qw, loc_ref[slot], preferred_element_type=F32)
    h2 = h1_ref[...] + gt2_ref[0] * moe
    o_ref[...] = _rms(h2, gfin_ref[...]) * (1.0 + scf_ref[0]) + shf_ref[0]


def _combine(posw, ys, h1, gt2, shf, scf, gfin, meta, tm, rows, tiles_per_batch):
    t, d = h1.shape
    per_b = pl.BlockSpec((1, 1, d), lambda i, *_: (i // tiles_per_batch, 0, 0))
    grid_spec = pltpu.PrefetchScalarGridSpec(
        num_scalar_prefetch=4,
        grid=(t // tm,),
        in_specs=[pl.BlockSpec((tm, LANES), lambda i, *_: (i, 0)),
                  pl.BlockSpec(memory_space=pl.ANY),
                  pl.BlockSpec((tm, d), lambda i, *_: (i, 0)),
                  per_b, per_b, per_b, pl.BlockSpec((1, d), lambda i, *_: (0, 0))],
        out_specs=pl.BlockSpec((tm, d), lambda i, *_: (i, 0)),
        scratch_shapes=[pltpu.VMEM((2, rows, d), BF16), pltpu.SemaphoreType.DMA((2,))],
    )
    return pl.pallas_call(
        _combine_kernel,
        grid_spec=grid_spec,
        out_shape=jax.ShapeDtypeStruct((t, d), F32),
        compiler_params=_cparams(("arbitrary",)),
        name="combine",
    )(meta["loff"], meta["goff"], meta["ngran"], meta["ntot"], posw, ys, h1, gt2, shf, scf, gfin)


def _rot_cols(w):
    half = ROPE // 2
    return jnp.concatenate([-w[..., half:], w[..., :half]], axis=-1)


def _pad_cols(w, lo, width):
    return jnp.pad(w, [(0, 0)] * (w.ndim - 1) + [(lo, width - lo - w.shape[-1])])


def _prep_weights(w_in, w_uq, w_uk, w_uv):
    o2, o3 = Q_LORA + KV_LORA, Q_LORA + KV_LORA + ROPE
    w_kr = w_in[:, o2:o3]
    wcat = jnp.concatenate([
        w_in[:, :o2],
        _pad_cols(w_kr, NOPE, HEAD_PAD), _pad_cols(_rot_cols(w_kr), NOPE, HEAD_PAD),
        w_in[:, o3:]], axis=1).astype(BF16)
    wq_a = _pad_cols(w_uq, 0, HEAD_PAD)
    wq_b = _pad_cols(_rot_cols(w_uq[..., NOPE:]), NOPE, HEAD_PAD)
    r = w_uq.shape[0]
    wqt = jnp.concatenate([wq_a.reshape(r, -1), wq_b.reshape(r, -1)], axis=1).T.astype(BF16)
    wk = _pad_cols(w_uk, 0, HEAD_PAD).reshape(r, -1).astype(BF16)
    wvt = _pad_cols(w_uv, 0, V_ROWS).reshape(r, -1).T.astype(BF16)
    ones = np.zeros((N_HEADS, V_ROWS), np.float32)
    ones[:, V_DIM] = 1.0
    return wcat, wqt, wk, wvt, jnp.asarray(ones.reshape(-1, 1))


def kernel(x, c, positions, w_ada, b_ada, g_norm_mix, w_in, g_q_lat, g_kv_lat, w_uq, w_uk, w_uv, w_branch_attn, w_branch_fourier, b_gates, w_out, g_norm_ffn, w_router_group, b_router_group, w_router_expert, b_router_expert, w_expert_gate, w_expert_up, w_expert_down, w_ada_final, b_ada_final, g_norm_final):
    bsz, s, d = x.shape
    assert w_ada.shape[0] == 1 and s % FFT_N2 == 0
    tm = min(512, s)

    mod = _mods(c, w_ada[0], b_ada[0])
    fmod = _mods(c, w_ada_final, b_ada_final)
    sh1, sc1, gt1, sh2, sc2, gt2 = [m.reshape(bsz, 1, d) for m in jnp.split(mod, 6, axis=-1)]
    shf, scf = [m.reshape(bsz, 1, d) for m in jnp.split(fmod, 2, axis=-1)]

    wcat, wqt, wk, wvt, vones = _prep_weights(w_in[0], w_uq[0], w_uk[0], w_uv[0])
    bd, m1, wb = _dft_constants(s)
    inv_freq = ROPE_THETA ** (-jnp.arange(0, ROPE, 2, dtype=F32) / ROPE)
    invf = _pad_cols(jnp.concatenate([inv_freq, inv_freq])[None, :], NOPE, HEAD_PAD).reshape(HEAD_PAD, 1)

    qt, k, vt, zr, zi, gates = _inproj(
        x, positions.reshape(bsz, 1, s), sh1, sc1, g_norm_mix[0][None, :], wcat,
        g_q_lat[0][None, :], g_kv_lat[0][None, :], wqt, wk, wvt, vones, bd, b_gates[0][None, :],
        invf, tm)

    attn = _attention(qt, k, vt, tq=min(256, s))
    br, bi = _fft_a(zr, zi, m1, ts2=8)
    four = _fft_b(br, bi, wb, tk1=16)

    wr = jnp.concatenate([w_router_expert[0].reshape(d, N_EXPERTS), w_router_group[0]], axis=1)
    wr = _pad_cols(wr, 0, LANES)
    wr_top = lax.bitcast_convert_type(
        lax.bitcast_convert_type(wr, jnp.uint32) & jnp.uint32(0xFFFF0000), F32)
    wr_hi = wr_top.astype(BF16)
    wr_lo = (wr - wr_top).astype(BF16)
    wr = jnp.concatenate([jnp.concatenate([wr_hi, wr_lo], axis=1),
                          jnp.concatenate([wr_hi, jnp.zeros_like(wr_lo)], axis=1)], axis=0)
    brt = _pad_cols(jnp.concatenate([b_router_expert[0].reshape(-1), b_router_group[0]])[None, :], 0, LANES)
    h1, v2, route, cnt = _merge(attn, four, gates, x, w_branch_attn[0].astype(BF16),
                                w_branch_fourier[0].astype(BF16), w_out[0].astype(BF16), gt1, sh2,
                                sc2, g_norm_ffn[0][None, :], wr, brt, tm)

    t = bsz * s
    meta, loffv, cap_tiles = _moe_metadata(cnt.reshape(-1, LANES), tm)
    rows = 2 * tm + N_EXPERTS * GRAN
    loffv = _pad_cols(loffv, 0, LANES).reshape(-1, 1, LANES)
    ltri = jnp.asarray(np.tril(np.ones((tm, tm), np.float32), -1), BF16)
    xs, posw = _dispatch(v2.reshape(t, d), route.reshape(t, LANES), meta, loffv, ltri, cap_tiles,
                         tm, rows)
    wgu = jnp.concatenate([w_expert_gate[0], w_expert_up[0]], axis=-1).astype(BF16)
    wd = w_expert_down[0].astype(BF16)
    ys = _experts(xs, wgu, wd, meta, cap_tiles)
    out = _combine(posw, ys, h1.reshape(t, d), gt2, shf, scf, g_norm_final[None, :], meta, tm, rows,
                   s // tm)
    return out.reshape(bsz, s, d)
```

```python
import functools

import numpy as np
import jax
import jax.numpy as jnp
from jax import lax
from jax.experimental import pallas as pl
from jax.experimental.pallas import tpu as pltpu

F32 = jnp.float32
BF16 = jnp.bfloat16

N_HEADS = 8
NOPE = 64
ROPE = 32
V_DIM = 64
HEAD_PAD = 128
V_ROWS = 128
Q_LORA = 256
KV_LORA = 256
ROPE_THETA = 10000.0
F_GROUPS = 8
F_GDIM = 64
F_WIDTH = F_GROUPS * F_GDIM
N_GROUPS = 4
E_PER_GROUP = 8
N_EXPERTS = N_GROUPS * E_PER_GROUP
D_EXPERT = 256
EPS = 1e-6
FFT_N2 = 64
LANES = 128
MERGE_ROWS = 128
ATTN_DEPTH = 1
QK_ROWS = 256
VMEM_LIMIT = 56 * 1024 * 1024


def _cparams(sem):
    return pltpu.CompilerParams(dimension_semantics=sem, vmem_limit_bytes=VMEM_LIMIT)


def _mods_kernel(c_ref, w_ref, b_ref, o_ref):
    c = c_ref[...]
    ca = c * jax.nn.sigmoid(c)
    o_ref[...] = jnp.dot(ca, w_ref[...], preferred_element_type=F32,
                         precision=lax.Precision.HIGHEST) + b_ref[...]


def _mods(c, w, b):
    bsz, d = c.shape
    n = w.shape[1]
    tn = 1024
    return pl.pallas_call(
        _mods_kernel,
        grid=(n // tn,),
        in_specs=[pl.BlockSpec((bsz, d), lambda j: (0, 0)),
                  pl.BlockSpec((d, tn), lambda j: (0, j)),
                  pl.BlockSpec((1, tn), lambda j: (0, j))],
        out_specs=pl.BlockSpec((bsz, tn), lambda j: (0, j)),
        out_shape=jax.ShapeDtypeStruct((bsz, n), F32),
        compiler_params=_cparams(("arbitrary",)),
        name="mods",
    )(c, w, b.reshape(1, n))


def _rms(x, g):
    r = lax.rsqrt(jnp.mean(x * x, axis=-1, keepdims=True) + EPS)
    return x * r * g


def _inproj_kernel(x_ref, pos_ref, sh_ref, sc_ref, gmix_ref, wcat_ref, gq_ref, gkv_ref,
                   wqt_ref, wk_ref, wvt_ref, vones_ref, bd_ref, bg_ref, invf_ref,
                   qt_ref, k_ref, vt_ref, zr_ref, zi_ref, gate_ref, *, scale):
    hp = N_HEADS * HEAD_PAD
    nt_dims = (((1,), (1,)), ((), ()))
    u = _rms(x_ref[0], gmix_ref[...]) * (1.0 + sc_ref[0]) + sh_ref[0]
    ub = u.astype(BF16)

    o_q, o_kv, o_kr, o_f, o_g = 0, Q_LORA, Q_LORA + KV_LORA, Q_LORA + KV_LORA + 2 * HEAD_PAD, \
        Q_LORA + KV_LORA + 2 * HEAD_PAD + F_WIDTH

    tm = pos_ref.shape[-1]
    ang = invf_ref[...] * pos_ref[0].astype(F32)
    tail = jnp.zeros((HEAD_PAD - NOPE - ROPE, tm), F32)
    cos_t = jnp.concatenate([jnp.ones((NOPE, tm), F32), jnp.cos(ang), tail], axis=0)
    sin_t = jnp.concatenate([jnp.zeros((NOPE, tm), F32), jnp.sin(ang), tail], axis=0)

    ql = jnp.dot(ub, wcat_ref[:, o_q:o_kv], preferred_element_type=F32)
    kvl = jnp.dot(ub, wcat_ref[:, o_kv:o_kr], preferred_element_type=F32)
    kr = jnp.dot(ub, wcat_ref[:, o_kr:o_f], preferred_element_type=F32)
    fin = jnp.dot(ub, wcat_ref[:, o_f:o_g], preferred_element_type=F32).astype(BF16)
    zg = jnp.dot(ub, wcat_ref[:, o_g:], preferred_element_type=F32) + bg_ref[...]

    qn = _rms(ql, gq_ref[...]).astype(BF16)
    kvn = _rms(kvl, gkv_ref[...]).astype(BF16)
    zz = jnp.dot(fin, bd_ref[...], preferred_element_type=F32)
    qab_t = lax.dot_general(wqt_ref[...], qn, nt_dims, preferred_element_type=F32)
    kn = jnp.dot(kvn, wk_ref[...], preferred_element_type=F32)
    vt = lax.dot_general(wvt_ref[...], kvn, nt_dims, preferred_element_type=F32) + vones_ref[...]

    gate_ref[0] = jax.nn.sigmoid(zg).astype(BF16)
    zr_ref[0] = zz[:, :F_WIDTH].astype(BF16)
    zi_ref[0] = zz[:, F_WIDTH:].astype(BF16)
    cq = cos_t * scale
    sq = sin_t * scale
    for h in range(N_HEADS):
        a = qab_t[h * HEAD_PAD:(h + 1) * HEAD_PAD]
        b = qab_t[hp + h * HEAD_PAD:hp + (h + 1) * HEAD_PAD]
        qt_ref[0, h] = (a * cq + b * sq).astype(BF16)
    krr = kr[:, :HEAD_PAD] * cos_t.T + kr[:, HEAD_PAD:] * sin_t.T
    for h in range(N_HEADS):
        k_ref[0, h] = (kn[:, h * HEAD_PAD:(h + 1) * HEAD_PAD] + krr).astype(BF16)
        vt_ref[0, h, 0] = vt[h * V_ROWS:(h + 1) * V_ROWS].astype(BF16)


def _inproj(x, pos_row, sh1, sc1, gmix, wcat, gq, gkv, wqt, wk, wvt, vones, bd, bg, invf, tm):
    bsz, s, d = x.shape
    hp = N_HEADS * HEAD_PAD
    nt = s // tm
    const = lambda shape: pl.BlockSpec(shape, lambda b, i: (0,) * len(shape))
    tok = lambda w: pl.BlockSpec((1, tm, w), lambda b, i: (b, i, 0))
    per_b = pl.BlockSpec((1, 1, d), lambda b, i: (b, 0, 0))
    scale = float((NOPE + ROPE) ** -0.5 * np.log2(np.e))
    return pl.pallas_call(
        functools.partial(_inproj_kernel, scale=scale),
        grid=(bsz, nt),
        in_specs=[tok(d), pl.BlockSpec((1, 1, tm), lambda b, i: (b, 0, i)), per_b, per_b,
                  const((1, d)), const(wcat.shape), const((1, Q_LORA)), const((1, KV_LORA)),
                  const(wqt.shape), const(wk.shape), const(wvt.shape), const(vones.shape),
                  const(bd.shape), const((1, bg.shape[1])), const((ROPE, 1))],
        out_specs=[pl.BlockSpec((1, N_HEADS, HEAD_PAD, tm), lambda b, i: (b, 0, 0, i)),
                   pl.BlockSpec((1, N_HEADS, tm, HEAD_PAD), lambda b, i: (b, 0, i, 0)),
                   pl.BlockSpec((1, N_HEADS, 1, V_ROWS, tm), lambda b, i: (b, 0, i, 0, 0)),
                   tok(F_WIDTH), tok(F_WIDTH), tok(2 * d)],
        out_shape=[jax.ShapeDtypeStruct((bsz, N_HEADS, HEAD_PAD, s), BF16),
                   jax.ShapeDtypeStruct((bsz, N_HEADS, s, HEAD_PAD), BF16),
                   jax.ShapeDtypeStruct((bsz, N_HEADS, nt, V_ROWS, tm), BF16),
                   jax.ShapeDtypeStruct((bsz, s, F_WIDTH), BF16),
                   jax.ShapeDtypeStruct((bsz, s, F_WIDTH), BF16),
                   jax.ShapeDtypeStruct((bsz, s, 2 * d), BF16)],
        compiler_params=_cparams(("parallel", "parallel")),
        name="inproj",
    )(x, pos_row, sh1, sc1, gmix, wcat, gq, gkv, wqt, wk, wvt, vones, bd, bg, invf)


def _attn_kernel(qt_ref, k_ref, vt_ref, o_ref, *p_refs, nk):
    tq = qt_ref.shape[3]
    tk = vt_ref.shape[4]
    qts = [qt_ref[0, hh] for hh in range(2)]
    nbuf = ATTN_DEPTH + 1

    def scores(c, hh):
        ss = []
        for i in range(tk // QK_ROWS):
            off = c * tk + i * QK_ROWS
            ss.append(jnp.dot(k_ref[0, hh, off:off + QK_ROWS, :], qts[hh],
                              preferred_element_type=F32))
        return ss

    def probs(ss, m, p_ref):
        m_new = m
        for s in ss:
            m_new = jnp.maximum(m_new, jnp.max(s, axis=0, keepdims=True))
        for i, s in enumerate(ss):
            p_ref[i * QK_ROWS:(i + 1) * QK_ROWS, :] = jnp.exp2(s - m_new).astype(BF16)
        return m_new, jnp.exp2(m - m_new)

    def accumulate(c, hh, acc, alpha, p_ref):
        return alpha * acc + jnp.dot(vt_ref[0, hh, c], p_ref[...], preferred_element_type=F32)

    hs = range(2)
    buf = lambda c, hh: p_refs[hh * nbuf + c % nbuf]
    m = [jnp.full((1, tq), -jnp.inf, F32) for _ in hs]
    alpha = [[None] * nk for _ in hs]
    accs = [jnp.zeros((V_ROWS, tq), F32) for _ in hs]
    for c in range(nk + ATTN_DEPTH):
        if c < nk:
            s = [scores(c, hh) for hh in hs]
        if c >= ATTN_DEPTH:
            for hh in hs:
                accs[hh] = accumulate(c - ATTN_DEPTH, hh, accs[hh], alpha[hh][c - ATTN_DEPTH],
                                      buf(c - ATTN_DEPTH, hh))
        if c < nk:
            for hh in hs:
                m[hh], alpha[hh][c] = probs(s[hh], m[hh], buf(c, hh))
    out_t = jnp.concatenate([a[:V_DIM] / a[V_DIM:V_DIM + 1] for a in accs], axis=0)
    o_ref[0] = out_t.T.astype(BF16)


def _attention(qt, k, vt, tq):
    bsz, nh, s, _ = k.shape
    nk, tk = vt.shape[2], vt.shape[4]
    return pl.pallas_call(
        functools.partial(_attn_kernel, nk=nk),
        grid=(bsz, nh // 2, s // tq),
        in_specs=[pl.BlockSpec((1, 2, HEAD_PAD, tq), lambda b, j, i: (b, j, 0, i)),
                  pl.BlockSpec((1, 2, s, HEAD_PAD), lambda b, j, i: (b, j, 0, 0)),
                  pl.BlockSpec((1, 2, nk, V_ROWS, tk), lambda b, j, i: (b, j, 0, 0, 0))],
        out_specs=pl.BlockSpec((1, tq, HEAD_PAD), lambda b, j, i: (b, i, j)),
        out_shape=jax.ShapeDtypeStruct((bsz, s, nh * V_DIM), BF16),
        scratch_shapes=[pltpu.VMEM((tk, tq), BF16)] * (2 * (ATTN_DEPTH + 1)),
        compiler_params=_cparams(("parallel", "parallel", "arbitrary")),
        name="attention",
    )(qt, k, vt)


def _fft_a_kernel(zr_ref, zi_ref, m1_ref, br_ref, bi_ref):
    n1 = zr_ref.shape[1]
    ts2 = m1_ref.shape[0]
    for j in range(ts2):
        sl = slice(j * F_WIDTH, (j + 1) * F_WIDTH)
        xx = jnp.concatenate([zr_ref[0, :, sl], zi_ref[0, :, sl]], axis=0)
        y = jnp.dot(m1_ref[j], xx, preferred_element_type=F32)
        br_ref[0, :, sl] = y[:n1].astype(BF16)
        bi_ref[0, :, sl] = y[n1:].astype(BF16)


def _fft_a(zr, zi, m1, ts2):
    bsz, s, _ = zr.shape
    n1 = s // FFT_N2
    zr2 = zr.reshape(bsz, n1, FFT_N2 * F_WIDTH)
    zi2 = zi.reshape(bsz, n1, FFT_N2 * F_WIDTH)
    blk = pl.BlockSpec((1, n1, ts2 * F_WIDTH), lambda b, j: (b, 0, j))
    return pl.pallas_call(
        _fft_a_kernel,
        grid=(bsz, FFT_N2 // ts2),
        in_specs=[blk, blk, pl.BlockSpec((ts2, 2 * n1, 2 * n1), lambda b, j: (j, 0, 0))],
        out_specs=[blk, blk],
        out_shape=[jax.ShapeDtypeStruct(zr2.shape, BF16)] * 2,
        compiler_params=_cparams(("parallel", "parallel")),
        name="fft_a",
    )(zr2, zi2, m1)


def _fft_b_kernel(br_ref, bi_ref, wb_ref, o_ref):
    tk1 = br_ref.shape[1]
    for j in range(tk1):
        xx = jnp.concatenate([br_ref[0, j], bi_ref[0, j]], axis=0)
        y = jnp.dot(wb_ref[...], xx, preferred_element_type=F32)
        o_ref[0, :, j * F_WIDTH:(j + 1) * F_WIDTH] = y.astype(BF16)


def _fft_b(br, bi, wb, tk1):
    bsz, n1, _ = br.shape
    br4 = br.reshape(bsz, n1, FFT_N2, F_WIDTH)
    bi4 = bi.reshape(bsz, n1, FFT_N2, F_WIDTH)
    blk = pl.BlockSpec((1, tk1, FFT_N2, F_WIDTH), lambda b, j: (b, j, 0, 0))
    out = pl.pallas_call(
        _fft_b_kernel,
        grid=(bsz, n1 // tk1),
        in_specs=[blk, blk, pl.BlockSpec((FFT_N2, 2 * FFT_N2), lambda b, j: (0, 0))],
        out_specs=pl.BlockSpec((1, FFT_N2, tk1 * F_WIDTH), lambda b, j: (b, 0, j)),
        out_shape=jax.ShapeDtypeStruct((bsz, FFT_N2, n1 * F_WIDTH), BF16),
        compiler_params=_cparams(("parallel", "parallel")),
        name="fft_b",
    )(br4, bi4, wb)
    return out.reshape(bsz, n1 * FFT_N2, F_WIDTH)


def _dft_constants(s):
    n1, n2 = s // FFT_N2, FFT_N2
    m = np.arange(F_GDIM)
    th = 2.0 * np.pi * np.outer(m, m) / F_GDIM
    eye = np.eye(F_GROUPS)
    bd = np.concatenate([np.kron(eye, np.cos(th)), -np.kron(eye, np.sin(th))], axis=1) / np.sqrt(F_GDIM)
    k1 = np.arange(n1)[None, :, None]
    s1 = np.arange(n1)[None, None, :]
    s2 = np.arange(n2)[:, None, None]
    ph = 2.0 * np.pi * ((k1 * (n2 * s1 + s2)) % s) / s
    cr, ci = np.cos(ph) / np.sqrt(n1), -np.sin(ph) / np.sqrt(n1)
    m1 = np.concatenate([np.concatenate([cr, -ci], axis=2), np.concatenate([ci, cr], axis=2)], axis=1)
    k2 = np.arange(n2)
    tb = 2.0 * np.pi * np.outer(k2, k2) / n2
    wb = np.concatenate([np.cos(tb), np.sin(tb)], axis=1) / np.sqrt(n2)
    return (jnp.asarray(bd, BF16), jnp.asarray(m1, BF16), jnp.asarray(wb, BF16))


def _route(parts, bias):
    logit = parts[:, :LANES] + parts[:, LANES:] + bias
    lane = lax.broadcasted_iota(jnp.int32, logit.shape, 1)
    neg = -jnp.inf
    big = jnp.int32(LANES)

    def softmax_masked(mask):
        z = jnp.where(mask, logit, neg)
        e = jnp.exp(z - jnp.max(z, axis=-1, keepdims=True))
        return e / jnp.sum(e, axis=-1, keepdims=True)

    def top1(p, mask):
        pm = jnp.where(mask, p, -1.0)
        best = jnp.max(pm, axis=-1, keepdims=True)
        idx = jnp.min(jnp.where(pm == best, lane, big), axis=-1, keepdims=True)
        return best, idx

    gmask = (lane >= N_EXPERTS) & (lane < N_EXPERTS + N_GROUPS)
    g_prob = softmax_masked(gmask)
    g_p, g_lane = top1(g_prob, gmask)
    e_lo = (g_lane - N_EXPERTS) * E_PER_GROUP
    emask = (lane >= e_lo) & (lane < e_lo + E_PER_GROUP)
    e_prob = softmax_masked(emask)
    p1, i1 = top1(e_prob, emask)
    p2, i2 = top1(e_prob, emask & (lane != i1))
    den = p1 + p2
    w1 = g_p * (p1 / den)
    w2 = g_p * (p2 / den)
    route = jnp.where(lane == 0, i1.astype(F32), jnp.where(lane == 1, i2.astype(F32),
                      jnp.where(lane == 2, w1, jnp.where(lane == 3, w2, 0.0))))
    cnt = jnp.sum(((lane == i1) | (lane == i2)).astype(F32), axis=0, keepdims=True)
    return route, cnt


def _merge_kernel(at_ref, fo_ref, gate_ref, x_ref, wba_ref, wbf_ref, wout_ref, gt1_ref, sh2_ref,
                  sc2_ref, gffn_ref, wr_ref, br_ref, h1_ref, v2_ref, route_ref, cnt_ref):
    d = x_ref.shape[-1]
    tm = x_ref.shape[1]
    blocks = [slice(i * MERGE_ROWS, (i + 1) * MERGE_ROWS) for i in range(tm // MERGE_ROWS)]
    a = [jnp.dot(at_ref[0, r, :], wba_ref[...], preferred_element_type=F32) for r in blocks]
    fo = [jnp.dot(fo_ref[0, r, :], wbf_ref[...], preferred_element_type=F32) for r in blocks]
    y = []
    for r, ai, fi in zip(blocks, a, fo):
        merged = gate_ref[0, r, :d].astype(F32) * ai + gate_ref[0, r, d:].astype(F32) * fi
        y.append(jnp.dot(merged.astype(BF16), wout_ref[...], preferred_element_type=F32))
    parts = []
    for r, yi in zip(blocks, y):
        h1 = x_ref[0, r, :] + gt1_ref[0] * yi
        h1_ref[0, r, :] = h1
        v2 = _rms(h1, gffn_ref[...]) * (1.0 + sc2_ref[0]) + sh2_ref[0]
        v_hi = v2.astype(BF16)
        v2_ref[0, r, :] = v_hi
        v_lo = (v2 - v_hi.astype(F32)).astype(BF16)
        parts.append(jnp.dot(jnp.concatenate([v_hi, v_lo], axis=1), wr_ref[...],
                             preferred_element_type=F32))
    cnt = None
    for r, pi in zip(blocks, parts):
        route, c = _route(pi, br_ref[...])
        route_ref[0, r, :] = route
        cnt = c if cnt is None else cnt + c
    cnt_ref[0, 0] = cnt


def _merge(attn, four, gates, x, wba, wbf, wout, gt1, sh2, sc2, gffn, wr, br, tm):
    bsz, s, d = x.shape
    const = lambda shape: pl.BlockSpec(shape, lambda b, i: (0,) * len(shape))
    tok = lambda w: pl.BlockSpec((1, tm, w), lambda b, i: (b, i, 0))
    per_b = pl.BlockSpec((1, 1, d), lambda b, i: (b, 0, 0))
    return pl.pallas_call(
        _merge_kernel,
        grid=(bsz, s // tm),
        in_specs=[tok(attn.shape[-1]), tok(F_WIDTH), tok(2 * d), tok(d), const(wba.shape),
                  const(wbf.shape), const(wout.shape), per_b, per_b, per_b, const((1, d)),
                  const(wr.shape), const((1, LANES))],
        out_specs=[tok(d), tok(d), tok(LANES),
                   pl.BlockSpec((1, 1, 1, LANES), lambda b, i: (b, i, 0, 0))],
        out_shape=[jax.ShapeDtypeStruct((bsz, s, d), F32), jax.ShapeDtypeStruct((bsz, s, d), BF16),
                   jax.ShapeDtypeStruct((bsz, s, LANES), F32),
                   jax.ShapeDtypeStruct((bsz, s // tm, 1, LANES), F32)],
        compiler_params=_cparams(("parallel", "parallel")),
        name="merge",
    )(attn, four, gates, x, wba, wbf, wout, gt1, sh2, sc2, gffn, wr, br)


GRAN = 16
MOE_ROW_TILE = 1024


def _moe_metadata(cnt, tm):
    ntile = cnt.shape[0]

    def cumsum(a, axis):
        n = a.shape[axis]
        tri = jnp.asarray(np.triu(np.ones((n, n), np.float32)))
        af = a.astype(F32)
        out = (jnp.dot(af, tri, precision=lax.Precision.HIGHEST) if axis == a.ndim - 1
               else jnp.dot(tri.T, af, precision=lax.Precision.HIGHEST))
        return out.astype(jnp.int32)

    c = cnt[:, :N_EXPERTS].astype(jnp.int32)
    pc = (c + GRAN - 1) // GRAN * GRAN
    loff = cumsum(pc, 1) - pc
    tot = jnp.sum(pc, axis=0)
    tot_r = (tot + MOE_ROW_TILE - 1) // MOE_ROW_TILE * MOE_ROW_TILE
    ends = cumsum(tot_r[None, :], 1)[0]
    base = ends - tot_r
    goff = base[None, :] + cumsum(pc, 0) - pc
    cap_tiles = (2 * ntile * tm + ntile * N_EXPERTS * GRAN) // MOE_ROW_TILE + N_EXPERTS
    n_valid = (ends[-1] // MOE_ROW_TILE).astype(jnp.int32)
    tile_start = jnp.arange(cap_tiles, dtype=jnp.int32) * MOE_ROW_TILE
    tile_expert = jnp.minimum(jnp.sum((ends[None, :] <= tile_start[:, None]).astype(jnp.int32), axis=1),
                              N_EXPERTS - 1)
    flat = lambda a: a.reshape(-1).astype(jnp.int32)
    tail = tot_r - tot + jnp.where(jnp.arange(N_EXPERTS) == N_EXPERTS - 1,
                                   cap_tiles * MOE_ROW_TILE - ends[-1], 0)
    meta = dict(loff=flat(loff), goff=flat(goff), ngran=flat(pc // GRAN),
                ntot=flat(jnp.sum(pc, axis=1) // GRAN),
                tail_start=flat(base + tot), tail_n=flat(tail // GRAN),
                tile_expert=tile_expert, n_valid=n_valid.reshape(1))
    return meta, loff.astype(F32), cap_tiles


def _run_copies(i, loff_s, goff_s, ngran_s, make_copy):
    def per_expert(e, n_started):
        n = ngran_s[i * N_EXPERTS + e]
        lo = loff_s[i * N_EXPERTS + e]
        go = goff_s[i * N_EXPERTS + e]

        def per_granule(g, carry):
            make_copy(pl.multiple_of(lo + g * GRAN, GRAN), pl.multiple_of(go + g * GRAN, GRAN)).start()
            return carry

        lax.fori_loop(0, n, per_granule, 0)
        return n_started + n

    return lax.fori_loop(0, N_EXPERTS, per_expert, 0)


def _wait_copies(n, make_copy):
    def one(_, carry):
        make_copy(0, 0).wait()
        return carry
    lax.fori_loop(0, n, one, 0)


def _local_positions(route, loffv, ltri):
    lane = lax.broadcasted_iota(jnp.int32, route.shape, 1).astype(F32)
    o1 = lane == route[:, 0:1]
    o2 = lane == route[:, 1:2]
    before = jnp.dot(ltri, (o1 | o2).astype(BF16), preferred_element_type=F32)
    start = before + loffv
    pos1 = jnp.sum(jnp.where(o1, start, 0.0), axis=-1, keepdims=True)
    pos2 = jnp.sum(jnp.where(o2, start, 0.0), axis=-1, keepdims=True)
    return pos1, pos2


def _dispatch_kernel(loff_s, goff_s, ngran_s, ntot_s, tstart_s, tn_s, v2_ref, route_ref, loffv_ref,
                     ltri_ref, xs_ref, posw_ref, loc_ref, zero_ref, sem):
    i = pl.program_id(0)
    last = pl.num_programs(0) - 1
    slot = i % 2
    rows = loc_ref.shape[1]
    route = route_ref[...]
    pos1, pos2 = _local_positions(route, loffv_ref[0], ltri_ref[...])
    lane = lax.broadcasted_iota(jnp.int32, route.shape, 1)
    posw = jnp.where(lane == 0, pos1, jnp.where(lane == 1, pos2, route))
    posw_ref[...] = posw
    pos_t = posw.T
    riota = lax.broadcasted_iota(jnp.int32, (rows, 1), 0).astype(F32)
    onehot = ((riota == pos_t[0:1]) | (riota == pos_t[1:2])).astype(BF16)
    loc_ref[slot] = jnp.dot(onehot, v2_ref[...], preferred_element_type=F32).astype(BF16)

    def out_copy(s):
        return lambda lo, go: pltpu.make_async_copy(
            loc_ref.at[s, pl.ds(lo, GRAN)], xs_ref.at[pl.ds(go, GRAN)], sem.at[s])

    _run_copies(i, loff_s, goff_s, ngran_s, out_copy(slot))

    @pl.when(i > 0)
    def _():
        _wait_copies(ntot_s[i - 1], out_copy(1 - slot))

    @pl.when(i == last)
    def _():
        _wait_copies(ntot_s[i], out_copy(slot))
        zero_ref[...] = jnp.zeros(zero_ref.shape, BF16)

        def zero_copy(go):
            return pltpu.make_async_copy(zero_ref, xs_ref.at[pl.ds(go, GRAN)], sem.at[0])

        def per_expert(e, n_started):
            def per_granule(g, carry):
                zero_copy(pl.multiple_of(tstart_s[e] + g * GRAN, GRAN)).start()
                return carry
            lax.fori_loop(0, tn_s[e], per_granule, 0)
            return n_started + tn_s[e]

        nz = lax.fori_loop(0, N_EXPERTS, per_expert, 0)
        _wait_copies(nz, lambda lo, go: zero_copy(go))


def _dispatch(v2, route, meta, loffv, ltri, cap_tiles, tm, rows):
    t, d = v2.shape
    ntile = t // tm
    grid_spec = pltpu.PrefetchScalarGridSpec(
        num_scalar_prefetch=6,
        grid=(ntile,),
        in_specs=[pl.BlockSpec((tm, d), lambda i, *_: (i, 0)),
                  pl.BlockSpec((tm, LANES), lambda i, *_: (i, 0)),
                  pl.BlockSpec((1, 1, LANES), lambda i, *_: (i, 0, 0)),
                  pl.BlockSpec((tm, tm), lambda i, *_: (0, 0))],
        out_specs=[pl.BlockSpec(memory_space=pl.ANY),
                   pl.BlockSpec((tm, LANES), lambda i, *_: (i, 0))],
        scratch_shapes=[pltpu.VMEM((2, rows, d), BF16), pltpu.VMEM((GRAN, d), BF16),
                        pltpu.SemaphoreType.DMA((2,))],
    )
    return pl.pallas_call(
        _dispatch_kernel,
        grid_spec=grid_spec,
        out_shape=[jax.ShapeDtypeStruct((cap_tiles * MOE_ROW_TILE, d), BF16),
                   jax.ShapeDtypeStruct((t, LANES), F32)],
        compiler_params=_cparams(("arbitrary",)),
        name="dispatch",
    )(meta["loff"], meta["goff"], meta["ngran"], meta["ntot"], meta["tail_start"], meta["tail_n"],
      v2, route, loffv, ltri)


def _experts_kernel(texp_s, nvalid_s, x_ref, wgu_ref, wd_ref, y_ref):
    del texp_s

    @pl.when(pl.program_id(0) < nvalid_s[0])
    def _():
        hc = jnp.dot(x_ref[...], wgu_ref[0], preferred_element_type=F32)
        g = hc[:, :D_EXPERT]
        hid = (g * jax.nn.sigmoid(g) * hc[:, D_EXPERT:]).astype(BF16)
        y_ref[...] = jnp.dot(hid, wd_ref[0], preferred_element_type=F32).astype(BF16)

    @pl.when(pl.program_id(0) >= nvalid_s[0])
    def _():
        y_ref[...] = jnp.zeros(y_ref.shape, BF16)


def _experts(xs, wgu, wd, meta, cap_tiles):
    d = xs.shape[1]
    row = lambda j, texp, nv: (jnp.minimum(j, nv[0] - 1), 0)
    wsel = lambda j, texp, nv: (texp[jnp.minimum(j, nv[0] - 1)], 0, 0)
    grid_spec = pltpu.PrefetchScalarGridSpec(
        num_scalar_prefetch=2,
        grid=(cap_tiles,),
        in_specs=[pl.BlockSpec((MOE_ROW_TILE, d), row),
                  pl.BlockSpec((1, d, 2 * D_EXPERT), wsel),
                  pl.BlockSpec((1, D_EXPERT, d), wsel)],
        out_specs=pl.BlockSpec((MOE_ROW_TILE, d), lambda j, texp, nv: (j, 0)),
    )
    return pl.pallas_call(
        _experts_kernel,
        grid_spec=grid_spec,
        out_shape=jax.ShapeDtypeStruct(xs.shape, BF16),
        compiler_params=_cparams(("arbitrary",)),
        name="experts",
    )(meta["tile_expert"], meta["n_valid"], xs, wgu, wd)


def _combine_kernel(loff_s, goff_s, ngran_s, ntot_s, posw_ref, ys_ref, h1_ref, gt2_ref, shf_ref,
                    scf_ref, gfin_ref, o_ref, loc_ref, sem):
    i = pl.program_id(0)
    slot = i % 2
    rows = loc_ref.shape[1]

    def in_copy(s):
        return lambda lo, go: pltpu.make_async_copy(
            ys_ref.at[pl.ds(go, GRAN)], loc_ref.at[s, pl.ds(lo, GRAN)], sem.at[s])

    def fetch(tile, s):
        loc_ref[s] = jnp.zeros(loc_ref.shape[1:], BF16)
        _run_copies(tile, loff_s, goff_s, ngran_s, in_copy(s))

    @pl.when(i == 0)
    def _():
        fetch(0, 0)

    @pl.when(i + 1 < pl.num_programs(0))
    def _():
        fetch(i + 1, 1 - slot)

    _wait_copies(ntot_s[i], in_copy(slot))

    posw = posw_ref[...]
    riota = lax.broadcasted_iota(jnp.int32, (1, rows), 1).astype(F32)
    qw = (jnp.where(riota == posw[:, 0:1], posw[:, 2:3], 0.0)
          + jnp.where(riota == posw[:, 1:2], posw[:, 3:4], 0.0)).astype(BF16)
    moe = jnp.dot(qw, loc_ref[slot], preferred_element_type=F32)
    h2 = h1_ref[...] + gt2_ref[0] * moe
    o_ref[...] = _rms(h2, gfin_ref[...]) * (1.0 + scf_ref[0]) + shf_ref[0]


def _combine(posw, ys, h1, gt2, shf, scf, gfin, meta, tm, rows, tiles_per_batch):
    t, d = h1.shape
    per_b = pl.BlockSpec((1, 1, d), lambda i, *_: (i // tiles_per_batch, 0, 0))
    grid_spec = pltpu.PrefetchScalarGridSpec(
        num_scalar_prefetch=4,
        grid=(t // tm,),
        in_specs=[pl.BlockSpec((tm, LANES), lambda i, *_: (i, 0)),
                  pl.BlockSpec(memory_space=pl.ANY),
                  pl.BlockSpec((tm, d), lambda i, *_: (i, 0)),
                  per_b, per_b, per_b, pl.BlockSpec((1, d), lambda i, *_: (0, 0))],
        out_specs=pl.BlockSpec((tm, d), lambda i, *_: (i, 0)),
        scratch_shapes=[pltpu.VMEM((2, rows, d), BF16), pltpu.SemaphoreType.DMA((2,))],
    )
    return pl.pallas_call(
        _combine_kernel,
        grid_spec=grid_spec,
        out_shape=jax.ShapeDtypeStruct((t, d), F32),
        compiler_params=_cparams(("arbitrary",)),
        name="combine",
    )(meta["loff"], meta["goff"], meta["ngran"], meta["ntot"], posw, ys, h1, gt2, shf, scf, gfin)


def _rot_cols(w):
    half = ROPE // 2
    return jnp.concatenate([-w[..., half:], w[..., :half]], axis=-1)


def _pad_cols(w, lo, width):
    return jnp.pad(w, [(0, 0)] * (w.ndim - 1) + [(lo, width - lo - w.shape[-1])])


def _prep_weights(w_in, w_uq, w_uk, w_uv):
    o2, o3 = Q_LORA + KV_LORA, Q_LORA + KV_LORA + ROPE
    w_kr = w_in[:, o2:o3]
    wcat = jnp.concatenate([
        w_in[:, :o2],
        _pad_cols(w_kr, NOPE, HEAD_PAD), _pad_cols(_rot_cols(w_kr), NOPE, HEAD_PAD),
        w_in[:, o3:]], axis=1).astype(BF16)
    wq_a = _pad_cols(w_uq, 0, HEAD_PAD)
    wq_b = _pad_cols(_rot_cols(w_uq[..., NOPE:]), NOPE, HEAD_PAD)
    r = w_uq.shape[0]
    wqt = jnp.concatenate([wq_a.reshape(r, -1), wq_b.reshape(r, -1)], axis=1).T.astype(BF16)
    wk = _pad_cols(w_uk, 0, HEAD_PAD).reshape(r, -1).astype(BF16)
    wvt = _pad_cols(w_uv, 0, V_ROWS).reshape(r, -1).T.astype(BF16)
    ones = np.zeros((N_HEADS, V_ROWS), np.float32)
    ones[:, V_DIM] = 1.0
    return wcat, wqt, wk, wvt, jnp.asarray(ones.reshape(-1, 1))


def kernel(x, c, positions, w_ada, b_ada, g_norm_mix, w_in, g_q_lat, g_kv_lat, w_uq, w_uk, w_uv, w_branch_attn, w_branch_fourier, b_gates, w_out, g_norm_ffn, w_router_group, b_router_group, w_router_expert, b_router_expert, w_expert_gate, w_expert_up, w_expert_down, w_ada_final, b_ada_final, g_norm_final):
    bsz, s, d = x.shape
    assert w_ada.shape[0] == 1 and s % FFT_N2 == 0
    tm = min(512, s)

    mod = _mods(c, w_ada[0], b_ada[0])
    fmod = _mods(c, w_ada_final, b_ada_final)
    sh1, sc1, gt1, sh2, sc2, gt2 = [m.reshape(bsz, 1, d) for m in jnp.split(mod, 6, axis=-1)]
    shf, scf = [m.reshape(bsz, 1, d) for m in jnp.split(fmod, 2, axis=-1)]

    wcat, wqt, wk, wvt, vones = _prep_weights(w_in[0], w_uq[0], w_uk[0], w_uv[0])
    bd, m1, wb = _dft_constants(s)
    inv_freq = ROPE_THETA ** (-jnp.arange(0, ROPE, 2, dtype=F32) / ROPE)
    invf = jnp.concatenate([inv_freq, inv_freq]).reshape(ROPE, 1)

    qt, k, vt, zr, zi, gates = _inproj(
        x, positions.reshape(bsz, 1, s), sh1, sc1, g_norm_mix[0][None, :], wcat,
        g_q_lat[0][None, :], g_kv_lat[0][None, :], wqt, wk, wvt, vones, bd, b_gates[0][None, :],
        invf, tm)

    attn = _attention(qt, k, vt, tq=min(256, s))
    br, bi = _fft_a(zr, zi, m1, ts2=8)
    four = _fft_b(br, bi, wb, tk1=16)

    wr = jnp.concatenate([w_router_expert[0].reshape(d, N_EXPERTS), w_router_group[0]], axis=1)
    wr = _pad_cols(wr, 0, LANES)
    wr_top = lax.bitcast_convert_type(
        lax.bitcast_convert_type(wr, jnp.uint32) & jnp.uint32(0xFFFF0000), F32)
    wr_hi = wr_top.astype(BF16)
    wr_lo = (wr - wr_top).astype(BF16)
    wr = jnp.concatenate([jnp.concatenate([wr_hi, wr_lo], axis=1),
                          jnp.concatenate([wr_hi, jnp.zeros_like(wr_lo)], axis=1)], axis=0)
    brt = _pad_cols(jnp.concatenate([b_router_expert[0].reshape(-1), b_router_group[0]])[None, :], 0, LANES)
    h1, v2, route, cnt = _merge(attn, four, gates, x, w_branch_attn[0].astype(BF16),
                                w_branch_fourier[0].astype(BF16), w_out[0].astype(BF16), gt1, sh2,
                                sc2, g_norm_ffn[0][None, :], wr, brt, tm)

    t = bsz * s
    meta, loffv, cap_tiles = _moe_metadata(cnt.reshape(-1, LANES), tm)
    rows = 2 * tm + N_EXPERTS * GRAN
    loffv = _pad_cols(loffv, 0, LANES).reshape(-1, 1, LANES)
    ltri = jnp.asarray(np.tril(np.ones((tm, tm), np.float32), -1), BF16)
    xs, posw = _dispatch(v2.reshape(t, d), route.reshape(t, LANES), meta, loffv, ltri, cap_tiles,
                         tm, rows)
    wgu = jnp.concatenate([w_expert_gate[0], w_expert_up[0]], axis=-1).astype(BF16)
    wd = w_expert_down[0].astype(BF16)
    ys = _experts(xs, wgu, wd, meta, cap_tiles)
    out = _combine(posw, ys, h1.reshape(t, d), gt2, shf, scf, g_norm_final[None, :], meta, tm, rows,
                   s // tm)
    return out.reshape(bsz, s, d)
```

```python
import functools

import numpy as np
import jax
import jax.numpy as jnp
from jax import lax
from jax.experimental import pallas as pl
from jax.experimental.pallas import tpu as pltpu

F32 = jnp.float32
BF16 = jnp.bfloat16

N_HEADS = 8
NOPE = 64
ROPE = 32
V_DIM = 64
HEAD_PAD = 128
V_ROWS = 96
Q_LORA = 256
KV_LORA = 256
ROPE_THETA = 10000.0
F_GROUPS = 8
F_GDIM = 64
F_WIDTH = F_GROUPS * F_GDIM
N_GROUPS = 4
E_PER_GROUP = 8
N_EXPERTS = N_GROUPS * E_PER_GROUP
D_EXPERT = 256
EPS = 1e-6
FFT_N2 = 64
LANES = 128
MERGE_ROWS = 128
ATTN_DEPTH = 1
QK_ROWS = 256
VMEM_LIMIT = 56 * 1024 * 1024


def _cparams(sem):
    return pltpu.CompilerParams(dimension_semantics=sem, vmem_limit_bytes=VMEM_LIMIT)


def _mods_kernel(c_ref, w_ref, b_ref, o_ref):
    c = c_ref[...]
    ca = c * jax.nn.sigmoid(c)
    o_ref[...] = jnp.dot(ca, w_ref[...], preferred_element_type=F32,
                         precision=lax.Precision.HIGHEST) + b_ref[...]


def _mods(c, w, b):
    bsz, d = c.shape
    n = w.shape[1]
    tn = 1024
    return pl.pallas_call(
        _mods_kernel,
        grid=(n // tn,),
        in_specs=[pl.BlockSpec((bsz, d), lambda j: (0, 0)),
                  pl.BlockSpec((d, tn), lambda j: (0, j)),
                  pl.BlockSpec((1, tn), lambda j: (0, j))],
        out_specs=pl.BlockSpec((bsz, tn), lambda j: (0, j)),
        out_shape=jax.ShapeDtypeStruct((bsz, n), F32),
        compiler_params=_cparams(("arbitrary",)),
        name="mods",
    )(c, w, b.reshape(1, n))


def _rms(x, g):
    r = lax.rsqrt(jnp.mean(x * x, axis=-1, keepdims=True) + EPS)
    return x * r * g


def _inproj_kernel(x_ref, pos_ref, sh_ref, sc_ref, gmix_ref, wcat_ref, gq_ref, gkv_ref,
                   wqt_ref, wk_ref, wvt_ref, vones_ref, bd_ref, bg_ref, invf_ref,
                   qt_ref, k_ref, vt_ref, zr_ref, zi_ref, gate_ref, *, scale):
    hp = N_HEADS * HEAD_PAD
    nt_dims = (((1,), (1,)), ((), ()))
    u = _rms(x_ref[0], gmix_ref[...]) * (1.0 + sc_ref[0]) + sh_ref[0]
    ub = u.astype(BF16)

    o_q, o_kv, o_kr, o_f, o_g = 0, Q_LORA, Q_LORA + KV_LORA, Q_LORA + KV_LORA + 2 * HEAD_PAD, \
        Q_LORA + KV_LORA + 2 * HEAD_PAD + F_WIDTH

    tm = pos_ref.shape[-1]
    ang = invf_ref[...] * pos_ref[0].astype(F32)
    tail = jnp.zeros((HEAD_PAD - NOPE - ROPE, tm), F32)
    cos_t = jnp.concatenate([jnp.ones((NOPE, tm), F32), jnp.cos(ang), tail], axis=0)
    sin_t = jnp.concatenate([jnp.zeros((NOPE, tm), F32), jnp.sin(ang), tail], axis=0)

    ql = jnp.dot(ub, wcat_ref[:, o_q:o_kv], preferred_element_type=F32)
    kvl = jnp.dot(ub, wcat_ref[:, o_kv:o_kr], preferred_element_type=F32)
    kr = jnp.dot(ub, wcat_ref[:, o_kr:o_f], preferred_element_type=F32)
    fin = jnp.dot(ub, wcat_ref[:, o_f:o_g], preferred_element_type=F32).astype(BF16)
    zg = jnp.dot(ub, wcat_ref[:, o_g:], preferred_element_type=F32) + bg_ref[...]

    qn = _rms(ql, gq_ref[...]).astype(BF16)
    kvn = _rms(kvl, gkv_ref[...]).astype(BF16)
    zz = jnp.dot(fin, bd_ref[...], preferred_element_type=F32)
    qab_t = lax.dot_general(wqt_ref[...], qn, nt_dims, preferred_element_type=F32)
    kn = jnp.dot(kvn, wk_ref[...], preferred_element_type=F32)
    vt = lax.dot_general(wvt_ref[...], kvn, nt_dims, preferred_element_type=F32) + vones_ref[...]

    gate_ref[0] = jax.nn.sigmoid(zg).astype(BF16)
    zr_ref[0] = zz[:, :F_WIDTH].astype(BF16)
    zi_ref[0] = zz[:, F_WIDTH:].astype(BF16)
    cq = cos_t * scale
    sq = sin_t * scale
    for h in range(N_HEADS):
        a = qab_t[h * HEAD_PAD:(h + 1) * HEAD_PAD]
        b = qab_t[hp + h * HEAD_PAD:hp + (h + 1) * HEAD_PAD]
        qt_ref[0, h] = (a * cq + b * sq).astype(BF16)
    krr = kr[:, :HEAD_PAD] * cos_t.T + kr[:, HEAD_PAD:] * sin_t.T
    for h in range(N_HEADS):
        k_ref[0, h] = (kn[:, h * HEAD_PAD:(h + 1) * HEAD_PAD] + krr).astype(BF16)
        vt_ref[0, h, 0] = vt[h * V_ROWS:(h + 1) * V_ROWS].astype(BF16)


def _inproj(x, pos_row, sh1, sc1, gmix, wcat, gq, gkv, wqt, wk, wvt, vones, bd, bg, invf, tm):
    bsz, s, d = x.shape
    hp = N_HEADS * HEAD_PAD
    nt = s // tm
    const = lambda shape: pl.BlockSpec(shape, lambda b, i: (0,) * len(shape))
    tok = lambda w: pl.BlockSpec((1, tm, w), lambda b, i: (b, i, 0))
    per_b = pl.BlockSpec((1, 1, d), lambda b, i: (b, 0, 0))
    scale = float((NOPE + ROPE) ** -0.5 * np.log2(np.e))
    return pl.pallas_call(
        functools.partial(_inproj_kernel, scale=scale),
        grid=(bsz, nt),
        in_specs=[tok(d), pl.BlockSpec((1, 1, tm), lambda b, i: (b, 0, i)), per_b, per_b,
                  const((1, d)), const(wcat.shape), const((1, Q_LORA)), const((1, KV_LORA)),
                  const(wqt.shape), const(wk.shape), const(wvt.shape), const(vones.shape),
                  const(bd.shape), const((1, bg.shape[1])), const((ROPE, 1))],
        out_specs=[pl.BlockSpec((1, N_HEADS, HEAD_PAD, tm), lambda b, i: (b, 0, 0, i)),
                   pl.BlockSpec((1, N_HEADS, tm, HEAD_PAD), lambda b, i: (b, 0, i, 0)),
                   pl.BlockSpec((1, N_HEADS, 1, V_ROWS, tm), lambda b, i: (b, 0, i, 0, 0)),
                   tok(F_WIDTH), tok(F_WIDTH), tok(2 * d)],
        out_shape=[jax.ShapeDtypeStruct((bsz, N_HEADS, HEAD_PAD, s), BF16),
                   jax.ShapeDtypeStruct((bsz, N_HEADS, s, HEAD_PAD), BF16),
                   jax.ShapeDtypeStruct((bsz, N_HEADS, nt, V_ROWS, tm), BF16),
                   jax.ShapeDtypeStruct((bsz, s, F_WIDTH), BF16),
                   jax.ShapeDtypeStruct((bsz, s, F_WIDTH), BF16),
                   jax.ShapeDtypeStruct((bsz, s, 2 * d), BF16)],
        compiler_params=_cparams(("parallel", "parallel")),
        name="inproj",
    )(x, pos_row, sh1, sc1, gmix, wcat, gq, gkv, wqt, wk, wvt, vones, bd, bg, invf)


def _attn_kernel(qt_ref, k_ref, vt_ref, o_ref, *p_refs, nk):
    tq = qt_ref.shape[3]
    tk = vt_ref.shape[4]
    qts = [qt_ref[0, hh] for hh in range(2)]
    nbuf = ATTN_DEPTH + 1

    def scores(c, hh):
        ss = []
        for i in range(tk // QK_ROWS):
            off = c * tk + i * QK_ROWS
            ss.append(jnp.dot(k_ref[0, hh, off:off + QK_ROWS, :], qts[hh],
                              preferred_element_type=F32))
        return ss

    def probs(ss, m, p_ref):
        m_new = m
        for s in ss:
            m_new = jnp.maximum(m_new, jnp.max(s, axis=0, keepdims=True))
        for i, s in enumerate(ss):
            p_ref[i * QK_ROWS:(i + 1) * QK_ROWS, :] = jnp.exp2(s - m_new).astype(BF16)
        return m_new, jnp.exp2(m - m_new)

    def accumulate(c, hh, acc, alpha, p_ref):
        return alpha * acc + jnp.dot(vt_ref[0, hh, c], p_ref[...], preferred_element_type=F32)

    hs = range(2)
    buf = lambda c, hh: p_refs[hh * nbuf + c % nbuf]
    m = [jnp.full((1, tq), -jnp.inf, F32) for _ in hs]
    alpha = [[None] * nk for _ in hs]
    accs = [jnp.zeros((V_ROWS, tq), F32) for _ in hs]
    for c in range(nk + ATTN_DEPTH):
        if c < nk:
            s = [scores(c, hh) for hh in hs]
        if c >= ATTN_DEPTH:
            for hh in hs:
                accs[hh] = accumulate(c - ATTN_DEPTH, hh, accs[hh], alpha[hh][c - ATTN_DEPTH],
                                      buf(c - ATTN_DEPTH, hh))
        if c < nk:
            for hh in hs:
                m[hh], alpha[hh][c] = probs(s[hh], m[hh], buf(c, hh))
    out_t = jnp.concatenate([a[:V_DIM] / a[V_DIM:V_DIM + 1] for a in accs], axis=0)
    o_ref[0] = out_t.T.astype(BF16)


def _attention(qt, k, vt, tq):
    bsz, nh, s, _ = k.shape
    nk, tk = vt.shape[2], vt.shape[4]
    return pl.pallas_call(
        functools.partial(_attn_kernel, nk=nk),
        grid=(bsz, nh // 2, s // tq),
        in_specs=[pl.BlockSpec((1, 2, HEAD_PAD, tq), lambda b, j, i: (b, j, 0, i)),
                  pl.BlockSpec((1, 2, s, HEAD_PAD), lambda b, j, i: (b, j, 0, 0)),
                  pl.BlockSpec((1, 2, nk, V_ROWS, tk), lambda b, j, i: (b, j, 0, 0, 0))],
        out_specs=pl.BlockSpec((1, tq, HEAD_PAD), lambda b, j, i: (b, i, j)),
        out_shape=jax.ShapeDtypeStruct((bsz, s, nh * V_DIM), BF16),
        scratch_shapes=[pltpu.VMEM((tk, tq), BF16)] * (2 * (ATTN_DEPTH + 1)),
        compiler_params=_cparams(("parallel", "parallel", "arbitrary")),
        name="attention",
    )(qt, k, vt)


def _fft_a_kernel(zr_ref, zi_ref, m1_ref, br_ref, bi_ref):
    n1 = zr_ref.shape[1]
    ts2 = m1_ref.shape[0]
    for j in range(ts2):
        sl = slice(j * F_WIDTH, (j + 1) * F_WIDTH)
        xx = jnp.concatenate([zr_ref[0, :, sl], zi_ref[0, :, sl]], axis=0)
        y = jnp.dot(m1_ref[j], xx, preferred_element_type=F32)
        br_ref[0, :, sl] = y[:n1].astype(BF16)
        bi_ref[0, :, sl] = y[n1:].astype(BF16)


def _fft_a(zr, zi, m1, ts2):
    bsz, s, _ = zr.shape
    n1 = s // FFT_N2
    zr2 = zr.reshape(bsz, n1, FFT_N2 * F_WIDTH)
    zi2 = zi.reshape(bsz, n1, FFT_N2 * F_WIDTH)
    blk = pl.BlockSpec((1, n1, ts2 * F_WIDTH), lambda b, j: (b, 0, j))
    return pl.pallas_call(
        _fft_a_kernel,
        grid=(bsz, FFT_N2 // ts2),
        in_specs=[blk, blk, pl.BlockSpec((ts2, 2 * n1, 2 * n1), lambda b, j: (j, 0, 0))],
        out_specs=[blk, blk],
        out_shape=[jax.ShapeDtypeStruct(zr2.shape, BF16)] * 2,
        compiler_params=_cparams(("parallel", "parallel")),
        name="fft_a",
    )(zr2, zi2, m1)


def _fft_b_kernel(br_ref, bi_ref, wb_ref, o_ref):
    tk1 = br_ref.shape[1]
    for j in range(tk1):
        xx = jnp.concatenate([br_ref[0, j], bi_ref[0, j]], axis=0)
        y = jnp.dot(wb_ref[...], xx, preferred_element_type=F32)
        o_ref[0, :, j * F_WIDTH:(j + 1) * F_WIDTH] = y.astype(BF16)


def _fft_b(br, bi, wb, tk1):
    bsz, n1, _ = br.shape
    br4 = br.reshape(bsz, n1, FFT_N2, F_WIDTH)
    bi4 = bi.reshape(bsz, n1, FFT_N2, F_WIDTH)
    blk = pl.BlockSpec((1, tk1, FFT_N2, F_WIDTH), lambda b, j: (b, j, 0, 0))
    out = pl.pallas_call(
        _fft_b_kernel,
        grid=(bsz, n1 // tk1),
        in_specs=[blk, blk, pl.BlockSpec((FFT_N2, 2 * FFT_N2), lambda b, j: (0, 0))],
        out_specs=pl.BlockSpec((1, FFT_N2, tk1 * F_WIDTH), lambda b, j: (b, 0, j)),
        out_shape=jax.ShapeDtypeStruct((bsz, FFT_N2, n1 * F_WIDTH), BF16),
        compiler_params=_cparams(("parallel", "parallel")),
        name="fft_b",
    )(br4, bi4, wb)
    return out.reshape(bsz, n1 * FFT_N2, F_WIDTH)


def _dft_constants(s):
    n1, n2 = s // FFT_N2, FFT_N2
    m = np.arange(F_GDIM)
    th = 2.0 * np.pi * np.outer(m, m) / F_GDIM
    eye = np.eye(F_GROUPS)
    bd = np.concatenate([np.kron(eye, np.cos(th)), -np.kron(eye, np.sin(th))], axis=1) / np.sqrt(F_GDIM)
    k1 = np.arange(n1)[None, :, None]
    s1 = np.arange(n1)[None, None, :]
    s2 = np.arange(n2)[:, None, None]
    ph = 2.0 * np.pi * ((k1 * (n2 * s1 + s2)) % s) / s
    cr, ci = np.cos(ph) / np.sqrt(n1), -np.sin(ph) / np.sqrt(n1)
    m1 = np.concatenate([np.concatenate([cr, -ci], axis=2), np.concatenate([ci, cr], axis=2)], axis=1)
    k2 = np.arange(n2)
    tb = 2.0 * np.pi * np.outer(k2, k2) / n2
    wb = np.concatenate([np.cos(tb), np.sin(tb)], axis=1) / np.sqrt(n2)
    return (jnp.asarray(bd, BF16), jnp.asarray(m1, BF16), jnp.asarray(wb, BF16))


def _route(parts, bias):
    logit = parts[:, :LANES] + parts[:, LANES:] + bias
    lane = lax.broadcasted_iota(jnp.int32, logit.shape, 1)
    neg = -jnp.inf
    big = jnp.int32(LANES)

    def softmax_masked(mask):
        z = jnp.where(mask, logit, neg)
        e = jnp.exp(z - jnp.max(z, axis=-1, keepdims=True))
        return e / jnp.sum(e, axis=-1, keepdims=True)

    def top1(p, mask):
        pm = jnp.where(mask, p, -1.0)
        best = jnp.max(pm, axis=-1, keepdims=True)
        idx = jnp.min(jnp.where(pm == best, lane, big), axis=-1, keepdims=True)
        return best, idx

    gmask = (lane >= N_EXPERTS) & (lane < N_EXPERTS + N_GROUPS)
    g_prob = softmax_masked(gmask)
    g_p, g_lane = top1(g_prob, gmask)
    e_lo = (g_lane - N_EXPERTS) * E_PER_GROUP
    emask = (lane >= e_lo) & (lane < e_lo + E_PER_GROUP)
    e_prob = softmax_masked(emask)
    p1, i1 = top1(e_prob, emask)
    p2, i2 = top1(e_prob, emask & (lane != i1))
    den = p1 + p2
    w1 = g_p * (p1 / den)
    w2 = g_p * (p2 / den)
    route = jnp.where(lane == 0, i1.astype(F32), jnp.where(lane == 1, i2.astype(F32),
                      jnp.where(lane == 2, w1, jnp.where(lane == 3, w2, 0.0))))
    cnt = jnp.sum(((lane == i1) | (lane == i2)).astype(F32), axis=0, keepdims=True)
    return route, cnt


def _merge_kernel(at_ref, fo_ref, gate_ref, x_ref, wba_ref, wbf_ref, wout_ref, gt1_ref, sh2_ref,
                  sc2_ref, gffn_ref, wr_ref, br_ref, h1_ref, v2_ref, route_ref, cnt_ref):
    d = x_ref.shape[-1]
    tm = x_ref.shape[1]
    blocks = [slice(i * MERGE_ROWS, (i + 1) * MERGE_ROWS) for i in range(tm // MERGE_ROWS)]
    a = [jnp.dot(at_ref[0, r, :], wba_ref[...], preferred_element_type=F32) for r in blocks]
    fo = [jnp.dot(fo_ref[0, r, :], wbf_ref[...], preferred_element_type=F32) for r in blocks]
    y = []
    for r, ai, fi in zip(blocks, a, fo):
        merged = gate_ref[0, r, :d].astype(F32) * ai + gate_ref[0, r, d:].astype(F32) * fi
        y.append(jnp.dot(merged.astype(BF16), wout_ref[...], preferred_element_type=F32))
    parts = []
    for r, yi in zip(blocks, y):
        h1 = x_ref[0, r, :] + gt1_ref[0] * yi
        h1_ref[0, r, :] = h1
        v2 = _rms(h1, gffn_ref[...]) * (1.0 + sc2_ref[0]) + sh2_ref[0]
        v_hi = v2.astype(BF16)
        v2_ref[0, r, :] = v_hi
        v_lo = (v2 - v_hi.astype(F32)).astype(BF16)
        parts.append(jnp.dot(jnp.concatenate([v_hi, v_lo], axis=1), wr_ref[...],
                             preferred_element_type=F32))
    cnt = None
    for r, pi in zip(blocks, parts):
        route, c = _route(pi, br_ref[...])
        route_ref[0, r, :] = route
        cnt = c if cnt is None else cnt + c
    cnt_ref[0, 0] = cnt


def _merge(attn, four, gates, x, wba, wbf, wout, gt1, sh2, sc2, gffn, wr, br, tm):
    bsz, s, d = x.shape
    const = lambda shape: pl.BlockSpec(shape, lambda b, i: (0,) * len(shape))
    tok = lambda w: pl.BlockSpec((1, tm, w), lambda b, i: (b, i, 0))
    per_b = pl.BlockSpec((1, 1, d), lambda b, i: (b, 0, 0))
    return pl.pallas_call(
        _merge_kernel,
        grid=(bsz, s // tm),
        in_specs=[tok(attn.shape[-1]), tok(F_WIDTH), tok(2 * d), tok(d), const(wba.shape),
                  const(wbf.shape), const(wout.shape), per_b, per_b, per_b, const((1, d)),
                  const(wr.shape), const((1, LANES))],
        out_specs=[tok(d), tok(d), tok(LANES),
                   pl.BlockSpec((1, 1, 1, LANES), lambda b, i: (b, i, 0, 0))],
        out_shape=[jax.ShapeDtypeStruct((bsz, s, d), F32), jax.ShapeDtypeStruct((bsz, s, d), BF16),
                   jax.ShapeDtypeStruct((bsz, s, LANES), F32),
                   jax.ShapeDtypeStruct((bsz, s // tm, 1, LANES), F32)],
        compiler_params=_cparams(("parallel", "parallel")),
        name="merge",
    )(attn, four, gates, x, wba, wbf, wout, gt1, sh2, sc2, gffn, wr, br)


GRAN = 16
MOE_ROW_TILE = 1024


def _moe_metadata(cnt, tm):
    ntile = cnt.shape[0]

    def cumsum(a, axis):
        n = a.shape[axis]
        tri = jnp.asarray(np.triu(np.ones((n, n), np.float32)))
        af = a.astype(F32)
        out = (jnp.dot(af, tri, precision=lax.Precision.HIGHEST) if axis == a.ndim - 1
               else jnp.dot(tri.T, af, precision=lax.Precision.HIGHEST))
        return out.astype(jnp.int32)

    c = cnt[:, :N_EXPERTS].astype(jnp.int32)
    pc = (c + GRAN - 1) // GRAN * GRAN
    loff = cumsum(pc, 1) - pc
    tot = jnp.sum(pc, axis=0)
    tot_r = (tot + MOE_ROW_TILE - 1) // MOE_ROW_TILE * MOE_ROW_TILE
    ends = cumsum(tot_r[None, :], 1)[0]
    base = ends - tot_r
    goff = base[None, :] + cumsum(pc, 0) - pc
    cap_tiles = (2 * ntile * tm + ntile * N_EXPERTS * GRAN) // MOE_ROW_TILE + N_EXPERTS
    n_valid = (ends[-1] // MOE_ROW_TILE).astype(jnp.int32)
    tile_start = jnp.arange(cap_tiles, dtype=jnp.int32) * MOE_ROW_TILE
    tile_expert = jnp.minimum(jnp.sum((ends[None, :] <= tile_start[:, None]).astype(jnp.int32), axis=1),
                              N_EXPERTS - 1)
    flat = lambda a: a.reshape(-1).astype(jnp.int32)
    tail = tot_r - tot + jnp.where(jnp.arange(N_EXPERTS) == N_EXPERTS - 1,
                                   cap_tiles * MOE_ROW_TILE - ends[-1], 0)
    meta = dict(loff=flat(loff), goff=flat(goff), ngran=flat(pc // GRAN),
                ntot=flat(jnp.sum(pc, axis=1) // GRAN),
                tail_start=flat(base + tot), tail_n=flat(tail // GRAN),
                tile_expert=tile_expert, n_valid=n_valid.reshape(1))
    return meta, loff.astype(F32), cap_tiles


def _run_copies(i, loff_s, goff_s, ngran_s, make_copy):
    def per_expert(e, n_started):
        n = ngran_s[i * N_EXPERTS + e]
        lo = loff_s[i * N_EXPERTS + e]
        go = goff_s[i * N_EXPERTS + e]

        def per_granule(g, carry):
            make_copy(pl.multiple_of(lo + g * GRAN, GRAN), pl.multiple_of(go + g * GRAN, GRAN)).start()
            return carry

        lax.fori_loop(0, n, per_granule, 0)
        return n_started + n

    return lax.fori_loop(0, N_EXPERTS, per_expert, 0)


def _wait_copies(n, make_copy):
    def one(_, carry):
        make_copy(0, 0).wait()
        return carry
    lax.fori_loop(0, n, one, 0)


def _local_positions(route, loffv, ltri):
    lane = lax.broadcasted_iota(jnp.int32, route.shape, 1).astype(F32)
    o1 = lane == route[:, 0:1]
    o2 = lane == route[:, 1:2]
    before = jnp.dot(ltri, (o1 | o2).astype(BF16), preferred_element_type=F32)
    start = before + loffv
    pos1 = jnp.sum(jnp.where(o1, start, 0.0), axis=-1, keepdims=True)
    pos2 = jnp.sum(jnp.where(o2, start, 0.0), axis=-1, keepdims=True)
    return pos1, pos2


def _dispatch_kernel(loff_s, goff_s, ngran_s, ntot_s, tstart_s, tn_s, v2_ref, route_ref, loffv_ref,
                     ltri_ref, xs_ref, posw_ref, loc_ref, zero_ref, sem):
    i = pl.program_id(0)
    last = pl.num_programs(0) - 1
    slot = i % 2
    rows = loc_ref.shape[1]
    route = route_ref[...]
    pos1, pos2 = _local_positions(route, loffv_ref[0], ltri_ref[...])
    lane = lax.broadcasted_iota(jnp.int32, route.shape, 1)
    posw = jnp.where(lane == 0, pos1, jnp.where(lane == 1, pos2, route))
    posw_ref[...] = posw
    pos_t = posw.T
    riota = lax.broadcasted_iota(jnp.int32, (rows, 1), 0).astype(F32)
    onehot = ((riota == pos_t[0:1]) | (riota == pos_t[1:2])).astype(BF16)
    loc_ref[slot] = jnp.dot(onehot, v2_ref[...], preferred_element_type=F32).astype(BF16)

    def out_copy(s):
        return lambda lo, go: pltpu.make_async_copy(
            loc_ref.at[s, pl.ds(lo, GRAN)], xs_ref.at[pl.ds(go, GRAN)], sem.at[s])

    _run_copies(i, loff_s, goff_s, ngran_s, out_copy(slot))

    @pl.when(i > 0)
    def _():
        _wait_copies(ntot_s[i - 1], out_copy(1 - slot))

    @pl.when(i == last)
    def _():
        _wait_copies(ntot_s[i], out_copy(slot))
        zero_ref[...] = jnp.zeros(zero_ref.shape, BF16)

        def zero_copy(go):
            return pltpu.make_async_copy(zero_ref, xs_ref.at[pl.ds(go, GRAN)], sem.at[0])

        def per_expert(e, n_started):
            def per_granule(g, carry):
                zero_copy(pl.multiple_of(tstart_s[e] + g * GRAN, GRAN)).start()
                return carry
            lax.fori_loop(0, tn_s[e], per_granule, 0)
            return n_started + tn_s[e]

        nz = lax.fori_loop(0, N_EXPERTS, per_expert, 0)
        _wait_copies(nz, lambda lo, go: zero_copy(go))


def _dispatch(v2, route, meta, loffv, ltri, cap_tiles, tm, rows):
    t, d = v2.shape
    ntile = t // tm
    grid_spec = pltpu.PrefetchScalarGridSpec(
        num_scalar_prefetch=6,
        grid=(ntile,),
        in_specs=[pl.BlockSpec((tm, d), lambda i, *_: (i, 0)),
                  pl.BlockSpec((tm, LANES), lambda i, *_: (i, 0)),
                  pl.BlockSpec((1, 1, LANES), lambda i, *_: (i, 0, 0)),
                  pl.BlockSpec((tm, tm), lambda i, *_: (0, 0))],
        out_specs=[pl.BlockSpec(memory_space=pl.ANY),
                   pl.BlockSpec((tm, LANES), lambda i, *_: (i, 0))],
        scratch_shapes=[pltpu.VMEM((2, rows, d), BF16), pltpu.VMEM((GRAN, d), BF16),
                        pltpu.SemaphoreType.DMA((2,))],
    )
    return pl.pallas_call(
        _dispatch_kernel,
        grid_spec=grid_spec,
        out_shape=[jax.ShapeDtypeStruct((cap_tiles * MOE_ROW_TILE, d), BF16),
                   jax.ShapeDtypeStruct((t, LANES), F32)],
        compiler_params=_cparams(("arbitrary",)),
        name="dispatch",
    )(meta["loff"], meta["goff"], meta["ngran"], meta["ntot"], meta["tail_start"], meta["tail_n"],
      v2, route, loffv, ltri)


def _experts_kernel(texp_s, nvalid_s, x_ref, wgu_ref, wd_ref, y_ref):
    del texp_s

    @pl.when(pl.program_id(0) < nvalid_s[0])
    def _():
        hc = jnp.dot(x_ref[...], wgu_ref[0], preferred_element_type=F32)
        g = hc[:, :D_EXPERT]
        hid = (g * jax.nn.sigmoid(g) * hc[:, D_EXPERT:]).astype(BF16)
        y_ref[...] = jnp.dot(hid, wd_ref[0], preferred_element_type=F32).astype(BF16)

    @pl.when(pl.program_id(0) >= nvalid_s[0])
    def _():
        y_ref[...] = jnp.zeros(y_ref.shape, BF16)


def _experts(xs, wgu, wd, meta, cap_tiles):
    d = xs.shape[1]
    row = lambda j, texp, nv: (jnp.minimum(j, nv[0] - 1), 0)
    wsel = lambda j, texp, nv: (texp[jnp.minimum(j, nv[0] - 1)], 0, 0)
    grid_spec = pltpu.PrefetchScalarGridSpec(
        num_scalar_prefetch=2,
        grid=(cap_tiles,),
        in_specs=[pl.BlockSpec((MOE_ROW_TILE, d), row),
                  pl.BlockSpec((1, d, 2 * D_EXPERT), wsel),
                  pl.BlockSpec((1, D_EXPERT, d), wsel)],
        out_specs=pl.BlockSpec((MOE_ROW_TILE, d), lambda j, texp, nv: (j, 0)),
    )
    return pl.pallas_call(
        _experts_kernel,
        grid_spec=grid_spec,
        out_shape=jax.ShapeDtypeStruct(xs.shape, BF16),
        compiler_params=_cparams(("arbitrary",)),
        name="experts",
    )(meta["tile_expert"], meta["n_valid"], xs, wgu, wd)


def _combine_kernel(loff_s, goff_s, ngran_s, ntot_s, posw_ref, ys_ref, h1_ref, gt2_ref, shf_ref,
                    scf_ref, gfin_ref, o_ref, loc_ref, sem):
    i = pl.program_id(0)
    slot = i % 2
    rows = loc_ref.shape[1]

    def in_copy(s):
        return lambda lo, go: pltpu.make_async_copy(
            ys_ref.at[pl.ds(go, GRAN)], loc_ref.at[s, pl.ds(lo, GRAN)], sem.at[s])

    def fetch(tile, s):
        loc_ref[s] = jnp.zeros(loc_ref.shape[1:], BF16)
        _run_copies(tile, loff_s, goff_s, ngran_s, in_copy(s))

    @pl.when(i == 0)
    def _():
        fetch(0, 0)

    @pl.when(i + 1 < pl.num_programs(0))
    def _():
        fetch(i + 1, 1 - slot)

    _wait_copies(ntot_s[i], in_copy(slot))

    posw = posw_ref[...]
    riota = lax.broadcasted_iota(jnp.int32, (1, rows), 1).astype(F32)
    qw = (jnp.where(riota == posw[:, 0:1], posw[:, 2:3], 0.0)
          + jnp.where(riota == posw[:, 1:2], posw[:, 3:4], 0.0)).astype(BF16)
    moe = jnp.dot(qw, loc_ref[slot], preferred_element_type=F32)
    h2 = h1_ref[...] + gt2_ref[0] * moe
    o_ref[...] = _rms(h2, gfin_ref[...]) * (1.0 + scf_ref[0]) + shf_ref[0]


def _combine(posw, ys, h1, gt2, shf, scf, gfin, meta, tm, rows, tiles_per_batch):
    t, d = h1.shape
    per_b = pl.BlockSpec((1, 1, d), lambda i, *_: (i // tiles_per_batch, 0, 0))
    grid_spec = pltpu.PrefetchScalarGridSpec(
        num_scalar_prefetch=4,
        grid=(t // tm,),
        in_specs=[pl.BlockSpec((tm, LANES), lambda i, *_: (i, 0)),
                  pl.BlockSpec(memory_space=pl.ANY),
                  pl.BlockSpec((tm, d), lambda i, *_: (i, 0)),
                  per_b, per_b, per_b, pl.BlockSpec((1, d), lambda i, *_: (0, 0))],
        out_specs=pl.BlockSpec((tm, d), lambda i, *_: (i, 0)),
        scratch_shapes=[pltpu.VMEM((2, rows, d), BF16), pltpu.SemaphoreType.DMA((2,))],
    )
    return pl.pallas_call(
        _combine_kernel,
        grid_spec=grid_spec,
        out_shape=jax.ShapeDtypeStruct((t, d), F32),
        compiler_params=_cparams(("arbitrary",)),
        name="combine",
    )(meta["loff"], meta["goff"], meta["ngran"], meta["ntot"], posw, ys, h1, gt2, shf, scf, gfin)


def _rot_cols(w):
    half = ROPE // 2
    return jnp.concatenate([-w[..., half:], w[..., :half]], axis=-1)


def _pad_cols(w, lo, width):
    return jnp.pad(w, [(0, 0)] * (w.ndim - 1) + [(lo, width - lo - w.shape[-1])])


def _prep_weights(w_in, w_uq, w_uk, w_uv):
    o2, o3 = Q_LORA + KV_LORA, Q_LORA + KV_LORA + ROPE
    w_kr = w_in[:, o2:o3]
    wcat = jnp.concatenate([
        w_in[:, :o2],
        _pad_cols(w_kr, NOPE, HEAD_PAD), _pad_cols(_rot_cols(w_kr), NOPE, HEAD_PAD),
        w_in[:, o3:]], axis=1).astype(BF16)
    wq_a = _pad_cols(w_uq, 0, HEAD_PAD)
    wq_b = _pad_cols(_rot_cols(w_uq[..., NOPE:]), NOPE, HEAD_PAD)
    r = w_uq.shape[0]
    wqt = jnp.concatenate([wq_a.reshape(r, -1), wq_b.reshape(r, -1)], axis=1).T.astype(BF16)
    wk = _pad_cols(w_uk, 0, HEAD_PAD).reshape(r, -1).astype(BF16)
    wvt = _pad_cols(w_uv, 0, V_ROWS).reshape(r, -1).T.astype(BF16)
    ones = np.zeros((N_HEADS, V_ROWS), np.float32)
    ones[:, V_DIM] = 1.0
    return wcat, wqt, wk, wvt, jnp.asarray(ones.reshape(-1, 1))


def kernel(x, c, positions, w_ada, b_ada, g_norm_mix, w_in, g_q_lat, g_kv_lat, w_uq, w_uk, w_uv, w_branch_attn, w_branch_fourier, b_gates, w_out, g_norm_ffn, w_router_group, b_router_group, w_router_expert, b_router_expert, w_expert_gate, w_expert_up, w_expert_down, w_ada_final, b_ada_final, g_norm_final):
    bsz, s, d = x.shape
    assert w_ada.shape[0] == 1 and s % FFT_N2 == 0
    tm = min(512, s)

    mod = _mods(c, w_ada[0], b_ada[0])
    fmod = _mods(c, w_ada_final, b_ada_final)
    sh1, sc1, gt1, sh2, sc2, gt2 = [m.reshape(bsz, 1, d) for m in jnp.split(mod, 6, axis=-1)]
    shf, scf = [m.reshape(bsz, 1, d) for m in jnp.split(fmod, 2, axis=-1)]

    wcat, wqt, wk, wvt, vones = _prep_weights(w_in[0], w_uq[0], w_uk[0], w_uv[0])
    bd, m1, wb = _dft_constants(s)
    inv_freq = ROPE_THETA ** (-jnp.arange(0, ROPE, 2, dtype=F32) / ROPE)
    invf = jnp.concatenate([inv_freq, inv_freq]).reshape(ROPE, 1)

    qt, k, vt, zr, zi, gates = _inproj(
        x, positions.reshape(bsz, 1, s), sh1, sc1, g_norm_mix[0][None, :], wcat,
        g_q_lat[0][None, :], g_kv_lat[0][None, :], wqt, wk, wvt, vones, bd, b_gates[0][None, :],
        invf, tm)

    attn = _attention(qt, k, vt, tq=min(256, s))
    br, bi = _fft_a(zr, zi, m1, ts2=8)
    four = _fft_b(br, bi, wb, tk1=16)

    wr = jnp.concatenate([w_router_expert[0].reshape(d, N_EXPERTS), w_router_group[0]], axis=1)
    wr = _pad_cols(wr, 0, LANES)
    wr_top = lax.bitcast_convert_type(
        lax.bitcast_convert_type(wr, jnp.uint32) & jnp.uint32(0xFFFF0000), F32)
    wr_hi = wr_top.astype(BF16)
    wr_lo = (wr - wr_top).astype(BF16)
    wr = jnp.concatenate([jnp.concatenate([wr_hi, wr_lo], axis=1),
                          jnp.concatenate([wr_hi, jnp.zeros_like(wr_lo)], axis=1)], axis=0)
    brt = _pad_cols(jnp.concatenate([b_router_expert[0].reshape(-1), b_router_group[0]])[None, :], 0, LANES)
    h1, v2, route, cnt = _merge(attn, four, gates, x, w_branch_attn[0].astype(BF16),
                                w_branch_fourier[0].astype(BF16), w_out[0].astype(BF16), gt1, sh2,
                                sc2, g_norm_ffn[0][None, :], wr, brt, tm)

    t = bsz * s
    meta, loffv, cap_tiles = _moe_metadata(cnt.reshape(-1, LANES), tm)
    rows = 2 * tm + N_EXPERTS * GRAN
    loffv = _pad_cols(loffv, 0, LANES).reshape(-1, 1, LANES)
    ltri = jnp.asarray(np.tril(np.ones((tm, tm), np.float32), -1), BF16)
    xs, posw = _dispatch(v2.reshape(t, d), route.reshape(t, LANES), meta, loffv, ltri, cap_tiles,
                         tm, rows)
    wgu = jnp.concatenate([w_expert_gate[0], w_expert_up[0]], axis=-1).astype(BF16)
    wd = w_expert_down[0].astype(BF16)
    ys = _experts(xs, wgu, wd, meta, cap_tiles)
    out = _combine(posw, ys, h1.reshape(t, d), gt2, shf, scf, g_norm_final[None, :], meta, tm, rows,
                   s // tm)
    return out.reshape(bsz, s, d)
```

```python
import functools

import numpy as np
import jax
import jax.numpy as jnp
from jax import lax
from jax.experimental import pallas as pl
from jax.experimental.pallas import tpu as pltpu

F32 = jnp.float32
BF16 = jnp.bfloat16

N_HEADS = 8
NOPE = 64
ROPE = 32
V_DIM = 64
HEAD_PAD = 128
V_ROWS = 96
Q_LORA = 256
KV_LORA = 256
ROPE_THETA = 10000.0
F_GROUPS = 8
F_GDIM = 64
F_WIDTH = F_GROUPS * F_GDIM
N_GROUPS = 4
E_PER_GROUP = 8
N_EXPERTS = N_GROUPS * E_PER_GROUP
D_EXPERT = 256
EPS = 1e-6
FFT_N2 = 64
LANES = 128
MERGE_ROWS = 128
ATTN_DEPTH = 1
QK_ROWS = 256
VMEM_LIMIT = 56 * 1024 * 1024


def _cparams(sem):
    return pltpu.CompilerParams(dimension_semantics=sem, vmem_limit_bytes=VMEM_LIMIT)


def _mods_kernel(c_ref, w_ref, b_ref, o_ref):
    c = c_ref[...]
    ca = c * jax.nn.sigmoid(c)
    o_ref[...] = jnp.dot(ca, w_ref[...], preferred_element_type=F32,
                         precision=lax.Precision.HIGHEST) + b_ref[...]


def _mods(c, w, b):
    bsz, d = c.shape
    n = w.shape[1]
    tn = 1024
    return pl.pallas_call(
        _mods_kernel,
        grid=(n // tn,),
        in_specs=[pl.BlockSpec((bsz, d), lambda j: (0, 0)),
                  pl.BlockSpec((d, tn), lambda j: (0, j)),
                  pl.BlockSpec((1, tn), lambda j: (0, j))],
        out_specs=pl.BlockSpec((bsz, tn), lambda j: (0, j)),
        out_shape=jax.ShapeDtypeStruct((bsz, n), F32),
        compiler_params=_cparams(("arbitrary",)),
        name="mods",
    )(c, w, b.reshape(1, n))


def _rms(x, g):
    r = lax.rsqrt(jnp.mean(x * x, axis=-1, keepdims=True) + EPS)
    return x * r * g


def _inproj_kernel(x_ref, pos_ref, sh_ref, sc_ref, gmix_ref, wcat_ref, gq_ref, gkv_ref,
                   wqt_ref, wk_ref, wvt_ref, vones_ref, bd_ref, bg_ref, invf_ref,
                   qt_ref, k_ref, vt_ref, zr_ref, zi_ref, gate_ref, *, scale):
    hp = N_HEADS * HEAD_PAD
    nt_dims = (((1,), (1,)), ((), ()))
    u = _rms(x_ref[0], gmix_ref[...]) * (1.0 + sc_ref[0]) + sh_ref[0]
    ub = u.astype(BF16)

    o_q, o_kv, o_kr, o_f, o_g = 0, Q_LORA, Q_LORA + KV_LORA, Q_LORA + KV_LORA + 2 * HEAD_PAD, \
        Q_LORA + KV_LORA + 2 * HEAD_PAD + F_WIDTH

    tm = pos_ref.shape[-1]
    ang = invf_ref[...] * pos_ref[0].astype(F32)
    tail = jnp.zeros((HEAD_PAD - NOPE - ROPE, tm), F32)
    cos_t = jnp.concatenate([jnp.ones((NOPE, tm), F32), jnp.cos(ang), tail], axis=0)
    sin_t = jnp.concatenate([jnp.zeros((NOPE, tm), F32), jnp.sin(ang), tail], axis=0)

    ql = jnp.dot(ub, wcat_ref[:, o_q:o_kv], preferred_element_type=F32)
    kvl = jnp.dot(ub, wcat_ref[:, o_kv:o_kr], preferred_element_type=F32)
    kr = jnp.dot(ub, wcat_ref[:, o_kr:o_f], preferred_element_type=F32)
    fin = jnp.dot(ub, wcat_ref[:, o_f:o_g], preferred_element_type=F32).astype(BF16)
    zg = jnp.dot(ub, wcat_ref[:, o_g:], preferred_element_type=F32) + bg_ref[...]

    qn = _rms(ql, gq_ref[...]).astype(BF16)
    kvn = _rms(kvl, gkv_ref[...]).astype(BF16)
    zz = jnp.dot(fin, bd_ref[...], preferred_element_type=F32)
    qab_t = lax.dot_general(wqt_ref[...], qn, nt_dims, preferred_element_type=F32)
    kn = jnp.dot(kvn, wk_ref[...], preferred_element_type=F32)
    vt = lax.dot_general(wvt_ref[...], kvn, nt_dims, preferred_element_type=F32) + vones_ref[...]

    gate_ref[0] = jax.nn.sigmoid(zg).astype(BF16)
    zr_ref[0] = zz[:, :F_WIDTH].astype(BF16)
    zi_ref[0] = zz[:, F_WIDTH:].astype(BF16)
    cq = cos_t * scale
    sq = sin_t * scale
    for h in range(N_HEADS):
        a = qab_t[h * HEAD_PAD:(h + 1) * HEAD_PAD]
        b = qab_t[hp + h * HEAD_PAD:hp + (h + 1) * HEAD_PAD]
        qt_ref[0, h] = (a * cq + b * sq).astype(BF16)
    krr = kr[:, :HEAD_PAD] * cos_t.T + kr[:, HEAD_PAD:] * sin_t.T
    for h in range(N_HEADS):
        k_ref[0, h] = (kn[:, h * HEAD_PAD:(h + 1) * HEAD_PAD] + krr).astype(BF16)
        vt_ref[0, h, 0] = vt[h * V_ROWS:(h + 1) * V_ROWS].astype(BF16)


def _inproj(x, pos_row, sh1, sc1, gmix, wcat, gq, gkv, wqt, wk, wvt, vones, bd, bg, invf, tm):
    bsz, s, d = x.shape
    hp = N_HEADS * HEAD_PAD
    nt = s // tm
    const = lambda shape: pl.BlockSpec(shape, lambda b, i: (0,) * len(shape))
    tok = lambda w: pl.BlockSpec((1, tm, w), lambda b, i: (b, i, 0))
    per_b = pl.BlockSpec((1, 1, d), lambda b, i: (b, 0, 0))
    scale = float((NOPE + ROPE) ** -0.5 * np.log2(np.e))
    return pl.pallas_call(
        functools.partial(_inproj_kernel, scale=scale),
        grid=(bsz, nt),
        in_specs=[tok(d), pl.BlockSpec((1, 1, tm), lambda b, i: (b, 0, i)), per_b, per_b,
                  const((1, d)), const(wcat.shape), const((1, Q_LORA)), const((1, KV_LORA)),
                  const(wqt.shape), const(wk.shape), const(wvt.shape), const(vones.shape),
                  const(bd.shape), const((1, bg.shape[1])), const((ROPE, 1))],
        out_specs=[pl.BlockSpec((1, N_HEADS, HEAD_PAD, tm), lambda b, i: (b, 0, 0, i)),
                   pl.BlockSpec((1, N_HEADS, tm, HEAD_PAD), lambda b, i: (b, 0, i, 0)),
                   pl.BlockSpec((1, N_HEADS, 1, V_ROWS, tm), lambda b, i: (b, 0, i, 0, 0)),
                   tok(F_WIDTH), tok(F_WIDTH), tok(2 * d)],
        out_shape=[jax.ShapeDtypeStruct((bsz, N_HEADS, HEAD_PAD, s), BF16),
                   jax.ShapeDtypeStruct((bsz, N_HEADS, s, HEAD_PAD), BF16),
                   jax.ShapeDtypeStruct((bsz, N_HEADS, nt, V_ROWS, tm), BF16),
                   jax.ShapeDtypeStruct((bsz, s, F_WIDTH), BF16),
                   jax.ShapeDtypeStruct((bsz, s, F_WIDTH), BF16),
                   jax.ShapeDtypeStruct((bsz, s, 2 * d), BF16)],
        compiler_params=_cparams(("parallel", "parallel")),
        name="inproj",
    )(x, pos_row, sh1, sc1, gmix, wcat, gq, gkv, wqt, wk, wvt, vones, bd, bg, invf)


def _attn_kernel(qt_ref, k_ref, vt_ref, o_ref, *p_refs, nk):
    tq = qt_ref.shape[3]
    tk = vt_ref.shape[4]
    qts = [qt_ref[0, hh] for hh in range(2)]
    nbuf = ATTN_DEPTH + 1

    def scores(c, hh):
        ss = []
        for i in range(tk // QK_ROWS):
            off = c * tk + i * QK_ROWS
            ss.append(jnp.dot(k_ref[0, hh, off:off + QK_ROWS, :], qts[hh],
                              preferred_element_type=F32))
        return ss

    def probs(ss, m, p_ref):
        m_new = m
        for s in ss:
            m_new = jnp.maximum(m_new, jnp.max(s, axis=0, keepdims=True))
        for i, s in enumerate(ss):
            p_ref[i * QK_ROWS:(i + 1) * QK_ROWS, :] = jnp.exp2(s - m_new).astype(BF16)
        return m_new, jnp.exp2(m - m_new)

    def accumulate(c, hh, acc, alpha, p_ref):
        return alpha * acc + jnp.dot(vt_ref[0, hh, c], p_ref[...], preferred_element_type=F32)

    hs = range(2)
    buf = lambda c, hh: p_refs[hh * nbuf + c % nbuf]
    m = [jnp.full((1, tq), -jnp.inf, F32) for _ in hs]
    alpha = [[None] * nk for _ in hs]
    accs = [jnp.zeros((V_ROWS, tq), F32) for _ in hs]
    for c in range(nk + ATTN_DEPTH):
        if c < nk:
            s = [scores(c, hh) for hh in hs]
        if c >= ATTN_DEPTH:
            for hh in hs:
                accs[hh] = accumulate(c - ATTN_DEPTH, hh, accs[hh], alpha[hh][c - ATTN_DEPTH],
                                      buf(c - ATTN_DEPTH, hh))
        if c < nk:
            for hh in hs:
                m[hh], alpha[hh][c] = probs(s[hh], m[hh], buf(c, hh))
    out_t = jnp.concatenate([a[:V_DIM] / a[V_DIM:V_DIM + 1] for a in accs], axis=0)
    o_ref[0] = out_t.T.astype(BF16)


def _attention(qt, k, vt, tq):
    bsz, nh, s, _ = k.shape
    nk, tk = vt.shape[2], vt.shape[4]
    return pl.pallas_call(
        functools.partial(_attn_kernel, nk=nk),
        grid=(bsz, nh // 2, s // tq),
        in_specs=[pl.BlockSpec((1, 2, HEAD_PAD, tq), lambda b, j, i: (b, j, 0, i)),
                  pl.BlockSpec((1, 2, s, HEAD_PAD), lambda b, j, i: (b, j, 0, 0)),
                  pl.BlockSpec((1, 2, nk, V_ROWS, tk), lambda b, j, i: (b, j, 0, 0, 0))],
        out_specs=pl.BlockSpec((1, tq, HEAD_PAD), lambda b, j, i: (b, i, j)),
        out_shape=jax.ShapeDtypeStruct((bsz, s, nh * V_DIM), BF16),
        scratch_shapes=[pltpu.VMEM((tk, tq), BF16)] * (2 * (ATTN_DEPTH + 1)),
        compiler_params=_cparams(("parallel", "parallel", "arbitrary")),
        name="attention",
    )(qt, k, vt)


def _fft_a_kernel(zr_ref, zi_ref, m1_ref, br_ref, bi_ref):
    n1 = zr_ref.shape[1]
    ts2 = m1_ref.shape[0]
    for j in range(ts2):
        sl = slice(j * F_WIDTH, (j + 1) * F_WIDTH)
        xx = jnp.concatenate([zr_ref[0, :, sl], zi_ref[0, :, sl]], axis=0)
        y = jnp.dot(m1_ref[j], xx, preferred_element_type=F32)
        br_ref[0, :, sl] = y[:n1].astype(BF16)
        bi_ref[0, :, sl] = y[n1:].astype(BF16)


def _fft_a(zr, zi, m1, ts2):
    bsz, s, _ = zr.shape
    n1 = s // FFT_N2
    zr2 = zr.reshape(bsz, n1, FFT_N2 * F_WIDTH)
    zi2 = zi.reshape(bsz, n1, FFT_N2 * F_WIDTH)
    blk = pl.BlockSpec((1, n1, ts2 * F_WIDTH), lambda b, j: (b, 0, j))
    return pl.pallas_call(
        _fft_a_kernel,
        grid=(bsz, FFT_N2 // ts2),
        in_specs=[blk, blk, pl.BlockSpec((ts2, 2 * n1, 2 * n1), lambda b, j: (j, 0, 0))],
        out_specs=[blk, blk],
        out_shape=[jax.ShapeDtypeStruct(zr2.shape, BF16)] * 2,
        compiler_params=_cparams(("parallel", "parallel")),
        name="fft_a",
    )(zr2, zi2, m1)


def _fft_b_kernel(br_ref, bi_ref, wb_ref, o_ref):
    tk1 = br_ref.shape[1]
    for j in range(tk1):
        xx = jnp.concatenate([br_ref[0, j], bi_ref[0, j]], axis=0)
        y = jnp.dot(wb_ref[...], xx, preferred_element_type=F32)
        o_ref[0, :, j * F_WIDTH:(j + 1) * F_WIDTH] = y.astype(BF16)


def _fft_b(br, bi, wb, tk1):
    bsz, n1, _ = br.shape
    br4 = br.reshape(bsz, n1, FFT_N2, F_WIDTH)
    bi4 = bi.reshape(bsz, n1, FFT_N2, F_WIDTH)
    blk = pl.BlockSpec((1, tk1, FFT_N2, F_WIDTH), lambda b, j: (b, j, 0, 0))
    out = pl.pallas_call(
        _fft_b_kernel,
        grid=(bsz, n1 // tk1),
        in_specs=[blk, blk, pl.BlockSpec((FFT_N2, 2 * FFT_N2), lambda b, j: (0, 0))],
        out_specs=pl.BlockSpec((1, FFT_N2, tk1 * F_WIDTH), lambda b, j: (b, 0, j)),
        out_shape=jax.ShapeDtypeStruct((bsz, FFT_N2, n1 * F_WIDTH), BF16),
        compiler_params=_cparams(("parallel", "parallel")),
        name="fft_b",
    )(br4, bi4, wb)
    return out.reshape(bsz, n1 * FFT_N2, F_WIDTH)


def _dft_constants(s):
    n1, n2 = s // FFT_N2, FFT_N2
    m = np.arange(F_GDIM)
    th = 2.0 * np.pi * np.outer(m, m) / F_GDIM
    eye = np.eye(F_GROUPS)
    bd = np.concatenate([np.kron(eye, np.cos(th)), -np.kron(eye, np.sin(th))], axis=1) / np.sqrt(F_GDIM)
    k1 = np.arange(n1)[None, :, None]
    s1 = np.arange(n1)[None, None, :]
    s2 = np.arange(n2)[:, None, None]
    ph = 2.0 * np.pi * ((k1 * (n2 * s1 + s2)) % s) / s
    cr, ci = np.cos(ph) / np.sqrt(n1), -np.sin(ph) / np.sqrt(n1)
    m1 = np.concatenate([np.concatenate([cr, -ci], axis=2), np.concatenate([ci, cr], axis=2)], axis=1)
    k2 = np.arange(n2)
    tb = 2.0 * np.pi * np.outer(k2, k2) / n2
    wb = np.concatenate([np.cos(tb), np.sin(tb)], axis=1) / np.sqrt(n2)
    return (jnp.asarray(bd, BF16), jnp.asarray(m1, BF16), jnp.asarray(wb, BF16))


def _route(parts, bias):
    logit = parts[:, :LANES] + parts[:, LANES:] + bias
    lane = lax.broadcasted_iota(jnp.int32, logit.shape, 1)
    neg = -jnp.inf
    big = jnp.int32(LANES)

    def softmax_masked(mask):
        z = jnp.where(mask, logit, neg)
        e = jnp.exp(z - jnp.max(z, axis=-1, keepdims=True))
        return e / jnp.sum(e, axis=-1, keepdims=True)

    def top1(p, mask):
        pm = jnp.where(mask, p, -1.0)
        best = jnp.max(pm, axis=-1, keepdims=True)
        idx = jnp.min(jnp.where(pm == best, lane, big), axis=-1, keepdims=True)
        return best, idx

    gmask = (lane >= N_EXPERTS) & (lane < N_EXPERTS + N_GROUPS)
    g_prob = softmax_masked(gmask)
    g_p, g_lane = top1(g_prob, gmask)
    e_lo = (g_lane - N_EXPERTS) * E_PER_GROUP
    emask = (lane >= e_lo) & (lane < e_lo + E_PER_GROUP)
    e_prob = softmax_masked(emask)
    p1, i1 = top1(e_prob, emask)
    p2, i2 = top1(e_prob, emask & (lane != i1))
    den = p1 + p2
    w1 = g_p * (p1 / den)
    w2 = g_p * (p2 / den)
    route = jnp.where(lane == 0, i1.astype(F32), jnp.where(lane == 1, i2.astype(F32),
                      jnp.where(lane == 2, w1, jnp.where(lane == 3, w2, 0.0))))
    cnt = jnp.sum(((lane == i1) | (lane == i2)).astype(F32), axis=0, keepdims=True)
    return route, cnt


def _merge_kernel(at_ref, fo_ref, gate_ref, x_ref, wba_ref, wbf_ref, wout_ref, gt1_ref, sh2_ref,
                  sc2_ref, gffn_ref, wr_ref, br_ref, h1_ref, v2_ref, route_ref, cnt_ref):
    d = x_ref.shape[-1]
    tm = x_ref.shape[1]
    blocks = [slice(i * MERGE_ROWS, (i + 1) * MERGE_ROWS) for i in range(tm // MERGE_ROWS)]
    a = [jnp.dot(at_ref[0, r, :], wba_ref[...], preferred_element_type=F32) for r in blocks]
    fo = [jnp.dot(fo_ref[0, r, :], wbf_ref[...], preferred_element_type=F32) for r in blocks]
    y = []
    for r, ai, fi in zip(blocks, a, fo):
        merged = gate_ref[0, r, :d].astype(F32) * ai + gate_ref[0, r, d:].astype(F32) * fi
        y.append(jnp.dot(merged.astype(BF16), wout_ref[...], preferred_element_type=F32))
    parts = []
    for r, yi in zip(blocks, y):
        h1 = x_ref[0, r, :] + gt1_ref[0] * yi
        h1_ref[0, r, :] = h1
        v2 = _rms(h1, gffn_ref[...]) * (1.0 + sc2_ref[0]) + sh2_ref[0]
        v_hi = v2.astype(BF16)
        v2_ref[0, r, :] = v_hi
        v_lo = (v2 - v_hi.astype(F32)).astype(BF16)
        parts.append(jnp.dot(jnp.concatenate([v_hi, v_lo], axis=1), wr_ref[...],
                             preferred_element_type=F32))
    cnt = None
    for r, pi in zip(blocks, parts):
        route, c = _route(pi, br_ref[...])
        route_ref[0, r, :] = route
        cnt = c if cnt is None else cnt + c
    cnt_ref[0, 0] = cnt


def _merge(attn, four, gates, x, wba, wbf, wout, gt1, sh2, sc2, gffn, wr, br, tm):
    bsz, s, d = x.shape
    const = lambda shape: pl.BlockSpec(shape, lambda b, i: (0,) * len(shape))
    tok = lambda w: pl.BlockSpec((1, tm, w), lambda b, i: (b, i, 0))
    per_b = pl.BlockSpec((1, 1, d), lambda b, i: (b, 0, 0))
    return pl.pallas_call(
        _merge_kernel,
        grid=(bsz, s // tm),
        in_specs=[tok(attn.shape[-1]), tok(F_WIDTH), tok(2 * d), tok(d), const(wba.shape),
                  const(wbf.shape), const(wout.shape), per_b, per_b, per_b, const((1, d)),
                  const(wr.shape), const((1, LANES))],
        out_specs=[tok(d), tok(d), tok(LANES),
                   pl.BlockSpec((1, 1, 1, LANES), lambda b, i: (b, i, 0, 0))],
        out_shape=[jax.ShapeDtypeStruct((bsz, s, d), F32), jax.ShapeDtypeStruct((bsz, s, d), BF16),
                   jax.ShapeDtypeStruct((bsz, s, LANES), F32),
                   jax.ShapeDtypeStruct((bsz, s // tm, 1, LANES), F32)],
        compiler_params=_cparams(("parallel", "parallel")),
        name="merge",
    )(attn, four, gates, x, wba, wbf, wout, gt1, sh2, sc2, gffn, wr, br)


GRAN = 16
MOE_TOKEN_BLOCK = 128
MOE_ROW_TILE = 1024


def _moe_metadata(cnt, tm):
    ntile = cnt.shape[0]

    def cumsum(a, axis):
        n = a.shape[axis]
        tri = jnp.asarray(np.triu(np.ones((n, n), np.float32)))
        af = a.astype(F32)
        out = (jnp.dot(af, tri, precision=lax.Precision.HIGHEST) if axis == a.ndim - 1
               else jnp.dot(tri.T, af, precision=lax.Precision.HIGHEST))
        return out.astype(jnp.int32)

    c = cnt[:, :N_EXPERTS].astype(jnp.int32)
    pc = (c + GRAN - 1) // GRAN * GRAN
    loff = cumsum(pc, 1) - pc
    tot = jnp.sum(pc, axis=0)
    tot_r = (tot + MOE_ROW_TILE - 1) // MOE_ROW_TILE * MOE_ROW_TILE
    ends = cumsum(tot_r[None, :], 1)[0]
    base = ends - tot_r
    goff = base[None, :] + cumsum(pc, 0) - pc
    cap_tiles = (2 * ntile * tm + ntile * N_EXPERTS * GRAN) // MOE_ROW_TILE + N_EXPERTS
    n_valid = (ends[-1] // MOE_ROW_TILE).astype(jnp.int32)
    tile_start = jnp.arange(cap_tiles, dtype=jnp.int32) * MOE_ROW_TILE
    tile_expert = jnp.minimum(jnp.sum((ends[None, :] <= tile_start[:, None]).astype(jnp.int32), axis=1),
                              N_EXPERTS - 1)
    flat = lambda a: a.reshape(-1).astype(jnp.int32)
    tail = tot_r - tot + jnp.where(jnp.arange(N_EXPERTS) == N_EXPERTS - 1,
                                   cap_tiles * MOE_ROW_TILE - ends[-1], 0)
    meta = dict(loff=flat(loff), goff=flat(goff), ngran=flat(pc // GRAN),
                ntot=flat(jnp.sum(pc, axis=1) // GRAN),
                tail_start=flat(base + tot), tail_n=flat(tail // GRAN),
                tile_expert=tile_expert, n_valid=n_valid.reshape(1))
    return meta, loff.astype(F32), cap_tiles


def _run_copies(i, loff_s, goff_s, ngran_s, make_copy):
    def per_expert(e, n_started):
        n = ngran_s[i * N_EXPERTS + e]
        lo = loff_s[i * N_EXPERTS + e]
        go = goff_s[i * N_EXPERTS + e]

        def per_granule(g, carry):
            make_copy(pl.multiple_of(lo + g * GRAN, GRAN), pl.multiple_of(go + g * GRAN, GRAN)).start()
            return carry

        lax.fori_loop(0, n, per_granule, 0)
        return n_started + n

    return lax.fori_loop(0, N_EXPERTS, per_expert, 0)


def _wait_copies(n, make_copy):
    def one(_, carry):
        make_copy(0, 0).wait()
        return carry
    lax.fori_loop(0, n, one, 0)


def _local_positions(route, loffv, ltri):
    lane = lax.broadcasted_iota(jnp.int32, route.shape, 1).astype(F32)
    o1 = lane == route[:, 0:1]
    o2 = lane == route[:, 1:2]
    before = jnp.dot(ltri, (o1 | o2).astype(BF16), preferred_element_type=F32)
    start = before + loffv
    pos1 = jnp.sum(jnp.where(o1, start, 0.0), axis=-1, keepdims=True)
    pos2 = jnp.sum(jnp.where(o2, start, 0.0), axis=-1, keepdims=True)
    return pos1, pos2


def _dispatch_kernel(loff_s, goff_s, ngran_s, ntot_s, tstart_s, tn_s, v2_ref, route_ref, loffv_ref,
                     ltri_ref, xs_ref, posw_ref, loc_ref, zero_ref, sem):
    i = pl.program_id(0)
    last = pl.num_programs(0) - 1
    slot = i % 2
    rows = loc_ref.shape[1]
    route = route_ref[...]
    pos1, pos2 = _local_positions(route, loffv_ref[0], ltri_ref[...])
    lane = lax.broadcasted_iota(jnp.int32, route.shape, 1)
    posw = jnp.where(lane == 0, pos1, jnp.where(lane == 1, pos2, route))
    posw_ref[...] = posw
    pos_t = posw.T
    riota = lax.broadcasted_iota(jnp.int32, (rows, 1), 0).astype(F32)
    onehot = ((riota == pos_t[0:1]) | (riota == pos_t[1:2])).astype(BF16)
    loc_ref[slot] = jnp.dot(onehot, v2_ref[...], preferred_element_type=F32).astype(BF16)

    def out_copy(s):
        return lambda lo, go: pltpu.make_async_copy(
            loc_ref.at[s, pl.ds(lo, GRAN)], xs_ref.at[pl.ds(go, GRAN)], sem.at[s])

    _run_copies(i, loff_s, goff_s, ngran_s, out_copy(slot))

    @pl.when(i > 0)
    def _():
        _wait_copies(ntot_s[i - 1], out_copy(1 - slot))

    @pl.when(i == last)
    def _():
        _wait_copies(ntot_s[i], out_copy(slot))
        zero_ref[...] = jnp.zeros(zero_ref.shape, BF16)

        def zero_copy(go):
            return pltpu.make_async_copy(zero_ref, xs_ref.at[pl.ds(go, GRAN)], sem.at[0])

        def per_expert(e, n_started):
            def per_granule(g, carry):
                zero_copy(pl.multiple_of(tstart_s[e] + g * GRAN, GRAN)).start()
                return carry
            lax.fori_loop(0, tn_s[e], per_granule, 0)
            return n_started + tn_s[e]

        nz = lax.fori_loop(0, N_EXPERTS, per_expert, 0)
        _wait_copies(nz, lambda lo, go: zero_copy(go))


def _dispatch(v2, route, meta, loffv, ltri, cap_tiles, tm, rows):
    t, d = v2.shape
    ntile = t // tm
    grid_spec = pltpu.PrefetchScalarGridSpec(
        num_scalar_prefetch=6,
        grid=(ntile,),
        in_specs=[pl.BlockSpec((tm, d), lambda i, *_: (i, 0)),
                  pl.BlockSpec((tm, LANES), lambda i, *_: (i, 0)),
                  pl.BlockSpec((1, 1, LANES), lambda i, *_: (i, 0, 0)),
                  pl.BlockSpec((tm, tm), lambda i, *_: (0, 0))],
        out_specs=[pl.BlockSpec(memory_space=pl.ANY),
                   pl.BlockSpec((tm, LANES), lambda i, *_: (i, 0))],
        scratch_shapes=[pltpu.VMEM((2, rows, d), BF16), pltpu.VMEM((GRAN, d), BF16),
                        pltpu.SemaphoreType.DMA((2,))],
    )
    return pl.pallas_call(
        _dispatch_kernel,
        grid_spec=grid_spec,
        out_shape=[jax.ShapeDtypeStruct((cap_tiles * MOE_ROW_TILE, d), BF16),
                   jax.ShapeDtypeStruct((t, LANES), F32)],
        compiler_params=_cparams(("arbitrary",)),
        name="dispatch",
    )(meta["loff"], meta["goff"], meta["ngran"], meta["ntot"], meta["tail_start"], meta["tail_n"],
      v2, route, loffv, ltri)


def _experts_kernel(texp_s, nvalid_s, x_ref, wgu_ref, wd_ref, y_ref):
    del texp_s

    @pl.when(pl.program_id(0) < nvalid_s[0])
    def _():
        hc = jnp.dot(x_ref[...], wgu_ref[0], preferred_element_type=F32)
        g = hc[:, :D_EXPERT]
        hid = (g * jax.nn.sigmoid(g) * hc[:, D_EXPERT:]).astype(BF16)
        y_ref[...] = jnp.dot(hid, wd_ref[0], preferred_element_type=F32).astype(BF16)

    @pl.when(pl.program_id(0) >= nvalid_s[0])
    def _():
        y_ref[...] = jnp.zeros(y_ref.shape, BF16)


def _experts(xs, wgu, wd, meta, cap_tiles):
    d = xs.shape[1]
    row = lambda j, texp, nv: (jnp.minimum(j, nv[0] - 1), 0)
    wsel = lambda j, texp, nv: (texp[jnp.minimum(j, nv[0] - 1)], 0, 0)
    grid_spec = pltpu.PrefetchScalarGridSpec(
        num_scalar_prefetch=2,
        grid=(cap_tiles,),
        in_specs=[pl.BlockSpec((MOE_ROW_TILE, d), row),
                  pl.BlockSpec((1, d, 2 * D_EXPERT), wsel),
                  pl.BlockSpec((1, D_EXPERT, d), wsel)],
        out_specs=pl.BlockSpec((MOE_ROW_TILE, d), lambda j, texp, nv: (j, 0)),
    )
    return pl.pallas_call(
        _experts_kernel,
        grid_spec=grid_spec,
        out_shape=jax.ShapeDtypeStruct(xs.shape, BF16),
        compiler_params=_cparams(("arbitrary",)),
        name="experts",
    )(meta["tile_expert"], meta["n_valid"], xs, wgu, wd)


def _combine_kernel(loff_s, goff_s, ngran_s, ntot_s, posw_ref, ys_ref, h1_ref, gt2_ref, shf_ref,
                    scf_ref, gfin_ref, o_ref, loc_ref, sem):
    i = pl.program_id(0)
    slot = i % 2
    rows = loc_ref.shape[1]

    def in_copy(s):
        return lambda lo, go: pltpu.make_async_copy(
            ys_ref.at[pl.ds(go, GRAN)], loc_ref.at[s, pl.ds(lo, GRAN)], sem.at[s])

    def fetch(tile, s):
        loc_ref[s] = jnp.zeros(loc_ref.shape[1:], BF16)
        _run_copies(tile, loff_s, goff_s, ngran_s, in_copy(s))

    @pl.when(i == 0)
    def _():
        fetch(0, 0)

    @pl.when(i + 1 < pl.num_programs(0))
    def _():
        fetch(i + 1, 1 - slot)

    _wait_copies(ntot_s[i], in_copy(slot))

    riota = lax.broadcasted_iota(jnp.int32, (1, rows), 1).astype(F32)
    for lo in range(0, posw_ref.shape[0], MOE_TOKEN_BLOCK):
        blk = slice(lo, lo + MOE_TOKEN_BLOCK)
        posw = posw_ref[blk, :]
        qw = (jnp.where(riota == posw[:, 0:1], posw[:, 2:3], 0.0)
              + jnp.where(riota == posw[:, 1:2], posw[:, 3:4], 0.0)).astype(BF16)
        moe = jnp.dot(qw, loc_ref[slot], preferred_element_type=F32)
        h2 = h1_ref[blk, :] + gt2_ref[0] * moe
        o_ref[blk, :] = _rms(h2, gfin_ref[...]) * (1.0 + scf_ref[0]) + shf_ref[0]


def _combine(posw, ys, h1, gt2, shf, scf, gfin, meta, tm, rows, tiles_per_batch):
    t, d = h1.shape
    assert tm % MOE_TOKEN_BLOCK == 0
    per_b = pl.BlockSpec((1, 1, d), lambda i, *_: (i // tiles_per_batch, 0, 0))
    grid_spec = pltpu.PrefetchScalarGridSpec(
        num_scalar_prefetch=4,
        grid=(t // tm,),
        in_specs=[pl.BlockSpec((tm, LANES), lambda i, *_: (i, 0)),
                  pl.BlockSpec(memory_space=pl.ANY),
                  pl.BlockSpec((tm, d), lambda i, *_: (i, 0)),
                  per_b, per_b, per_b, pl.BlockSpec((1, d), lambda i, *_: (0, 0))],
        out_specs=pl.BlockSpec((tm, d), lambda i, *_: (i, 0)),
        scratch_shapes=[pltpu.VMEM((2, rows, d), BF16), pltpu.SemaphoreType.DMA((2,))],
    )
    return pl.pallas_call(
        _combine_kernel,
        grid_spec=grid_spec,
        out_shape=jax.ShapeDtypeStruct((t, d), F32),
        compiler_params=_cparams(("arbitrary",)),
        name="combine",
    )(meta["loff"], meta["goff"], meta["ngran"], meta["ntot"], posw, ys, h1, gt2, shf, scf, gfin)


def _rot_cols(w):
    half = ROPE // 2
    return jnp.concatenate([-w[..., half:], w[..., :half]], axis=-1)


def _pad_cols(w, lo, width):
    return jnp.pad(w, [(0, 0)] * (w.ndim - 1) + [(lo, width - lo - w.shape[-1])])


def _prep_weights(w_in, w_uq, w_uk, w_uv):
    o2, o3 = Q_LORA + KV_LORA, Q_LORA + KV_LORA + ROPE
    w_kr = w_in[:, o2:o3]
    wcat = jnp.concatenate([
        w_in[:, :o2],
        _pad_cols(w_kr, NOPE, HEAD_PAD), _pad_cols(_rot_cols(w_kr), NOPE, HEAD_PAD),
        w_in[:, o3:]], axis=1).astype(BF16)
    wq_a = _pad_cols(w_uq, 0, HEAD_PAD)
    wq_b = _pad_cols(_rot_cols(w_uq[..., NOPE:]), NOPE, HEAD_PAD)
    r = w_uq.shape[0]
    wqt = jnp.concatenate([wq_a.reshape(r, -1), wq_b.reshape(r, -1)], axis=1).T.astype(BF16)
    wk = _pad_cols(w_uk, 0, HEAD_PAD).reshape(r, -1).astype(BF16)
    wvt = _pad_cols(w_uv, 0, V_ROWS).reshape(r, -1).T.astype(BF16)
    ones = np.zeros((N_HEADS, V_ROWS), np.float32)
    ones[:, V_DIM] = 1.0
    return wcat, wqt, wk, wvt, jnp.asarray(ones.reshape(-1, 1))


def kernel(x, c, positions, w_ada, b_ada, g_norm_mix, w_in, g_q_lat, g_kv_lat, w_uq, w_uk, w_uv, w_branch_attn, w_branch_fourier, b_gates, w_out, g_norm_ffn, w_router_group, b_router_group, w_router_expert, b_router_expert, w_expert_gate, w_expert_up, w_expert_down, w_ada_final, b_ada_final, g_norm_final):
    bsz, s, d = x.shape
    assert w_ada.shape[0] == 1 and s % FFT_N2 == 0
    tm = min(512, s)

    mod = _mods(c, w_ada[0], b_ada[0])
    fmod = _mods(c, w_ada_final, b_ada_final)
    sh1, sc1, gt1, sh2, sc2, gt2 = [m.reshape(bsz, 1, d) for m in jnp.split(mod, 6, axis=-1)]
    shf, scf = [m.reshape(bsz, 1, d) for m in jnp.split(fmod, 2, axis=-1)]

    wcat, wqt, wk, wvt, vones = _prep_weights(w_in[0], w_uq[0], w_uk[0], w_uv[0])
    bd, m1, wb = _dft_constants(s)
    inv_freq = ROPE_THETA ** (-jnp.arange(0, ROPE, 2, dtype=F32) / ROPE)
    invf = jnp.concatenate([inv_freq, inv_freq]).reshape(ROPE, 1)

    qt, k, vt, zr, zi, gates = _inproj(
        x, positions.reshape(bsz, 1, s), sh1, sc1, g_norm_mix[0][None, :], wcat,
        g_q_lat[0][None, :], g_kv_lat[0][None, :], wqt, wk, wvt, vones, bd, b_gates[0][None, :],
        invf, tm)

    attn = _attention(qt, k, vt, tq=min(256, s))
    br, bi = _fft_a(zr, zi, m1, ts2=8)
    four = _fft_b(br, bi, wb, tk1=16)

    wr = jnp.concatenate([w_router_expert[0].reshape(d, N_EXPERTS), w_router_group[0]], axis=1)
    wr = _pad_cols(wr, 0, LANES)
    wr_top = lax.bitcast_convert_type(
        lax.bitcast_convert_type(wr, jnp.uint32) & jnp.uint32(0xFFFF0000), F32)
    wr_hi = wr_top.astype(BF16)
    wr_lo = (wr - wr_top).astype(BF16)
    wr = jnp.concatenate([jnp.concatenate([wr_hi, wr_lo], axis=1),
                          jnp.concatenate([wr_hi, jnp.zeros_like(wr_lo)], axis=1)], axis=0)
    brt = _pad_cols(jnp.concatenate([b_router_expert[0].reshape(-1), b_router_group[0]])[None, :], 0, LANES)
    h1, v2, route, cnt = _merge(attn, four, gates, x, w_branch_attn[0].astype(BF16),
                                w_branch_fourier[0].astype(BF16), w_out[0].astype(BF16), gt1, sh2,
                                sc2, g_norm_ffn[0][None, :], wr, brt, tm)

    t = bsz * s
    meta, loffv, cap_tiles = _moe_metadata(cnt.reshape(-1, LANES), tm)
    rows = 2 * tm + N_EXPERTS * GRAN
    loffv = _pad_cols(loffv, 0, LANES).reshape(-1, 1, LANES)
    ltri = jnp.asarray(np.tril(np.ones((tm, tm), np.float32), -1), BF16)
    xs, posw = _dispatch(v2.reshape(t, d), route.reshape(t, LANES), meta, loffv, ltri, cap_tiles,
                         tm, rows)
    wgu = jnp.concatenate([w_expert_gate[0], w_expert_up[0]], axis=-1).astype(BF16)
    wd = w_expert_down[0].astype(BF16)
    ys = _experts(xs, wgu, wd, meta, cap_tiles)
    out = _combine(posw, ys, h1.reshape(t, d), gt2, shf, scf, g_norm_final[None, :], meta, tm, rows,
                   s // tm)
    return out.reshape(bsz, s, d)
```

```python
import functools

import numpy as np
import jax
import jax.numpy as jnp
from jax import lax
from jax.experimental import pallas as pl
from jax.experimental.pallas import tpu as pltpu

F32 = jnp.float32
BF16 = jnp.bfloat16

N_HEADS = 8
NOPE = 64
ROPE = 32
V_DIM = 64
HEAD_PAD = 128
V_ROWS = 96
Q_LORA = 256
KV_LORA = 256
ROPE_THETA = 10000.0
F_GROUPS = 8
F_GDIM = 64
F_WIDTH = F_GROUPS * F_GDIM
N_GROUPS = 4
E_PER_GROUP = 8
N_EXPERTS = N_GROUPS * E_PER_GROUP
D_EXPERT = 256
EPS = 1e-6
FFT_N2 = 64
LANES = 128
MERGE_ROWS = 128
ATTN_DEPTH = 1
QK_ROWS = 256
VMEM_LIMIT = 56 * 1024 * 1024


def _cparams(sem):
    return pltpu.CompilerParams(dimension_semantics=sem, vmem_limit_bytes=VMEM_LIMIT)


def _mods_kernel(c_ref, w_ref, b_ref, o_ref):
    c = c_ref[...]
    ca = c * jax.nn.sigmoid(c)
    o_ref[...] = jnp.dot(ca, w_ref[...], preferred_element_type=F32,
                         precision=lax.Precision.HIGHEST) + b_ref[...]


def _mods(c, w, b):
    bsz, d = c.shape
    n = w.shape[1]
    tn = 1024
    return pl.pallas_call(
        _mods_kernel,
        grid=(n // tn,),
        in_specs=[pl.BlockSpec((bsz, d), lambda j: (0, 0)),
                  pl.BlockSpec((d, tn), lambda j: (0, j)),
                  pl.BlockSpec((1, tn), lambda j: (0, j))],
        out_specs=pl.BlockSpec((bsz, tn), lambda j: (0, j)),
        out_shape=jax.ShapeDtypeStruct((bsz, n), F32),
        compiler_params=_cparams(("arbitrary",)),
        name="mods",
    )(c, w, b.reshape(1, n))


def _rms(x, g):
    r = lax.rsqrt(jnp.mean(x * x, axis=-1, keepdims=True) + EPS)
    return x * r * g


def _inproj_kernel(x_ref, pos_ref, sh_ref, sc_ref, gmix_ref, wcat_ref, gq_ref, gkv_ref,
                   wqt_ref, wk_ref, wvt_ref, vones_ref, bd_ref, bg_ref, invf_ref,
                   qt_ref, k_ref, vt_ref, zr_ref, zi_ref, gate_ref, *, scale):
    hp = N_HEADS * HEAD_PAD
    nt_dims = (((1,), (1,)), ((), ()))
    u = _rms(x_ref[0], gmix_ref[...]) * (1.0 + sc_ref[0]) + sh_ref[0]
    ub = u.astype(BF16)

    o_q, o_kv, o_kr, o_f, o_g = 0, Q_LORA, Q_LORA + KV_LORA, Q_LORA + KV_LORA + 2 * HEAD_PAD, \
        Q_LORA + KV_LORA + 2 * HEAD_PAD + F_WIDTH

    tm = pos_ref.shape[-1]
    ang = invf_ref[...] * pos_ref[0].astype(F32)
    tail = jnp.zeros((HEAD_PAD - NOPE - ROPE, tm), F32)
    cos_t = jnp.concatenate([jnp.ones((NOPE, tm), F32), jnp.cos(ang), tail], axis=0)
    sin_t = jnp.concatenate([jnp.zeros((NOPE, tm), F32), jnp.sin(ang), tail], axis=0)

    ql = jnp.dot(ub, wcat_ref[:, o_q:o_kv], preferred_element_type=F32)
    kvl = jnp.dot(ub, wcat_ref[:, o_kv:o_kr], preferred_element_type=F32)
    kr = jnp.dot(ub, wcat_ref[:, o_kr:o_f], preferred_element_type=F32)
    fin = jnp.dot(ub, wcat_ref[:, o_f:o_g], preferred_element_type=F32).astype(BF16)
    zg = jnp.dot(ub, wcat_ref[:, o_g:], preferred_element_type=F32) + bg_ref[...]

    qn = _rms(ql, gq_ref[...]).astype(BF16)
    kvn = _rms(kvl, gkv_ref[...]).astype(BF16)
    zz = jnp.dot(fin, bd_ref[...], preferred_element_type=F32)
    qab_t = lax.dot_general(wqt_ref[...], qn, nt_dims, preferred_element_type=F32)
    kn = jnp.dot(kvn, wk_ref[...], preferred_element_type=F32)
    vt = lax.dot_general(wvt_ref[...], kvn, nt_dims, preferred_element_type=F32) + vones_ref[...]

    gate_ref[0] = jax.nn.sigmoid(zg).astype(BF16)
    zr_ref[0] = zz[:, :F_WIDTH].astype(BF16)
    zi_ref[0] = zz[:, F_WIDTH:].astype(BF16)
    cq = cos_t * scale
    sq = sin_t * scale
    for h in range(N_HEADS):
        a = qab_t[h * HEAD_PAD:(h + 1) * HEAD_PAD]
        b = qab_t[hp + h * HEAD_PAD:hp + (h + 1) * HEAD_PAD]
        qt_ref[0, h] = (a * cq + b * sq).astype(BF16)
    krr = kr[:, :HEAD_PAD] * cos_t.T + kr[:, HEAD_PAD:] * sin_t.T
    for h in range(N_HEADS):
        k_ref[0, h] = (kn[:, h * HEAD_PAD:(h + 1) * HEAD_PAD] + krr).astype(BF16)
        vt_ref[0, h, 0] = vt[h * V_ROWS:(h + 1) * V_ROWS].astype(BF16)


def _inproj(x, pos_row, sh1, sc1, gmix, wcat, gq, gkv, wqt, wk, wvt, vones, bd, bg, invf, tm):
    bsz, s, d = x.shape
    hp = N_HEADS * HEAD_PAD
    nt = s // tm
    const = lambda shape: pl.BlockSpec(shape, lambda b, i: (0,) * len(shape))
    tok = lambda w: pl.BlockSpec((1, tm, w), lambda b, i: (b, i, 0))
    per_b = pl.BlockSpec((1, 1, d), lambda b, i: (b, 0, 0))
    scale = float((NOPE + ROPE) ** -0.5 * np.log2(np.e))
    return pl.pallas_call(
        functools.partial(_inproj_kernel, scale=scale),
        grid=(bsz, nt),
        in_specs=[tok(d), pl.BlockSpec((1, 1, tm), lambda b, i: (b, 0, i)), per_b, per_b,
                  const((1, d)), const(wcat.shape), const((1, Q_LORA)), const((1, KV_LORA)),
                  const(wqt.shape), const(wk.shape), const(wvt.shape), const(vones.shape),
                  const(bd.shape), const((1, bg.shape[1])), const((ROPE, 1))],
        out_specs=[pl.BlockSpec((1, N_HEADS, HEAD_PAD, tm), lambda b, i: (b, 0, 0, i)),
                   pl.BlockSpec((1, N_HEADS, tm, HEAD_PAD), lambda b, i: (b, 0, i, 0)),
                   pl.BlockSpec((1, N_HEADS, 1, V_ROWS, tm), lambda b, i: (b, 0, i, 0, 0)),
                   tok(F_WIDTH), tok(F_WIDTH), tok(2 * d)],
        out_shape=[jax.ShapeDtypeStruct((bsz, N_HEADS, HEAD_PAD, s), BF16),
                   jax.ShapeDtypeStruct((bsz, N_HEADS, s, HEAD_PAD), BF16),
                   jax.ShapeDtypeStruct((bsz, N_HEADS, nt, V_ROWS, tm), BF16),
                   jax.ShapeDtypeStruct((bsz, s, F_WIDTH), BF16),
                   jax.ShapeDtypeStruct((bsz, s, F_WIDTH), BF16),
                   jax.ShapeDtypeStruct((bsz, s, 2 * d), BF16)],
        compiler_params=_cparams(("parallel", "parallel")),
        name="inproj",
    )(x, pos_row, sh1, sc1, gmix, wcat, gq, gkv, wqt, wk, wvt, vones, bd, bg, invf)


def _attn_kernel(qt_ref, k_ref, vt_ref, o_ref, *p_refs, nk):
    tq = qt_ref.shape[3]
    tk = vt_ref.shape[4]
    qts = [qt_ref[0, hh] for hh in range(2)]
    nbuf = ATTN_DEPTH + 1

    def scores(c, hh):
        ss = []
        for i in range(tk // QK_ROWS):
            off = c * tk + i * QK_ROWS
            ss.append(jnp.dot(k_ref[0, hh, off:off + QK_ROWS, :], qts[hh],
                              preferred_element_type=F32))
        return ss

    def probs(ss, m, p_ref):
        m_new = m
        for s in ss:
            m_new = jnp.maximum(m_new, jnp.max(s, axis=0, keepdims=True))
        for i, s in enumerate(ss):
            p_ref[i * QK_ROWS:(i + 1) * QK_ROWS, :] = jnp.exp2(s - m_new).astype(BF16)
        return m_new, jnp.exp2(m - m_new)

    def accumulate(c, hh, acc, alpha, p_ref):
        return alpha * acc + jnp.dot(vt_ref[0, hh, c], p_ref[...], preferred_element_type=F32)

    hs = range(2)
    buf = lambda c, hh: p_refs[hh * nbuf + c % nbuf]
    m = [jnp.full((1, tq), -jnp.inf, F32) for _ in hs]
    alpha = [[None] * nk for _ in hs]
    accs = [jnp.zeros((V_ROWS, tq), F32) for _ in hs]
    for c in range(nk + ATTN_DEPTH):
        if c < nk:
            s = [scores(c, hh) for hh in hs]
        if c >= ATTN_DEPTH:
            for hh in hs:
                accs[hh] = accumulate(c - ATTN_DEPTH, hh, accs[hh], alpha[hh][c - ATTN_DEPTH],
                                      buf(c - ATTN_DEPTH, hh))
        if c < nk:
            for hh in hs:
                m[hh], alpha[hh][c] = probs(s[hh], m[hh], buf(c, hh))
    out_t = jnp.concatenate([a[:V_DIM] / a[V_DIM:V_DIM + 1] for a in accs], axis=0)
    o_ref[0] = out_t.T.astype(BF16)


def _attention(qt, k, vt, tq):
    bsz, nh, s, _ = k.shape
    nk, tk = vt.shape[2], vt.shape[4]
    return pl.pallas_call(
        functools.partial(_attn_kernel, nk=nk),
        grid=(bsz, nh // 2, s // tq),
        in_specs=[pl.BlockSpec((1, 2, HEAD_PAD, tq), lambda b, j, i: (b, j, 0, i)),
                  pl.BlockSpec((1, 2, s, HEAD_PAD), lambda b, j, i: (b, j, 0, 0)),
                  pl.BlockSpec((1, 2, nk, V_ROWS, tk), lambda b, j, i: (b, j, 0, 0, 0))],
        out_specs=pl.BlockSpec((1, tq, HEAD_PAD), lambda b, j, i: (b, i, j)),
        out_shape=jax.ShapeDtypeStruct((bsz, s, nh * V_DIM), BF16),
        scratch_shapes=[pltpu.VMEM((tk, tq), BF16)] * (2 * (ATTN_DEPTH + 1)),
        compiler_params=_cparams(("parallel", "parallel", "arbitrary")),
        name="attention",
    )(qt, k, vt)


def _fft_a_kernel(zr_ref, zi_ref, m1_ref, br_ref, bi_ref):
    n1 = zr_ref.shape[1]
    ts2 = m1_ref.shape[0]
    for j in range(ts2):
        sl = slice(j * F_WIDTH, (j + 1) * F_WIDTH)
        xx = jnp.concatenate([zr_ref[0, :, sl], zi_ref[0, :, sl]], axis=0)
        y = jnp.dot(m1_ref[j], xx, preferred_element_type=F32)
        br_ref[0, :, sl] = y[:n1].astype(BF16)
        bi_ref[0, :, sl] = y[n1:].astype(BF16)


def _fft_a(zr, zi, m1, ts2):
    bsz, s, _ = zr.shape
    n1 = s // FFT_N2
    zr2 = zr.reshape(bsz, n1, FFT_N2 * F_WIDTH)
    zi2 = zi.reshape(bsz, n1, FFT_N2 * F_WIDTH)
    blk = pl.BlockSpec((1, n1, ts2 * F_WIDTH), lambda b, j: (b, 0, j))
    return pl.pallas_call(
        _fft_a_kernel,
        grid=(bsz, FFT_N2 // ts2),
        in_specs=[blk, blk, pl.BlockSpec((ts2, 2 * n1, 2 * n1), lambda b, j: (j, 0, 0))],
        out_specs=[blk, blk],
        out_shape=[jax.ShapeDtypeStruct(zr2.shape, BF16)] * 2,
        compiler_params=_cparams(("parallel", "parallel")),
        name="fft_a",
    )(zr2, zi2, m1)


def _fft_b_kernel(br_ref, bi_ref, wb_ref, o_ref):
    tk1 = br_ref.shape[1]
    for j in range(tk1):
        xx = jnp.concatenate([br_ref[0, j], bi_ref[0, j]], axis=0)
        y = jnp.dot(wb_ref[...], xx, preferred_element_type=F32)
        o_ref[0, :, j * F_WIDTH:(j + 1) * F_WIDTH] = y.astype(BF16)


def _fft_b(br, bi, wb, tk1):
    bsz, n1, _ = br.shape
    br4 = br.reshape(bsz, n1, FFT_N2, F_WIDTH)
    bi4 = bi.reshape(bsz, n1, FFT_N2, F_WIDTH)
    blk = pl.BlockSpec((1, tk1, FFT_N2, F_WIDTH), lambda b, j: (b, j, 0, 0))
    out = pl.pallas_call(
        _fft_b_kernel,
        grid=(bsz, n1 // tk1),
        in_specs=[blk, blk, pl.BlockSpec((FFT_N2, 2 * FFT_N2), lambda b, j: (0, 0))],
        out_specs=pl.BlockSpec((1, FFT_N2, tk1 * F_WIDTH), lambda b, j: (b, 0, j)),
        out_shape=jax.ShapeDtypeStruct((bsz, FFT_N2, n1 * F_WIDTH), BF16),
        compiler_params=_cparams(("parallel", "parallel")),
        name="fft_b",
    )(br4, bi4, wb)
    return out.reshape(bsz, n1 * FFT_N2, F_WIDTH)


def _dft_constants(s):
    n1, n2 = s // FFT_N2, FFT_N2
    m = np.arange(F_GDIM)
    th = 2.0 * np.pi * np.outer(m, m) / F_GDIM
    eye = np.eye(F_GROUPS)
    bd = np.concatenate([np.kron(eye, np.cos(th)), -np.kron(eye, np.sin(th))], axis=1) / np.sqrt(F_GDIM)
    k1 = np.arange(n1)[None, :, None]
    s1 = np.arange(n1)[None, None, :]
    s2 = np.arange(n2)[:, None, None]
    ph = 2.0 * np.pi * ((k1 * (n2 * s1 + s2)) % s) / s
    cr, ci = np.cos(ph) / np.sqrt(n1), -np.sin(ph) / np.sqrt(n1)
    m1 = np.concatenate([np.concatenate([cr, -ci], axis=2), np.concatenate([ci, cr], axis=2)], axis=1)
    k2 = np.arange(n2)
    tb = 2.0 * np.pi * np.outer(k2, k2) / n2
    wb = np.concatenate([np.cos(tb), np.sin(tb)], axis=1) / np.sqrt(n2)
    return (jnp.asarray(bd, BF16), jnp.asarray(m1, BF16), jnp.asarray(wb, BF16))


def _route(parts, bias):
    logit = parts[:, :LANES] + parts[:, LANES:] + bias
    lane = lax.broadcasted_iota(jnp.int32, logit.shape, 1)
    neg = -jnp.inf
    big = jnp.int32(LANES)

    def softmax_masked(mask):
        z = jnp.where(mask, logit, neg)
        e = jnp.exp(z - jnp.max(z, axis=-1, keepdims=True))
        return e / jnp.sum(e, axis=-1, keepdims=True)

    def top1(p, mask):
        pm = jnp.where(mask, p, -1.0)
        best = jnp.max(pm, axis=-1, keepdims=True)
        idx = jnp.min(jnp.where(pm == best, lane, big), axis=-1, keepdims=True)
        return best, idx

    gmask = (lane >= N_EXPERTS) & (lane < N_EXPERTS + N_GROUPS)
    g_prob = softmax_masked(gmask)
    g_p, g_lane = top1(g_prob, gmask)
    e_lo = (g_lane - N_EXPERTS) * E_PER_GROUP
    emask = (lane >= e_lo) & (lane < e_lo + E_PER_GROUP)
    e_prob = softmax_masked(emask)
    p1, i1 = top1(e_prob, emask)
    p2, i2 = top1(e_prob, emask & (lane != i1))
    den = p1 + p2
    w1 = g_p * (p1 / den)
    w2 = g_p * (p2 / den)
    route = jnp.where(lane == 0, i1.astype(F32), jnp.where(lane == 1, i2.astype(F32),
                      jnp.where(lane == 2, w1, jnp.where(lane == 3, w2, 0.0))))
    cnt = jnp.sum(((lane == i1) | (lane == i2)).astype(F32), axis=0, keepdims=True)
    return route, cnt


def _merge_kernel(at_ref, fo_ref, gate_ref, x_ref, wba_ref, wbf_ref, wout_ref, gt1_ref, sh2_ref,
                  sc2_ref, gffn_ref, wr_ref, br_ref, h1_ref, v2_ref, route_ref, cnt_ref):
    d = x_ref.shape[-1]
    tm = x_ref.shape[1]
    blocks = [slice(i * MERGE_ROWS, (i + 1) * MERGE_ROWS) for i in range(tm // MERGE_ROWS)]
    a = [jnp.dot(at_ref[0, r, :], wba_ref[...], preferred_element_type=F32) for r in blocks]
    fo = [jnp.dot(fo_ref[0, r, :], wbf_ref[...], preferred_element_type=F32) for r in blocks]
    y = []
    for r, ai, fi in zip(blocks, a, fo):
        merged = gate_ref[0, r, :d].astype(F32) * ai + gate_ref[0, r, d:].astype(F32) * fi
        y.append(jnp.dot(merged.astype(BF16), wout_ref[...], preferred_element_type=F32))
    parts = []
    for r, yi in zip(blocks, y):
        h1 = x_ref[0, r, :] + gt1_ref[0] * yi
        h1_ref[0, r, :] = h1
        v2 = _rms(h1, gffn_ref[...]) * (1.0 + sc2_ref[0]) + sh2_ref[0]
        v_hi = v2.astype(BF16)
        v2_ref[0, r, :] = v_hi
        v_lo = (v2 - v_hi.astype(F32)).astype(BF16)
        parts.append(jnp.dot(jnp.concatenate([v_hi, v_lo], axis=1), wr_ref[...],
                             preferred_element_type=F32))
    cnt = None
    for r, pi in zip(blocks, parts):
        route, c = _route(pi, br_ref[...])
        route_ref[0, r, :] = route
        cnt = c if cnt is None else cnt + c
    cnt_ref[0, 0] = cnt


def _merge(attn, four, gates, x, wba, wbf, wout, gt1, sh2, sc2, gffn, wr, br, tm):
    bsz, s, d = x.shape
    const = lambda shape: pl.BlockSpec(shape, lambda b, i: (0,) * len(shape))
    tok = lambda w: pl.BlockSpec((1, tm, w), lambda b, i: (b, i, 0))
    per_b = pl.BlockSpec((1, 1, d), lambda b, i: (b, 0, 0))
    return pl.pallas_call(
        _merge_kernel,
        grid=(bsz, s // tm),
        in_specs=[tok(attn.shape[-1]), tok(F_WIDTH), tok(2 * d), tok(d), const(wba.shape),
                  const(wbf.shape), const(wout.shape), per_b, per_b, per_b, const((1, d)),
                  const(wr.shape), const((1, LANES))],
        out_specs=[tok(d), tok(d), tok(LANES),
                   pl.BlockSpec((1, 1, 1, LANES), lambda b, i: (b, i, 0, 0))],
        out_shape=[jax.ShapeDtypeStruct((bsz, s, d), F32), jax.ShapeDtypeStruct((bsz, s, d), BF16),
                   jax.ShapeDtypeStruct((bsz, s, LANES), F32),
                   jax.ShapeDtypeStruct((bsz, s // tm, 1, LANES), F32)],
        compiler_params=_cparams(("parallel", "parallel")),
        name="merge",
    )(attn, four, gates, x, wba, wbf, wout, gt1, sh2, sc2, gffn, wr, br)


GRAN = 16
MOE_ROW_TILE = 1024


def _moe_metadata(cnt, tm):
    ntile = cnt.shape[0]

    def cumsum(a, axis):
        n = a.shape[axis]
        tri = jnp.asarray(np.triu(np.ones((n, n), np.float32)))
        af = a.astype(F32)
        out = (jnp.dot(af, tri, precision=lax.Precision.HIGHEST) if axis == a.ndim - 1
               else jnp.dot(tri.T, af, precision=lax.Precision.HIGHEST))
        return out.astype(jnp.int32)

    c = cnt[:, :N_EXPERTS].astype(jnp.int32)
    pc = (c + GRAN - 1) // GRAN * GRAN
    loff = cumsum(pc, 1) - pc
    tot = jnp.sum(pc, axis=0)
    tot_r = (tot + MOE_ROW_TILE - 1) // MOE_ROW_TILE * MOE_ROW_TILE
    ends = cumsum(tot_r[None, :], 1)[0]
    base = ends - tot_r
    goff = base[None, :] + cumsum(pc, 0) - pc
    cap_tiles = (2 * ntile * tm + ntile * N_EXPERTS * GRAN) // MOE_ROW_TILE + N_EXPERTS
    n_valid = (ends[-1] // MOE_ROW_TILE).astype(jnp.int32)
    tile_start = jnp.arange(cap_tiles, dtype=jnp.int32) * MOE_ROW_TILE
    tile_expert = jnp.minimum(jnp.sum((ends[None, :] <= tile_start[:, None]).astype(jnp.int32), axis=1),
                              N_EXPERTS - 1)
    flat = lambda a: a.reshape(-1).astype(jnp.int32)
    tail = tot_r - tot + jnp.where(jnp.arange(N_EXPERTS) == N_EXPERTS - 1,
                                   cap_tiles * MOE_ROW_TILE - ends[-1], 0)
    meta = dict(loff=flat(loff), goff=flat(goff), ngran=flat(pc // GRAN),
                ntot=jnp.concatenate([flat(jnp.sum(pc // (2 * GRAN), axis=1)),
                                      flat(jnp.sum(pc // GRAN % 2, axis=1))]),
                tail_start=flat(base + tot), tail_n=flat(tail // GRAN),
                tile_expert=tile_expert, n_valid=n_valid.reshape(1))
    return meta, loff.astype(F32), cap_tiles


def _run_copies(i, loff_s, goff_s, ngran_s, make_copy):
    def per_expert(e, carry):
        n = ngran_s[i * N_EXPERTS + e]
        lo = loff_s[i * N_EXPERTS + e]
        go = goff_s[i * N_EXPERTS + e]

        def per_pair(g, c):
            make_copy(pl.multiple_of(lo + g * 2 * GRAN, GRAN),
                      pl.multiple_of(go + g * 2 * GRAN, GRAN), 2).start()
            return c

        lax.fori_loop(0, n // 2, per_pair, 0)

        @pl.when(n % 2 == 1)
        def _():
            make_copy(pl.multiple_of(lo + (n - 1) * GRAN, GRAN),
                      pl.multiple_of(go + (n - 1) * GRAN, GRAN), 1).start()

        return carry

    lax.fori_loop(0, N_EXPERTS, per_expert, 0)


def _wait_copies(n_pairs, n_singles, make_copy):
    def wait(k):
        def one(_, carry):
            make_copy(0, 0, k).wait()
            return carry
        return one
    lax.fori_loop(0, n_pairs, wait(2), 0)
    lax.fori_loop(0, n_singles, wait(1), 0)


def _local_positions(route, loffv, ltri):
    lane = lax.broadcasted_iota(jnp.int32, route.shape, 1).astype(F32)
    o1 = lane == route[:, 0:1]
    o2 = lane == route[:, 1:2]
    before = jnp.dot(ltri, (o1 | o2).astype(BF16), preferred_element_type=F32)
    start = before + loffv
    pos1 = jnp.sum(jnp.where(o1, start, 0.0), axis=-1, keepdims=True)
    pos2 = jnp.sum(jnp.where(o2, start, 0.0), axis=-1, keepdims=True)
    return pos1, pos2


def _dispatch_kernel(loff_s, goff_s, ngran_s, ntot_s, tstart_s, tn_s, v2_ref, route_ref, loffv_ref,
                     ltri_ref, xs_ref, posw_ref, loc_ref, zero_ref, sem):
    i = pl.program_id(0)
    last = pl.num_programs(0) - 1
    slot = i % 2
    rows = loc_ref.shape[1]
    route = route_ref[...]
    pos1, pos2 = _local_positions(route, loffv_ref[0], ltri_ref[...])
    lane = lax.broadcasted_iota(jnp.int32, route.shape, 1)
    posw = jnp.where(lane == 0, pos1, jnp.where(lane == 1, pos2, route))
    posw_ref[...] = posw
    pos_t = posw.T
    riota = lax.broadcasted_iota(jnp.int32, (rows, 1), 0).astype(F32)
    onehot = ((riota == pos_t[0:1]) | (riota == pos_t[1:2])).astype(BF16)
    loc_ref[slot] = jnp.dot(onehot, v2_ref[...], preferred_element_type=F32).astype(BF16)

    def out_copy(s):
        return lambda lo, go, k: pltpu.make_async_copy(
            loc_ref.at[s, pl.ds(lo, k * GRAN)], xs_ref.at[pl.ds(go, k * GRAN)], sem.at[s])

    _run_copies(i, loff_s, goff_s, ngran_s, out_copy(slot))
    ntile = pl.num_programs(0)

    @pl.when(i > 0)
    def _():
        _wait_copies(ntot_s[i - 1], ntot_s[ntile + i - 1], out_copy(1 - slot))

    @pl.when(i == last)
    def _():
        _wait_copies(ntot_s[i], ntot_s[ntile + i], out_copy(slot))
        zero_ref[...] = jnp.zeros(zero_ref.shape, BF16)

        def zero_copy(go):
            return pltpu.make_async_copy(zero_ref, xs_ref.at[pl.ds(go, GRAN)], sem.at[0])

        def per_expert(e, n_started):
            def per_granule(g, carry):
                zero_copy(pl.multiple_of(tstart_s[e] + g * GRAN, GRAN)).start()
                return carry
            lax.fori_loop(0, tn_s[e], per_granule, 0)
            return n_started + tn_s[e]

        nz = lax.fori_loop(0, N_EXPERTS, per_expert, 0)
        _wait_copies(0, nz, lambda lo, go, k: zero_copy(go))


def _dispatch(v2, route, meta, loffv, ltri, cap_tiles, tm, rows):
    t, d = v2.shape
    ntile = t // tm
    grid_spec = pltpu.PrefetchScalarGridSpec(
        num_scalar_prefetch=6,
        grid=(ntile,),
        in_specs=[pl.BlockSpec((tm, d), lambda i, *_: (i, 0)),
                  pl.BlockSpec((tm, LANES), lambda i, *_: (i, 0)),
                  pl.BlockSpec((1, 1, LANES), lambda i, *_: (i, 0, 0)),
                  pl.BlockSpec((tm, tm), lambda i, *_: (0, 0))],
        out_specs=[pl.BlockSpec(memory_space=pl.ANY),
                   pl.BlockSpec((tm, LANES), lambda i, *_: (i, 0))],
        scratch_shapes=[pltpu.VMEM((2, rows, d), BF16), pltpu.VMEM((GRAN, d), BF16),
                        pltpu.SemaphoreType.DMA((2,))],
    )
    return pl.pallas_call(
        _dispatch_kernel,
        grid_spec=grid_spec,
        out_shape=[jax.ShapeDtypeStruct((cap_tiles * MOE_ROW_TILE, d), BF16),
                   jax.ShapeDtypeStruct((t, LANES), F32)],
        compiler_params=_cparams(("arbitrary",)),
        name="dispatch",
    )(meta["loff"], meta["goff"], meta["ngran"], meta["ntot"], meta["tail_start"], meta["tail_n"],
      v2, route, loffv, ltri)


def _experts_kernel(texp_s, nvalid_s, x_ref, wgu_ref, wd_ref, y_ref):
    del texp_s

    @pl.when(pl.program_id(0) < nvalid_s[0])
    def _():
        hc = jnp.dot(x_ref[...], wgu_ref[0], preferred_element_type=F32)
        g = hc[:, :D_EXPERT]
        hid = (g * jax.nn.sigmoid(g) * hc[:, D_EXPERT:]).astype(BF16)
        y_ref[...] = jnp.dot(hid, wd_ref[0], preferred_element_type=F32).astype(BF16)

    @pl.when(pl.program_id(0) >= nvalid_s[0])
    def _():
        y_ref[...] = jnp.zeros(y_ref.shape, BF16)


def _experts(xs, wgu, wd, meta, cap_tiles):
    d = xs.shape[1]
    row = lambda j, texp, nv: (jnp.minimum(j, nv[0] - 1), 0)
    wsel = lambda j, texp, nv: (texp[jnp.minimum(j, nv[0] - 1)], 0, 0)
    grid_spec = pltpu.PrefetchScalarGridSpec(
        num_scalar_prefetch=2,
        grid=(cap_tiles,),
        in_specs=[pl.BlockSpec((MOE_ROW_TILE, d), row),
                  pl.BlockSpec((1, d, 2 * D_EXPERT), wsel),
                  pl.BlockSpec((1, D_EXPERT, d), wsel)],
        out_specs=pl.BlockSpec((MOE_ROW_TILE, d), lambda j, texp, nv: (j, 0)),
    )
    return pl.pallas_call(
        _experts_kernel,
        grid_spec=grid_spec,
        out_shape=jax.ShapeDtypeStruct(xs.shape, BF16),
        compiler_params=_cparams(("arbitrary",)),
        name="experts",
    )(meta["tile_expert"], meta["n_valid"], xs, wgu, wd)


def _combine_kernel(loff_s, goff_s, ngran_s, ntot_s, posw_ref, ys_ref, h1_ref, gt2_ref, shf_ref,
                    scf_ref, gfin_ref, o_ref, loc_ref, sem):
    i = pl.program_id(0)
    slot = i % 2
    rows = loc_ref.shape[1]

    def in_copy(s):
        return lambda lo, go, k: pltpu.make_async_copy(
            ys_ref.at[pl.ds(go, k * GRAN)], loc_ref.at[s, pl.ds(lo, k * GRAN)], sem.at[s])

    def fetch(tile, s):
        loc_ref[s] = jnp.zeros(loc_ref.shape[1:], BF16)
        _run_copies(tile, loff_s, goff_s, ngran_s, in_copy(s))

    @pl.when(i == 0)
    def _():
        fetch(0, 0)

    @pl.when(i + 1 < pl.num_programs(0))
    def _():
        fetch(i + 1, 1 - slot)

    _wait_copies(ntot_s[i], ntot_s[pl.num_programs(0) + i], in_copy(slot))

    posw = posw_ref[...]
    riota = lax.broadcasted_iota(jnp.int32, (1, rows), 1).astype(F32)
    qw = (jnp.where(riota == posw[:, 0:1], posw[:, 2:3], 0.0)
          + jnp.where(riota == posw[:, 1:2], posw[:, 3:4], 0.0)).astype(BF16)
    moe = jnp.dot(qw, loc_ref[slot], preferred_element_type=F32)
    h2 = h1_ref[...] + gt2_ref[0] * moe
    o_ref[...] = _rms(h2, gfin_ref[...]) * (1.0 + scf_ref[0]) + shf_ref[0]


def _combine(posw, ys, h1, gt2, shf, scf, gfin, meta, tm, rows, tiles_per_batch):
    t, d = h1.shape
    per_b = pl.BlockSpec((1, 1, d), lambda i, *_: (i // tiles_per_batch, 0, 0))
    grid_spec = pltpu.PrefetchScalarGridSpec(
        num_scalar_prefetch=4,
        grid=(t // tm,),
        in_specs=[pl.BlockSpec((tm, LANES), lambda i, *_: (i, 0)),
                  pl.BlockSpec(memory_space=pl.ANY),
                  pl.BlockSpec((tm, d), lambda i, *_: (i, 0)),
                  per_b, per_b, per_b, pl.BlockSpec((1, d), lambda i, *_: (0, 0))],
        out_specs=pl.BlockSpec((tm, d), lambda i, *_: (i, 0)),
        scratch_shapes=[pltpu.VMEM((2, rows, d), BF16), pltpu.SemaphoreType.DMA((2,))],
    )
    return pl.pallas_call(
        _combine_kernel,
        grid_spec=grid_spec,
        out_shape=jax.ShapeDtypeStruct((t, d), F32),
        compiler_params=_cparams(("arbitrary",)),
        name="combine",
    )(meta["loff"], meta["goff"], meta["ngran"], meta["ntot"], posw, ys, h1, gt2, shf, scf, gfin)


def _rot_cols(w):
    half = ROPE // 2
    return jnp.concatenate([-w[..., half:], w[..., :half]], axis=-1)


def _pad_cols(w, lo, width):
    return jnp.pad(w, [(0, 0)] * (w.ndim - 1) + [(lo, width - lo - w.shape[-1])])


def _prep_weights(w_in, w_uq, w_uk, w_uv):
    o2, o3 = Q_LORA + KV_LORA, Q_LORA + KV_LORA + ROPE
    w_kr = w_in[:, o2:o3]
    wcat = jnp.concatenate([
        w_in[:, :o2],
        _pad_cols(w_kr, NOPE, HEAD_PAD), _pad_cols(_rot_cols(w_kr), NOPE, HEAD_PAD),
        w_in[:, o3:]], axis=1).astype(BF16)
    wq_a = _pad_cols(w_uq, 0, HEAD_PAD)
    wq_b = _pad_cols(_rot_cols(w_uq[..., NOPE:]), NOPE, HEAD_PAD)
    r = w_uq.shape[0]
    wqt = jnp.concatenate([wq_a.reshape(r, -1), wq_b.reshape(r, -1)], axis=1).T.astype(BF16)
    wk = _pad_cols(w_uk, 0, HEAD_PAD).reshape(r, -1).astype(BF16)
    wvt = _pad_cols(w_uv, 0, V_ROWS).reshape(r, -1).T.astype(BF16)
    ones = np.zeros((N_HEADS, V_ROWS), np.float32)
    ones[:, V_DIM] = 1.0
    return wcat, wqt, wk, wvt, jnp.asarray(ones.reshape(-1, 1))


def kernel(x, c, positions, w_ada, b_ada, g_norm_mix, w_in, g_q_lat, g_kv_lat, w_uq, w_uk, w_uv, w_branch_attn, w_branch_fourier, b_gates, w_out, g_norm_ffn, w_router_group, b_router_group, w_router_expert, b_router_expert, w_expert_gate, w_expert_up, w_expert_down, w_ada_final, b_ada_final, g_norm_final):
    bsz, s, d = x.shape
    assert w_ada.shape[0] == 1 and s % FFT_N2 == 0
    tm = min(512, s)

    mod = _mods(c, w_ada[0], b_ada[0])
    fmod = _mods(c, w_ada_final, b_ada_final)
    sh1, sc1, gt1, sh2, sc2, gt2 = [m.reshape(bsz, 1, d) for m in jnp.split(mod, 6, axis=-1)]
    shf, scf = [m.reshape(bsz, 1, d) for m in jnp.split(fmod, 2, axis=-1)]

    wcat, wqt, wk, wvt, vones = _prep_weights(w_in[0], w_uq[0], w_uk[0], w_uv[0])
    bd, m1, wb = _dft_constants(s)
    inv_freq = ROPE_THETA ** (-jnp.arange(0, ROPE, 2, dtype=F32) / ROPE)
    invf = jnp.concatenate([inv_freq, inv_freq]).reshape(ROPE, 1)

    qt, k, vt, zr, zi, gates = _inproj(
        x, positions.reshape(bsz, 1, s), sh1, sc1, g_norm_mix[0][None, :], wcat,
        g_q_lat[0][None, :], g_kv_lat[0][None, :], wqt, wk, wvt, vones, bd, b_gates[0][None, :],
        invf, tm)

    attn = _attention(qt, k, vt, tq=min(256, s))
    br, bi = _fft_a(zr, zi, m1, ts2=8)
    four = _fft_b(br, bi, wb, tk1=16)

    wr = jnp.concatenate([w_router_expert[0].reshape(d, N_EXPERTS), w_router_group[0]], axis=1)
    wr = _pad_cols(wr, 0, LANES)
    wr_top = lax.bitcast_convert_type(
        lax.bitcast_convert_type(wr, jnp.uint32) & jnp.uint32(0xFFFF0000), F32)
    wr_hi = wr_top.astype(BF16)
    wr_lo = (wr - wr_top).astype(BF16)
    wr = jnp.concatenate([jnp.concatenate([wr_hi, wr_lo], axis=1),
                          jnp.concatenate([wr_hi, jnp.zeros_like(wr_lo)], axis=1)], axis=0)
    brt = _pad_cols(jnp.concatenate([b_router_expert[0].reshape(-1), b_router_group[0]])[None, :], 0, LANES)
    h1, v2, route, cnt = _merge(attn, four, gates, x, w_branch_attn[0].astype(BF16),
                                w_branch_fourier[0].astype(BF16), w_out[0].astype(BF16), gt1, sh2,
                                sc2, g_norm_ffn[0][None, :], wr, brt, tm)

    t = bsz * s
    meta, loffv, cap_tiles = _moe_metadata(cnt.reshape(-1, LANES), tm)
    rows = 2 * tm + N_EXPERTS * GRAN
    loffv = _pad_cols(loffv, 0, LANES).reshape(-1, 1, LANES)
    ltri = jnp.asarray(np.tril(np.ones((tm, tm), np.float32), -1), BF16)
    xs, posw = _dispatch(v2.reshape(t, d), route.reshape(t, LANES), meta, loffv, ltri, cap_tiles,
                         tm, rows)
    wgu = jnp.concatenate([w_expert_gate[0], w_expert_up[0]], axis=-1).astype(BF16)
    wd = w_expert_down[0].astype(BF16)
    ys = _experts(xs, wgu, wd, meta, cap_tiles)
    out = _combine(posw, ys, h1.reshape(t, d), gt2, shf, scf, g_norm_final[None, :], meta, tm, rows,
                   s // tm)
    return out.reshape(bsz, s, d)
```

```python
import functools

import numpy as np
import jax
import jax.numpy as jnp
from jax import lax
from jax.experimental import pallas as pl
from jax.experimental.pallas import tpu as pltpu

F32 = jnp.float32
BF16 = jnp.bfloat16

N_HEADS = 8
NOPE = 64
ROPE = 32
V_DIM = 64
HEAD_PAD = 128
V_ROWS = 96
Q_LORA = 256
KV_LORA = 256
ROPE_THETA = 10000.0
F_GROUPS = 8
F_GDIM = 64
F_WIDTH = F_GROUPS * F_GDIM
N_GROUPS = 4
E_PER_GROUP = 8
N_EXPERTS = N_GROUPS * E_PER_GROUP
D_EXPERT = 256
EPS = 1e-6
FFT_N2 = 64
LANES = 128
MERGE_ROWS = 128
ATTN_DEPTH = 1
ATTN_Q_SUB = 256
QK_ROWS = 256
VMEM_LIMIT = 56 * 1024 * 1024


def _cparams(sem):
    return pltpu.CompilerParams(dimension_semantics=sem, vmem_limit_bytes=VMEM_LIMIT)


def _mods_kernel(c_ref, w_ref, b_ref, o_ref):
    c = c_ref[...]
    ca = c * jax.nn.sigmoid(c)
    o_ref[...] = jnp.dot(ca, w_ref[...], preferred_element_type=F32,
                         precision=lax.Precision.HIGHEST) + b_ref[...]


def _mods(c, w, b):
    bsz, d = c.shape
    n = w.shape[1]
    tn = 1024
    return pl.pallas_call(
        _mods_kernel,
        grid=(n // tn,),
        in_specs=[pl.BlockSpec((bsz, d), lambda j: (0, 0)),
                  pl.BlockSpec((d, tn), lambda j: (0, j)),
                  pl.BlockSpec((1, tn), lambda j: (0, j))],
        out_specs=pl.BlockSpec((bsz, tn), lambda j: (0, j)),
        out_shape=jax.ShapeDtypeStruct((bsz, n), F32),
        compiler_params=_cparams(("arbitrary",)),
        name="mods",
    )(c, w, b.reshape(1, n))


def _rms(x, g):
    r = lax.rsqrt(jnp.mean(x * x, axis=-1, keepdims=True) + EPS)
    return x * r * g


def _inproj_kernel(x_ref, pos_ref, sh_ref, sc_ref, gmix_ref, wcat_ref, gq_ref, gkv_ref,
                   wqt_ref, wk_ref, wvt_ref, vones_ref, bd_ref, bg_ref, invf_ref,
                   qt_ref, k_ref, vt_ref, zr_ref, zi_ref, gate_ref, *, scale):
    hp = N_HEADS * HEAD_PAD
    nt_dims = (((1,), (1,)), ((), ()))
    u = _rms(x_ref[0], gmix_ref[...]) * (1.0 + sc_ref[0]) + sh_ref[0]
    ub = u.astype(BF16)

    o_q, o_kv, o_kr, o_f, o_g = 0, Q_LORA, Q_LORA + KV_LORA, Q_LORA + KV_LORA + 2 * HEAD_PAD, \
        Q_LORA + KV_LORA + 2 * HEAD_PAD + F_WIDTH

    tm = pos_ref.shape[-1]
    ang = invf_ref[...] * pos_ref[0].astype(F32)
    tail = jnp.zeros((HEAD_PAD - NOPE - ROPE, tm), F32)
    cos_t = jnp.concatenate([jnp.ones((NOPE, tm), F32), jnp.cos(ang), tail], axis=0)
    sin_t = jnp.concatenate([jnp.zeros((NOPE, tm), F32), jnp.sin(ang), tail], axis=0)

    ql = jnp.dot(ub, wcat_ref[:, o_q:o_kv], preferred_element_type=F32)
    kvl = jnp.dot(ub, wcat_ref[:, o_kv:o_kr], preferred_element_type=F32)
    kr = jnp.dot(ub, wcat_ref[:, o_kr:o_f], preferred_element_type=F32)
    fin = jnp.dot(ub, wcat_ref[:, o_f:o_g], preferred_element_type=F32).astype(BF16)
    zg = jnp.dot(ub, wcat_ref[:, o_g:], preferred_element_type=F32) + bg_ref[...]

    qn = _rms(ql, gq_ref[...]).astype(BF16)
    kvn = _rms(kvl, gkv_ref[...]).astype(BF16)
    zz = jnp.dot(fin, bd_ref[...], preferred_element_type=F32)
    qab_t = lax.dot_general(wqt_ref[...], qn, nt_dims, preferred_element_type=F32)
    kn = jnp.dot(kvn, wk_ref[...], preferred_element_type=F32)
    vt = lax.dot_general(wvt_ref[...], kvn, nt_dims, preferred_element_type=F32) + vones_ref[...]

    gate_ref[0] = jax.nn.sigmoid(zg).astype(BF16)
    zr_ref[0] = zz[:, :F_WIDTH].astype(BF16)
    zi_ref[0] = zz[:, F_WIDTH:].astype(BF16)
    cq = cos_t * scale
    sq = sin_t * scale
    for h in range(N_HEADS):
        a = qab_t[h * HEAD_PAD:(h + 1) * HEAD_PAD]
        b = qab_t[hp + h * HEAD_PAD:hp + (h + 1) * HEAD_PAD]
        qt_ref[0, h] = (a * cq + b * sq).astype(BF16)
    krr = kr[:, :HEAD_PAD] * cos_t.T + kr[:, HEAD_PAD:] * sin_t.T
    for h in range(N_HEADS):
        k_ref[0, h] = (kn[:, h * HEAD_PAD:(h + 1) * HEAD_PAD] + krr).astype(BF16)
        vt_ref[0, h, 0] = vt[h * V_ROWS:(h + 1) * V_ROWS].astype(BF16)


def _inproj(x, pos_row, sh1, sc1, gmix, wcat, gq, gkv, wqt, wk, wvt, vones, bd, bg, invf, tm):
    bsz, s, d = x.shape
    hp = N_HEADS * HEAD_PAD
    nt = s // tm
    const = lambda shape: pl.BlockSpec(shape, lambda b, i: (0,) * len(shape))
    tok = lambda w: pl.BlockSpec((1, tm, w), lambda b, i: (b, i, 0))
    per_b = pl.BlockSpec((1, 1, d), lambda b, i: (b, 0, 0))
    scale = float((NOPE + ROPE) ** -0.5 * np.log2(np.e))
    return pl.pallas_call(
        functools.partial(_inproj_kernel, scale=scale),
        grid=(bsz, nt),
        in_specs=[tok(d), pl.BlockSpec((1, 1, tm), lambda b, i: (b, 0, i)), per_b, per_b,
                  const((1, d)), const(wcat.shape), const((1, Q_LORA)), const((1, KV_LORA)),
                  const(wqt.shape), const(wk.shape), const(wvt.shape), const(vones.shape),
                  const(bd.shape), const((1, bg.shape[1])), const((ROPE, 1))],
        out_specs=[pl.BlockSpec((1, N_HEADS, HEAD_PAD, tm), lambda b, i: (b, 0, 0, i)),
                   pl.BlockSpec((1, N_HEADS, tm, HEAD_PAD), lambda b, i: (b, 0, i, 0)),
                   pl.BlockSpec((1, N_HEADS, 1, V_ROWS, tm), lambda b, i: (b, 0, i, 0, 0)),
                   tok(F_WIDTH), tok(F_WIDTH), tok(2 * d)],
        out_shape=[jax.ShapeDtypeStruct((bsz, N_HEADS, HEAD_PAD, s), BF16),
                   jax.ShapeDtypeStruct((bsz, N_HEADS, s, HEAD_PAD), BF16),
                   jax.ShapeDtypeStruct((bsz, N_HEADS, nt, V_ROWS, tm), BF16),
                   jax.ShapeDtypeStruct((bsz, s, F_WIDTH), BF16),
                   jax.ShapeDtypeStruct((bsz, s, F_WIDTH), BF16),
                   jax.ShapeDtypeStruct((bsz, s, 2 * d), BF16)],
        compiler_params=_cparams(("parallel", "parallel")),
        name="inproj",
    )(x, pos_row, sh1, sc1, gmix, wcat, gq, gkv, wqt, wk, wvt, vones, bd, bg, invf)


def _attn_kernel(qt_ref, k_ref, vt_ref, o_ref, *p_refs, nk):
    tq = qt_ref.shape[3]
    tk = vt_ref.shape[4]
    nbuf = ATTN_DEPTH + 1
    streams = [(hh, qi) for hh in range(2) for qi in range(tq // ATTN_Q_SUB)]
    qts = [qt_ref[0, hh][:, qi * ATTN_Q_SUB:(qi + 1) * ATTN_Q_SUB] for hh, qi in streams]

    def scores(c, st):
        hh = streams[st][0]
        ss = []
        for i in range(tk // QK_ROWS):
            off = c * tk + i * QK_ROWS
            ss.append(jnp.dot(k_ref[0, hh, off:off + QK_ROWS, :], qts[st],
                              preferred_element_type=F32))
        return ss

    def probs(ss, m, p_ref):
        m_new = m
        for s in ss:
            m_new = jnp.maximum(m_new, jnp.max(s, axis=0, keepdims=True))
        for i, s in enumerate(ss):
            p_ref[i * QK_ROWS:(i + 1) * QK_ROWS, :] = jnp.exp2(s - m_new).astype(BF16)
        return m_new, jnp.exp2(m - m_new)

    def accumulate(c, st, acc, alpha, p_ref):
        return alpha * acc + jnp.dot(vt_ref[0, streams[st][0], c], p_ref[...],
                                     preferred_element_type=F32)

    sts = range(len(streams))
    buf = lambda c, st: p_refs[st * nbuf + c % nbuf]
    m = [jnp.full((1, ATTN_Q_SUB), -jnp.inf, F32) for _ in sts]
    alpha = [[None] * nk for _ in sts]
    accs = [jnp.zeros((V_ROWS, ATTN_Q_SUB), F32) for _ in sts]
    for c in range(nk + ATTN_DEPTH):
        if c < nk:
            s = [scores(c, st) for st in sts]
        if c >= ATTN_DEPTH:
            for st in sts:
                accs[st] = accumulate(c - ATTN_DEPTH, st, accs[st], alpha[st][c - ATTN_DEPTH],
                                      buf(c - ATTN_DEPTH, st))
        if c < nk:
            for st in sts:
                m[st], alpha[st][c] = probs(s[st], m[st], buf(c, st))
    for qi in range(tq // ATTN_Q_SUB):
        out_t = jnp.concatenate([accs[st][:V_DIM] / accs[st][V_DIM:V_DIM + 1]
                                 for st in sts if streams[st][1] == qi], axis=0)
        o_ref[0, qi * ATTN_Q_SUB:(qi + 1) * ATTN_Q_SUB, :] = out_t.T.astype(BF16)


def _attention(qt, k, vt, tq):
    bsz, nh, s, _ = k.shape
    nk, tk = vt.shape[2], vt.shape[4]
    return pl.pallas_call(
        functools.partial(_attn_kernel, nk=nk),
        grid=(bsz, nh // 2, s // tq),
        in_specs=[pl.BlockSpec((1, 2, HEAD_PAD, tq), lambda b, j, i: (b, j, 0, i)),
                  pl.BlockSpec((1, 2, s, HEAD_PAD), lambda b, j, i: (b, j, 0, 0)),
                  pl.BlockSpec((1, 2, nk, V_ROWS, tk), lambda b, j, i: (b, j, 0, 0, 0))],
        out_specs=pl.BlockSpec((1, tq, HEAD_PAD), lambda b, j, i: (b, i, j)),
        out_shape=jax.ShapeDtypeStruct((bsz, s, nh * V_DIM), BF16),
        scratch_shapes=[pltpu.VMEM((tk, ATTN_Q_SUB), BF16)] * (2 * (tq // ATTN_Q_SUB) * (ATTN_DEPTH + 1)),
        compiler_params=_cparams(("parallel", "parallel", "arbitrary")),
        name="attention",
    )(qt, k, vt)


def _fft_a_kernel(zr_ref, zi_ref, m1_ref, br_ref, bi_ref):
    n1 = zr_ref.shape[1]
    ts2 = m1_ref.shape[0]
    for j in range(ts2):
        sl = slice(j * F_WIDTH, (j + 1) * F_WIDTH)
        xx = jnp.concatenate([zr_ref[0, :, sl], zi_ref[0, :, sl]], axis=0)
        y = jnp.dot(m1_ref[j], xx, preferred_element_type=F32)
        br_ref[0, :, sl] = y[:n1].astype(BF16)
        bi_ref[0, :, sl] = y[n1:].astype(BF16)


def _fft_a(zr, zi, m1, ts2):
    bsz, s, _ = zr.shape
    n1 = s // FFT_N2
    zr2 = zr.reshape(bsz, n1, FFT_N2 * F_WIDTH)
    zi2 = zi.reshape(bsz, n1, FFT_N2 * F_WIDTH)
    blk = pl.BlockSpec((1, n1, ts2 * F_WIDTH), lambda b, j: (b, 0, j))
    return pl.pallas_call(
        _fft_a_kernel,
        grid=(bsz, FFT_N2 // ts2),
        in_specs=[blk, blk, pl.BlockSpec((ts2, 2 * n1, 2 * n1), lambda b, j: (j, 0, 0))],
        out_specs=[blk, blk],
        out_shape=[jax.ShapeDtypeStruct(zr2.shape, BF16)] * 2,
        compiler_params=_cparams(("parallel", "parallel")),
        name="fft_a",
    )(zr2, zi2, m1)


def _fft_b_kernel(br_ref, bi_ref, wb_ref, o_ref):
    tk1 = br_ref.shape[1]
    for j in range(tk1):
        xx = jnp.concatenate([br_ref[0, j], bi_ref[0, j]], axis=0)
        y = jnp.dot(wb_ref[...], xx, preferred_element_type=F32)
        o_ref[0, :, j * F_WIDTH:(j + 1) * F_WIDTH] = y.astype(BF16)


def _fft_b(br, bi, wb, tk1):
    bsz, n1, _ = br.shape
    br4 = br.reshape(bsz, n1, FFT_N2, F_WIDTH)
    bi4 = bi.reshape(bsz, n1, FFT_N2, F_WIDTH)
    blk = pl.BlockSpec((1, tk1, FFT_N2, F_WIDTH), lambda b, j: (b, j, 0, 0))
    out = pl.pallas_call(
        _fft_b_kernel,
        grid=(bsz, n1 // tk1),
        in_specs=[blk, blk, pl.BlockSpec((FFT_N2, 2 * FFT_N2), lambda b, j: (0, 0))],
        out_specs=pl.BlockSpec((1, FFT_N2, tk1 * F_WIDTH), lambda b, j: (b, 0, j)),
        out_shape=jax.ShapeDtypeStruct((bsz, FFT_N2, n1 * F_WIDTH), BF16),
        compiler_params=_cparams(("parallel", "parallel")),
        name="fft_b",
    )(br4, bi4, wb)
    return out.reshape(bsz, n1 * FFT_N2, F_WIDTH)


def _dft_constants(s):
    n1, n2 = s // FFT_N2, FFT_N2
    m = np.arange(F_GDIM)
    th = 2.0 * np.pi * np.outer(m, m) / F_GDIM
    eye = np.eye(F_GROUPS)
    bd = np.concatenate([np.kron(eye, np.cos(th)), -np.kron(eye, np.sin(th))], axis=1) / np.sqrt(F_GDIM)
    k1 = np.arange(n1)[None, :, None]
    s1 = np.arange(n1)[None, None, :]
    s2 = np.arange(n2)[:, None, None]
    ph = 2.0 * np.pi * ((k1 * (n2 * s1 + s2)) % s) / s
    cr, ci = np.cos(ph) / np.sqrt(n1), -np.sin(ph) / np.sqrt(n1)
    m1 = np.concatenate([np.concatenate([cr, -ci], axis=2), np.concatenate([ci, cr], axis=2)], axis=1)
    k2 = np.arange(n2)
    tb = 2.0 * np.pi * np.outer(k2, k2) / n2
    wb = np.concatenate([np.cos(tb), np.sin(tb)], axis=1) / np.sqrt(n2)
    return (jnp.asarray(bd, BF16), jnp.asarray(m1, BF16), jnp.asarray(wb, BF16))


def _route(parts, bias):
    logit = parts[:, :LANES] + parts[:, LANES:] + bias
    lane = lax.broadcasted_iota(jnp.int32, logit.shape, 1)
    neg = -jnp.inf
    big = jnp.int32(LANES)

    def softmax_masked(mask):
        z = jnp.where(mask, logit, neg)
        e = jnp.exp(z - jnp.max(z, axis=-1, keepdims=True))
        return e / jnp.sum(e, axis=-1, keepdims=True)

    def top1(p, mask):
        pm = jnp.where(mask, p, -1.0)
        best = jnp.max(pm, axis=-1, keepdims=True)
        idx = jnp.min(jnp.where(pm == best, lane, big), axis=-1, keepdims=True)
        return best, idx

    gmask = (lane >= N_EXPERTS) & (lane < N_EXPERTS + N_GROUPS)
    g_prob = softmax_masked(gmask)
    g_p, g_lane = top1(g_prob, gmask)
    e_lo = (g_lane - N_EXPERTS) * E_PER_GROUP
    emask = (lane >= e_lo) & (lane < e_lo + E_PER_GROUP)
    e_prob = softmax_masked(emask)
    p1, i1 = top1(e_prob, emask)
    p2, i2 = top1(e_prob, emask & (lane != i1))
    den = p1 + p2
    w1 = g_p * (p1 / den)
    w2 = g_p * (p2 / den)
    route = jnp.where(lane == 0, i1.astype(F32), jnp.where(lane == 1, i2.astype(F32),
                      jnp.where(lane == 2, w1, jnp.where(lane == 3, w2, 0.0))))
    cnt = jnp.sum(((lane == i1) | (lane == i2)).astype(F32), axis=0, keepdims=True)
    return route, cnt


def _merge_kernel(at_ref, fo_ref, gate_ref, x_ref, wba_ref, wbf_ref, wout_ref, gt1_ref, sh2_ref,
                  sc2_ref, gffn_ref, wr_ref, br_ref, h1_ref, v2_ref, route_ref, cnt_ref):
    d = x_ref.shape[-1]
    tm = x_ref.shape[1]
    blocks = [slice(i * MERGE_ROWS, (i + 1) * MERGE_ROWS) for i in range(tm // MERGE_ROWS)]
    a = [jnp.dot(at_ref[0, r, :], wba_ref[...], preferred_element_type=F32) for r in blocks]
    fo = [jnp.dot(fo_ref[0, r, :], wbf_ref[...], preferred_element_type=F32) for r in blocks]
    y = []
    for r, ai, fi in zip(blocks, a, fo):
        merged = gate_ref[0, r, :d].astype(F32) * ai + gate_ref[0, r, d:].astype(F32) * fi
        y.append(jnp.dot(merged.astype(BF16), wout_ref[...], preferred_element_type=F32))
    parts = []
    for r, yi in zip(blocks, y):
        h1 = x_ref[0, r, :] + gt1_ref[0] * yi
        h1_ref[0, r, :] = h1
        v2 = _rms(h1, gffn_ref[...]) * (1.0 + sc2_ref[0]) + sh2_ref[0]
        v_hi = v2.astype(BF16)
        v2_ref[0, r, :] = v_hi
        v_lo = (v2 - v_hi.astype(F32)).astype(BF16)
        parts.append(jnp.dot(jnp.concatenate([v_hi, v_lo], axis=1), wr_ref[...],
                             preferred_element_type=F32))
    cnt = None
    for r, pi in zip(blocks, parts):
        route, c = _route(pi, br_ref[...])
        route_ref[0, r, :] = route
        cnt = c if cnt is None else cnt + c
    cnt_ref[0, 0] = cnt


def _merge(attn, four, gates, x, wba, wbf, wout, gt1, sh2, sc2, gffn, wr, br, tm):
    bsz, s, d = x.shape
    const = lambda shape: pl.BlockSpec(shape, lambda b, i: (0,) * len(shape))
    tok = lambda w: pl.BlockSpec((1, tm, w), lambda b, i: (b, i, 0))
    per_b = pl.BlockSpec((1, 1, d), lambda b, i: (b, 0, 0))
    return pl.pallas_call(
        _merge_kernel,
        grid=(bsz, s // tm),
        in_specs=[tok(attn.shape[-1]), tok(F_WIDTH), tok(2 * d), tok(d), const(wba.shape),
                  const(wbf.shape), const(wout.shape), per_b, per_b, per_b, const((1, d)),
                  const(wr.shape), const((1, LANES))],
        out_specs=[tok(d), tok(d), tok(LANES),
                   pl.BlockSpec((1, 1, 1, LANES), lambda b, i: (b, i, 0, 0))],
        out_shape=[jax.ShapeDtypeStruct((bsz, s, d), F32), jax.ShapeDtypeStruct((bsz, s, d), BF16),
                   jax.ShapeDtypeStruct((bsz, s, LANES), F32),
                   jax.ShapeDtypeStruct((bsz, s // tm, 1, LANES), F32)],
        compiler_params=_cparams(("parallel", "parallel")),
        name="merge",
    )(attn, four, gates, x, wba, wbf, wout, gt1, sh2, sc2, gffn, wr, br)


GRAN = 16
MOE_ROW_TILE = 1024


def _moe_metadata(cnt, tm):
    ntile = cnt.shape[0]

    def cumsum(a, axis):
        n = a.shape[axis]
        tri = jnp.asarray(np.triu(np.ones((n, n), np.float32)))
        af = a.astype(F32)
        out = (jnp.dot(af, tri, precision=lax.Precision.HIGHEST) if axis == a.ndim - 1
               else jnp.dot(tri.T, af, precision=lax.Precision.HIGHEST))
        return out.astype(jnp.int32)

    c = cnt[:, :N_EXPERTS].astype(jnp.int32)
    pc = (c + GRAN - 1) // GRAN * GRAN
    loff = cumsum(pc, 1) - pc
    tot = jnp.sum(pc, axis=0)
    tot_r = (tot + MOE_ROW_TILE - 1) // MOE_ROW_TILE * MOE_ROW_TILE
    ends = cumsum(tot_r[None, :], 1)[0]
    base = ends - tot_r
    goff = base[None, :] + cumsum(pc, 0) - pc
    cap_tiles = (2 * ntile * tm + ntile * N_EXPERTS * GRAN) // MOE_ROW_TILE + N_EXPERTS
    n_valid = (ends[-1] // MOE_ROW_TILE).astype(jnp.int32)
    tile_start = jnp.arange(cap_tiles, dtype=jnp.int32) * MOE_ROW_TILE
    tile_expert = jnp.minimum(jnp.sum((ends[None, :] <= tile_start[:, None]).astype(jnp.int32), axis=1),
                              N_EXPERTS - 1)
    flat = lambda a: a.reshape(-1).astype(jnp.int32)
    tail = tot_r - tot + jnp.where(jnp.arange(N_EXPERTS) == N_EXPERTS - 1,
                                   cap_tiles * MOE_ROW_TILE - ends[-1], 0)
    meta = dict(loff=flat(loff), goff=flat(goff), ngran=flat(pc // GRAN),
                ntot=jnp.concatenate([flat(jnp.sum(pc // (2 * GRAN), axis=1)),
                                      flat(jnp.sum(pc // GRAN % 2, axis=1))]),
                tail_start=flat(base + tot), tail_n=flat(tail // GRAN),
                tile_expert=tile_expert, n_valid=n_valid.reshape(1))
    return meta, loff.astype(F32), cap_tiles


def _run_copies(i, loff_s, goff_s, ngran_s, make_copy):
    def per_expert(e, carry):
        n = ngran_s[i * N_EXPERTS + e]
        lo = loff_s[i * N_EXPERTS + e]
        go = goff_s[i * N_EXPERTS + e]

        def per_pair(g, c):
            make_copy(pl.multiple_of(lo + g * 2 * GRAN, GRAN),
                      pl.multiple_of(go + g * 2 * GRAN, GRAN), 2).start()
            return c

        lax.fori_loop(0, n // 2, per_pair, 0)

        @pl.when(n % 2 == 1)
        def _():
            make_copy(pl.multiple_of(lo + (n - 1) * GRAN, GRAN),
                      pl.multiple_of(go + (n - 1) * GRAN, GRAN), 1).start()

        return carry

    lax.fori_loop(0, N_EXPERTS, per_expert, 0)


def _wait_copies(n_pairs, n_singles, make_copy):
    def wait(k):
        def one(_, carry):
            make_copy(0, 0, k).wait()
            return carry
        return one
    lax.fori_loop(0, n_pairs, wait(2), 0)
    lax.fori_loop(0, n_singles, wait(1), 0)


def _local_positions(route, loffv, ltri):
    lane = lax.broadcasted_iota(jnp.int32, route.shape, 1).astype(F32)
    o1 = lane == route[:, 0:1]
    o2 = lane == route[:, 1:2]
    before = jnp.dot(ltri, (o1 | o2).astype(BF16), preferred_element_type=F32)
    start = before + loffv
    pos1 = jnp.sum(jnp.where(o1, start, 0.0), axis=-1, keepdims=True)
    pos2 = jnp.sum(jnp.where(o2, start, 0.0), axis=-1, keepdims=True)
    return pos1, pos2


def _dispatch_kernel(loff_s, goff_s, ngran_s, ntot_s, tstart_s, tn_s, v2_ref, route_ref, loffv_ref,
                     ltri_ref, xs_ref, posw_ref, loc_ref, zero_ref, sem):
    i = pl.program_id(0)
    last = pl.num_programs(0) - 1
    slot = i % 2
    rows = loc_ref.shape[1]
    route = route_ref[...]
    pos1, pos2 = _local_positions(route, loffv_ref[0], ltri_ref[...])
    lane = lax.broadcasted_iota(jnp.int32, route.shape, 1)
    posw = jnp.where(lane == 0, pos1, jnp.where(lane == 1, pos2, route))
    posw_ref[...] = posw
    pos_t = posw.T
    riota = lax.broadcasted_iota(jnp.int32, (rows, 1), 0).astype(F32)
    onehot = ((riota == pos_t[0:1]) | (riota == pos_t[1:2])).astype(BF16)
    loc_ref[slot] = jnp.dot(onehot, v2_ref[...], preferred_element_type=F32).astype(BF16)

    def out_copy(s):
        return lambda lo, go, k: pltpu.make_async_copy(
            loc_ref.at[s, pl.ds(lo, k * GRAN)], xs_ref.at[pl.ds(go, k * GRAN)], sem.at[s])

    _run_copies(i, loff_s, goff_s, ngran_s, out_copy(slot))
    ntile = pl.num_programs(0)

    @pl.when(i > 0)
    def _():
        _wait_copies(ntot_s[i - 1], ntot_s[ntile + i - 1], out_copy(1 - slot))

    @pl.when(i == last)
    def _():
        _wait_copies(ntot_s[i], ntot_s[ntile + i], out_copy(slot))
        zero_ref[...] = jnp.zeros(zero_ref.shape, BF16)

        def zero_copy(go):
            return pltpu.make_async_copy(zero_ref, xs_ref.at[pl.ds(go, GRAN)], sem.at[0])

        def per_expert(e, n_started):
            def per_granule(g, carry):
                zero_copy(pl.multiple_of(tstart_s[e] + g * GRAN, GRAN)).start()
                return carry
            lax.fori_loop(0, tn_s[e], per_granule, 0)
            return n_started + tn_s[e]

        nz = lax.fori_loop(0, N_EXPERTS, per_expert, 0)
        _wait_copies(0, nz, lambda lo, go, k: zero_copy(go))


def _dispatch(v2, route, meta, loffv, ltri, cap_tiles, tm, rows):
    t, d = v2.shape
    ntile = t // tm
    grid_spec = pltpu.PrefetchScalarGridSpec(
        num_scalar_prefetch=6,
        grid=(ntile,),
        in_specs=[pl.BlockSpec((tm, d), lambda i, *_: (i, 0)),
                  pl.BlockSpec((tm, LANES), lambda i, *_: (i, 0)),
                  pl.BlockSpec((1, 1, LANES), lambda i, *_: (i, 0, 0)),
                  pl.BlockSpec((tm, tm), lambda i, *_: (0, 0))],
        out_specs=[pl.BlockSpec(memory_space=pl.ANY),
                   pl.BlockSpec((tm, LANES), lambda i, *_: (i, 0))],
        scratch_shapes=[pltpu.VMEM((2, rows, d), BF16), pltpu.VMEM((GRAN, d), BF16),
                        pltpu.SemaphoreType.DMA((2,))],
    )
    return pl.pallas_call(
        _dispatch_kernel,
        grid_spec=grid_spec,
        out_shape=[jax.ShapeDtypeStruct((cap_tiles * MOE_ROW_TILE, d), BF16),
                   jax.ShapeDtypeStruct((t, LANES), F32)],
        compiler_params=_cparams(("arbitrary",)),
        name="dispatch",
    )(meta["loff"], meta["goff"], meta["ngran"], meta["ntot"], meta["tail_start"], meta["tail_n"],
      v2, route, loffv, ltri)


def _experts_kernel(texp_s, nvalid_s, x_ref, wgu_ref, wd_ref, y_ref):
    del texp_s

    @pl.when(pl.program_id(0) < nvalid_s[0])
    def _():
        hc = jnp.dot(x_ref[...], wgu_ref[0], preferred_element_type=F32)
        g = hc[:, :D_EXPERT]
        hid = (g * jax.nn.sigmoid(g) * hc[:, D_EXPERT:]).astype(BF16)
        y_ref[...] = jnp.dot(hid, wd_ref[0], preferred_element_type=F32).astype(BF16)

    @pl.when(pl.program_id(0) >= nvalid_s[0])
    def _():
        y_ref[...] = jnp.zeros(y_ref.shape, BF16)


def _experts(xs, wgu, wd, meta, cap_tiles):
    d = xs.shape[1]
    row = lambda j, texp, nv: (jnp.minimum(j, nv[0] - 1), 0)
    wsel = lambda j, texp, nv: (texp[jnp.minimum(j, nv[0] - 1)], 0, 0)
    grid_spec = pltpu.PrefetchScalarGridSpec(
        num_scalar_prefetch=2,
        grid=(cap_tiles,),
        in_specs=[pl.BlockSpec((MOE_ROW_TILE, d), row),
                  pl.BlockSpec((1, d, 2 * D_EXPERT), wsel),
                  pl.BlockSpec((1, D_EXPERT, d), wsel)],
        out_specs=pl.BlockSpec((MOE_ROW_TILE, d), lambda j, texp, nv: (j, 0)),
    )
    return pl.pallas_call(
        _experts_kernel,
        grid_spec=grid_spec,
        out_shape=jax.ShapeDtypeStruct(xs.shape, BF16),
        compiler_params=_cparams(("arbitrary",)),
        name="experts",
    )(meta["tile_expert"], meta["n_valid"], xs, wgu, wd)


def _combine_kernel(loff_s, goff_s, ngran_s, ntot_s, posw_ref, ys_ref, h1_ref, gt2_ref, shf_ref,
                    scf_ref, gfin_ref, o_ref, loc_ref, sem):
    i = pl.program_id(0)
    slot = i % 2
    rows = loc_ref.shape[1]

    def in_copy(s):
        return lambda lo, go, k: pltpu.make_async_copy(
            ys_ref.at[pl.ds(go, k * GRAN)], loc_ref.at[s, pl.ds(lo, k * GRAN)], sem.at[s])

    def fetch(tile, s):
        loc_ref[s] = jnp.zeros(loc_ref.shape[1:], BF16)
        _run_copies(tile, loff_s, goff_s, ngran_s, in_copy(s))

    @pl.when(i == 0)
    def _():
        fetch(0, 0)

    @pl.when(i + 1 < pl.num_programs(0))
    def _():
        fetch(i + 1, 1 - slot)

    _wait_copies(ntot_s[i], ntot_s[pl.num_programs(0) + i], in_copy(slot))

    posw = posw_ref[...]
    riota = lax.broadcasted_iota(jnp.int32, (1, rows), 1).astype(F32)
    qw = (jnp.where(riota == posw[:, 0:1], posw[:, 2:3], 0.0)
          + jnp.where(riota == posw[:, 1:2], posw[:, 3:4], 0.0)).astype(BF16)
    moe = jnp.dot(qw, loc_ref[slot], preferred_element_type=F32)
    h2 = h1_ref[...] + gt2_ref[0] * moe
    o_ref[...] = _rms(h2, gfin_ref[...]) * (1.0 + scf_ref[0]) + shf_ref[0]


def _combine(posw, ys, h1, gt2, shf, scf, gfin, meta, tm, rows, tiles_per_batch):
    t, d = h1.shape
    per_b = pl.BlockSpec((1, 1, d), lambda i, *_: (i // tiles_per_batch, 0, 0))
    grid_spec = pltpu.PrefetchScalarGridSpec(
        num_scalar_prefetch=4,
        grid=(t // tm,),
        in_specs=[pl.BlockSpec((tm, LANES), lambda i, *_: (i, 0)),
                  pl.BlockSpec(memory_space=pl.ANY),
                  pl.BlockSpec((tm, d), lambda i, *_: (i, 0)),
                  per_b, per_b, per_b, pl.BlockSpec((1, d), lambda i, *_: (0, 0))],
        out_specs=pl.BlockSpec((tm, d), lambda i, *_: (i, 0)),
        scratch_shapes=[pltpu.VMEM((2, rows, d), BF16), pltpu.SemaphoreType.DMA((2,))],
    )
    return pl.pallas_call(
        _combine_kernel,
        grid_spec=grid_spec,
        out_shape=jax.ShapeDtypeStruct((t, d), F32),
        compiler_params=_cparams(("arbitrary",)),
        name="combine",
    )(meta["loff"], meta["goff"], meta["ngran"], meta["ntot"], posw, ys, h1, gt2, shf, scf, gfin)


def _rot_cols(w):
    half = ROPE // 2
    return jnp.concatenate([-w[..., half:], w[..., :half]], axis=-1)


def _pad_cols(w, lo, width):
    return jnp.pad(w, [(0, 0)] * (w.ndim - 1) + [(lo, width - lo - w.shape[-1])])


def _prep_weights(w_in, w_uq, w_uk, w_uv):
    o2, o3 = Q_LORA + KV_LORA, Q_LORA + KV_LORA + ROPE
    w_kr = w_in[:, o2:o3]
    wcat = jnp.concatenate([
        w_in[:, :o2],
        _pad_cols(w_kr, NOPE, HEAD_PAD), _pad_cols(_rot_cols(w_kr), NOPE, HEAD_PAD),
        w_in[:, o3:]], axis=1).astype(BF16)
    wq_a = _pad_cols(w_uq, 0, HEAD_PAD)
    wq_b = _pad_cols(_rot_cols(w_uq[..., NOPE:]), NOPE, HEAD_PAD)
    r = w_uq.shape[0]
    wqt = jnp.concatenate([wq_a.reshape(r, -1), wq_b.reshape(r, -1)], axis=1).T.astype(BF16)
    wk = _pad_cols(w_uk, 0, HEAD_PAD).reshape(r, -1).astype(BF16)
    wvt = _pad_cols(w_uv, 0, V_ROWS).reshape(r, -1).T.astype(BF16)
    ones = np.zeros((N_HEADS, V_ROWS), np.float32)
    ones[:, V_DIM] = 1.0
    return wcat, wqt, wk, wvt, jnp.asarray(ones.reshape(-1, 1))


def kernel(x, c, positions, w_ada, b_ada, g_norm_mix, w_in, g_q_lat, g_kv_lat, w_uq, w_uk, w_uv, w_branch_attn, w_branch_fourier, b_gates, w_out, g_norm_ffn, w_router_group, b_router_group, w_router_expert, b_router_expert, w_expert_gate, w_expert_up, w_expert_down, w_ada_final, b_ada_final, g_norm_final):
    bsz, s, d = x.shape
    assert w_ada.shape[0] == 1 and s % FFT_N2 == 0
    tm = min(512, s)

    mod = _mods(c, w_ada[0], b_ada[0])
    fmod = _mods(c, w_ada_final, b_ada_final)
    sh1, sc1, gt1, sh2, sc2, gt2 = [m.reshape(bsz, 1, d) for m in jnp.split(mod, 6, axis=-1)]
    shf, scf = [m.reshape(bsz, 1, d) for m in jnp.split(fmod, 2, axis=-1)]

    wcat, wqt, wk, wvt, vones = _prep_weights(w_in[0], w_uq[0], w_uk[0], w_uv[0])
    bd, m1, wb = _dft_constants(s)
    inv_freq = ROPE_THETA ** (-jnp.arange(0, ROPE, 2, dtype=F32) / ROPE)
    invf = jnp.concatenate([inv_freq, inv_freq]).reshape(ROPE, 1)

    qt, k, vt, zr, zi, gates = _inproj(
        x, positions.reshape(bsz, 1, s), sh1, sc1, g_norm_mix[0][None, :], wcat,
        g_q_lat[0][None, :], g_kv_lat[0][None, :], wqt, wk, wvt, vones, bd, b_gates[0][None, :],
        invf, tm)

    attn = _attention(qt, k, vt, tq=min(2 * ATTN_Q_SUB, s))
    br, bi = _fft_a(zr, zi, m1, ts2=8)
    four = _fft_b(br, bi, wb, tk1=16)

    wr = jnp.concatenate([w_router_expert[0].reshape(d, N_EXPERTS), w_router_group[0]], axis=1)
    wr = _pad_cols(wr, 0, LANES)
    wr_top = lax.bitcast_convert_type(
        lax.bitcast_convert_type(wr, jnp.uint32) & jnp.uint32(0xFFFF0000), F32)
    wr_hi = wr_top.astype(BF16)
    wr_lo = (wr - wr_top).astype(BF16)
    wr = jnp.concatenate([jnp.concatenate([wr_hi, wr_lo], axis=1),
                          jnp.concatenate([wr_hi, jnp.zeros_like(wr_lo)], axis=1)], axis=0)
    brt = _pad_cols(jnp.concatenate([b_router_expert[0].reshape(-1), b_router_group[0]])[None, :], 0, LANES)
    h1, v2, route, cnt = _merge(attn, four, gates, x, w_branch_attn[0].astype(BF16),
                                w_branch_fourier[0].astype(BF16), w_out[0].astype(BF16), gt1, sh2,
                                sc2, g_norm_ffn[0][None, :], wr, brt, tm)

    t = bsz * s
    meta, loffv, cap_tiles = _moe_metadata(cnt.reshape(-1, LANES), tm)
    rows = 2 * tm + N_EXPERTS * GRAN
    loffv = _pad_cols(loffv, 0, LANES).reshape(-1, 1, LANES)
    ltri = jnp.asarray(np.tril(np.ones((tm, tm), np.float32), -1), BF16)
    xs, posw = _dispatch(v2.reshape(t, d), route.reshape(t, LANES), meta, loffv, ltri, cap_tiles,
                         tm, rows)
    wgu = jnp.concatenate([w_expert_gate[0], w_expert_up[0]], axis=-1).astype(BF16)
    wd = w_expert_down[0].astype(BF16)
    ys = _experts(xs, wgu, wd, meta, cap_tiles)
    out = _combine(posw, ys, h1.reshape(t, d), gt2, shf, scf, g_norm_final[None, :], meta, tm, rows,
                   s // tm)
    return out.reshape(bsz, s, d)
```

```python
import functools

import numpy as np
import jax
import jax.numpy as jnp
from jax import lax
from jax.experimental import pallas as pl
from jax.experimental.pallas import tpu as pltpu

F32 = jnp.float32
BF16 = jnp.bfloat16

N_HEADS = 8
NOPE = 64
ROPE = 32
V_DIM = 64
HEAD_PAD = 128
V_ROWS = 96
Q_LORA = 256
KV_LORA = 256
ROPE_THETA = 10000.0
F_GROUPS = 8
F_GDIM = 64
F_WIDTH = F_GROUPS * F_GDIM
N_GROUPS = 4
E_PER_GROUP = 8
N_EXPERTS = N_GROUPS * E_PER_GROUP
D_EXPERT = 256
EPS = 1e-6
FFT_N2 = 64
LANES = 128
MERGE_ROWS = 128
ATTN_DEPTH = 1
QK_ROWS = 256
V7X_VMEM_BYTES = 64 * 1024 * 1024
VMEM_LIMIT = V7X_VMEM_BYTES - 8 * 1024 * 1024


def _cparams(sem):
    return pltpu.CompilerParams(dimension_semantics=sem, vmem_limit_bytes=VMEM_LIMIT)


def _mods_kernel(c_ref, w_ref, b_ref, o_ref):
    c = c_ref[...]
    ca = c * jax.nn.sigmoid(c)
    o_ref[...] = jnp.dot(ca, w_ref[...], preferred_element_type=F32,
                         precision=lax.Precision.HIGHEST) + b_ref[...]


def _mods(c, w, b):
    bsz, d = c.shape
    n = w.shape[1]
    tn = 1024
    return pl.pallas_call(
        _mods_kernel,
        grid=(n // tn,),
        in_specs=[pl.BlockSpec((bsz, d), lambda j: (0, 0)),
                  pl.BlockSpec((d, tn), lambda j: (0, j)),
                  pl.BlockSpec((1, tn), lambda j: (0, j))],
        out_specs=pl.BlockSpec((bsz, tn), lambda j: (0, j)),
        out_shape=jax.ShapeDtypeStruct((bsz, n), F32),
        compiler_params=_cparams(("arbitrary",)),
        name="mods",
    )(c, w, b.reshape(1, n))


def _rms(x, g):
    r = lax.rsqrt(jnp.mean(x * x, axis=-1, keepdims=True) + EPS)
    return x * r * g


def _inproj_kernel(x_ref, pos_ref, sh_ref, sc_ref, gmix_ref, wcat_ref, gq_ref, gkv_ref,
                   wqt_ref, wk_ref, wvt_ref, vones_ref, bd_ref, bg_ref, invf_ref,
                   qt_ref, k_ref, vt_ref, zr_ref, zi_ref, gate_ref, *, scale):
    hp = N_HEADS * HEAD_PAD
    nt_dims = (((1,), (1,)), ((), ()))
    u = _rms(x_ref[0], gmix_ref[...]) * (1.0 + sc_ref[0]) + sh_ref[0]
    ub = u.astype(BF16)

    o_q, o_kv, o_kr, o_f, o_g = 0, Q_LORA, Q_LORA + KV_LORA, Q_LORA + KV_LORA + 2 * HEAD_PAD, \
        Q_LORA + KV_LORA + 2 * HEAD_PAD + F_WIDTH

    tm = pos_ref.shape[-1]
    ang = invf_ref[...] * pos_ref[0].astype(F32)
    tail = jnp.zeros((HEAD_PAD - NOPE - ROPE, tm), F32)
    cos_t = jnp.concatenate([jnp.ones((NOPE, tm), F32), jnp.cos(ang), tail], axis=0)
    sin_t = jnp.concatenate([jnp.zeros((NOPE, tm), F32), jnp.sin(ang), tail], axis=0)

    ql = jnp.dot(ub, wcat_ref[:, o_q:o_kv], preferred_element_type=F32)
    kvl = jnp.dot(ub, wcat_ref[:, o_kv:o_kr], preferred_element_type=F32)
    kr = jnp.dot(ub, wcat_ref[:, o_kr:o_f], preferred_element_type=F32)
    fin = jnp.dot(ub, wcat_ref[:, o_f:o_g], preferred_element_type=F32).astype(BF16)
    zg = jnp.dot(ub, wcat_ref[:, o_g:], preferred_element_type=F32) + bg_ref[...]

    qn = _rms(ql, gq_ref[...]).astype(BF16)
    kvn = _rms(kvl, gkv_ref[...]).astype(BF16)
    zz = jnp.dot(fin, bd_ref[...], preferred_element_type=F32)
    qab_t = lax.dot_general(wqt_ref[...], qn, nt_dims, preferred_element_type=F32)
    kn = jnp.dot(kvn, wk_ref[...], preferred_element_type=F32)
    vt = lax.dot_general(wvt_ref[...], kvn, nt_dims, preferred_element_type=F32) + vones_ref[...]

    gate_ref[0] = jax.nn.sigmoid(zg).astype(BF16)
    zr_ref[0] = zz[:, :F_WIDTH].astype(BF16)
    zi_ref[0] = zz[:, F_WIDTH:].astype(BF16)
    cq = cos_t * scale
    sq = sin_t * scale
    for h in range(N_HEADS):
        a = qab_t[h * HEAD_PAD:(h + 1) * HEAD_PAD]
        b = qab_t[hp + h * HEAD_PAD:hp + (h + 1) * HEAD_PAD]
        qt_ref[0, h] = (a * cq + b * sq).astype(BF16)
    krr = kr[:, :HEAD_PAD] * cos_t.T + kr[:, HEAD_PAD:] * sin_t.T
    for h in range(N_HEADS):
        k_ref[0, h] = (kn[:, h * HEAD_PAD:(h + 1) * HEAD_PAD] + krr).astype(BF16)
        vt_ref[0, h, 0] = vt[h * V_ROWS:(h + 1) * V_ROWS].astype(BF16)


def _inproj(x, pos_row, sh1, sc1, gmix, wcat, gq, gkv, wqt, wk, wvt, vones, bd, bg, invf, tm):
    bsz, s, d = x.shape
    hp = N_HEADS * HEAD_PAD
    nt = s // tm
    const = lambda shape: pl.BlockSpec(shape, lambda b, i: (0,) * len(shape))
    tok = lambda w: pl.BlockSpec((1, tm, w), lambda b, i: (b, i, 0))
    per_b = pl.BlockSpec((1, 1, d), lambda b, i: (b, 0, 0))
    scale = float((NOPE + ROPE) ** -0.5 * np.log2(np.e))
    return pl.pallas_call(
        functools.partial(_inproj_kernel, scale=scale),
        grid=(bsz, nt),
        in_specs=[tok(d), pl.BlockSpec((1, 1, tm), lambda b, i: (b, 0, i)), per_b, per_b,
                  const((1, d)), const(wcat.shape), const((1, Q_LORA)), const((1, KV_LORA)),
                  const(wqt.shape), const(wk.shape), const(wvt.shape), const(vones.shape),
                  const(bd.shape), const((1, bg.shape[1])), const((ROPE, 1))],
        out_specs=[pl.BlockSpec((1, N_HEADS, HEAD_PAD, tm), lambda b, i: (b, 0, 0, i)),
                   pl.BlockSpec((1, N_HEADS, tm, HEAD_PAD), lambda b, i: (b, 0, i, 0)),
                   pl.BlockSpec((1, N_HEADS, 1, V_ROWS, tm), lambda b, i: (b, 0, i, 0, 0)),
                   tok(F_WIDTH), tok(F_WIDTH), tok(2 * d)],
        out_shape=[jax.ShapeDtypeStruct((bsz, N_HEADS, HEAD_PAD, s), BF16),
                   jax.ShapeDtypeStruct((bsz, N_HEADS, s, HEAD_PAD), BF16),
                   jax.ShapeDtypeStruct((bsz, N_HEADS, nt, V_ROWS, tm), BF16),
                   jax.ShapeDtypeStruct((bsz, s, F_WIDTH), BF16),
                   jax.ShapeDtypeStruct((bsz, s, F_WIDTH), BF16),
                   jax.ShapeDtypeStruct((bsz, s, 2 * d), BF16)],
        compiler_params=_cparams(("parallel", "parallel")),
        name="inproj",
    )(x, pos_row, sh1, sc1, gmix, wcat, gq, gkv, wqt, wk, wvt, vones, bd, bg, invf)


def _attn_kernel(qt_ref, k_ref, vt_ref, o_ref, *p_refs, nk):
    tq = qt_ref.shape[3]
    tk = vt_ref.shape[4]
    qts = [qt_ref[0, hh] for hh in range(2)]
    nbuf = ATTN_DEPTH + 1

    def scores(c, hh):
        ss = []
        for i in range(tk // QK_ROWS):
            off = c * tk + i * QK_ROWS
            ss.append(jnp.dot(k_ref[0, hh, off:off + QK_ROWS, :], qts[hh],
                              preferred_element_type=F32))
        return ss

    def probs(ss, m, p_ref):
        m_new = m
        for s in ss:
            m_new = jnp.maximum(m_new, jnp.max(s, axis=0, keepdims=True))
        for i, s in enumerate(ss):
            p_ref[i * QK_ROWS:(i + 1) * QK_ROWS, :] = jnp.exp2(s - m_new).astype(BF16)
        return m_new, jnp.exp2(m - m_new)

    def accumulate(c, hh, acc, alpha, p_ref):
        return alpha * acc + jnp.dot(vt_ref[0, hh, c], p_ref[...], preferred_element_type=F32)

    hs = range(2)
    buf = lambda c, hh: p_refs[hh * nbuf + c % nbuf]
    m = [jnp.full((1, tq), -jnp.inf, F32) for _ in hs]
    alpha = [[None] * nk for _ in hs]
    accs = [jnp.zeros((V_ROWS, tq), F32) for _ in hs]
    for c in range(nk + ATTN_DEPTH):
        if c < nk:
            s = [scores(c, hh) for hh in hs]
        if c >= ATTN_DEPTH:
            for hh in hs:
                accs[hh] = accumulate(c - ATTN_DEPTH, hh, accs[hh], alpha[hh][c - ATTN_DEPTH],
                                      buf(c - ATTN_DEPTH, hh))
        if c < nk:
            for hh in hs:
                m[hh], alpha[hh][c] = probs(s[hh], m[hh], buf(c, hh))
    out_t = jnp.concatenate([a[:V_DIM] / a[V_DIM:V_DIM + 1] for a in accs], axis=0)
    o_ref[0] = out_t.T.astype(BF16)


def _attention(qt, k, vt, tq):
    bsz, nh, s, _ = k.shape
    nk, tk = vt.shape[2], vt.shape[4]
    return pl.pallas_call(
        functools.partial(_attn_kernel, nk=nk),
        grid=(bsz, nh // 2, s // tq),
        in_specs=[pl.BlockSpec((1, 2, HEAD_PAD, tq), lambda b, j, i: (b, j, 0, i)),
                  pl.BlockSpec((1, 2, s, HEAD_PAD), lambda b, j, i: (b, j, 0, 0)),
                  pl.BlockSpec((1, 2, nk, V_ROWS, tk), lambda b, j, i: (b, j, 0, 0, 0))],
        out_specs=pl.BlockSpec((1, tq, HEAD_PAD), lambda b, j, i: (b, i, j)),
        out_shape=jax.ShapeDtypeStruct((bsz, s, nh * V_DIM), BF16),
        scratch_shapes=[pltpu.VMEM((tk, tq), BF16)] * (2 * (ATTN_DEPTH + 1)),
        compiler_params=_cparams(("parallel", "parallel", "arbitrary")),
        name="attention",
    )(qt, k, vt)


def _fft_a_kernel(zr_ref, zi_ref, m1_ref, br_ref, bi_ref):
    n1 = zr_ref.shape[1]
    ts2 = m1_ref.shape[0]
    for j in range(ts2):
        sl = slice(j * F_WIDTH, (j + 1) * F_WIDTH)
        xx = jnp.concatenate([zr_ref[0, :, sl], zi_ref[0, :, sl]], axis=0)
        y = jnp.dot(m1_ref[j], xx, preferred_element_type=F32)
        br_ref[0, :, sl] = y[:n1].astype(BF16)
        bi_ref[0, :, sl] = y[n1:].astype(BF16)


def _fft_a(zr, zi, m1, ts2):
    bsz, s, _ = zr.shape
    n1 = s // FFT_N2
    zr2 = zr.reshape(bsz, n1, FFT_N2 * F_WIDTH)
    zi2 = zi.reshape(bsz, n1, FFT_N2 * F_WIDTH)
    blk = pl.BlockSpec((1, n1, ts2 * F_WIDTH), lambda b, j: (b, 0, j))
    return pl.pallas_call(
        _fft_a_kernel,
        grid=(bsz, FFT_N2 // ts2),
        in_specs=[blk, blk, pl.BlockSpec((ts2, 2 * n1, 2 * n1), lambda b, j: (j, 0, 0))],
        out_specs=[blk, blk],
        out_shape=[jax.ShapeDtypeStruct(zr2.shape, BF16)] * 2,
        compiler_params=_cparams(("parallel", "parallel")),
        name="fft_a",
    )(zr2, zi2, m1)


def _fft_b_kernel(br_ref, bi_ref, wb_ref, o_ref):
    tk1 = br_ref.shape[1]
    for j in range(tk1):
        xx = jnp.concatenate([br_ref[0, j], bi_ref[0, j]], axis=0)
        y = jnp.dot(wb_ref[...], xx, preferred_element_type=F32)
        o_ref[0, :, j * F_WIDTH:(j + 1) * F_WIDTH] = y.astype(BF16)


def _fft_b(br, bi, wb, tk1):
    bsz, n1, _ = br.shape
    br4 = br.reshape(bsz, n1, FFT_N2, F_WIDTH)
    bi4 = bi.reshape(bsz, n1, FFT_N2, F_WIDTH)
    blk = pl.BlockSpec((1, tk1, FFT_N2, F_WIDTH), lambda b, j: (b, j, 0, 0))
    out = pl.pallas_call(
        _fft_b_kernel,
        grid=(bsz, n1 // tk1),
        in_specs=[blk, blk, pl.BlockSpec((FFT_N2, 2 * FFT_N2), lambda b, j: (0, 0))],
        out_specs=pl.BlockSpec((1, FFT_N2, tk1 * F_WIDTH), lambda b, j: (b, 0, j)),
        out_shape=jax.ShapeDtypeStruct((bsz, FFT_N2, n1 * F_WIDTH), BF16),
        compiler_params=_cparams(("parallel", "parallel")),
        name="fft_b",
    )(br4, bi4, wb)
    return out.reshape(bsz, n1 * FFT_N2, F_WIDTH)


def _dft_constants(s):
    n1, n2 = s // FFT_N2, FFT_N2
    m = np.arange(F_GDIM)
    th = 2.0 * np.pi * np.outer(m, m) / F_GDIM
    eye = np.eye(F_GROUPS)
    bd = np.concatenate([np.kron(eye, np.cos(th)), -np.kron(eye, np.sin(th))], axis=1) / np.sqrt(F_GDIM)
    k1 = np.arange(n1)[None, :, None]
    s1 = np.arange(n1)[None, None, :]
    s2 = np.arange(n2)[:, None, None]
    ph = 2.0 * np.pi * ((k1 * (n2 * s1 + s2)) % s) / s
    cr, ci = np.cos(ph) / np.sqrt(n1), -np.sin(ph) / np.sqrt(n1)
    m1 = np.concatenate([np.concatenate([cr, -ci], axis=2), np.concatenate([ci, cr], axis=2)], axis=1)
    k2 = np.arange(n2)
    tb = 2.0 * np.pi * np.outer(k2, k2) / n2
    wb = np.concatenate([np.cos(tb), np.sin(tb)], axis=1) / np.sqrt(n2)
    return (jnp.asarray(bd, BF16), jnp.asarray(m1, BF16), jnp.asarray(wb, BF16))


def _route(parts, bias):
    logit = parts[:, :LANES] + parts[:, LANES:] + bias
    lane = lax.broadcasted_iota(jnp.int32, logit.shape, 1)
    neg = -jnp.inf
    big = jnp.int32(LANES)

    def softmax_masked(mask):
        z = jnp.where(mask, logit, neg)
        e = jnp.exp(z - jnp.max(z, axis=-1, keepdims=True))
        return e / jnp.sum(e, axis=-1, keepdims=True)

    def top1(p, mask):
        pm = jnp.where(mask, p, -1.0)
        best = jnp.max(pm, axis=-1, keepdims=True)
        idx = jnp.min(jnp.where(pm == best, lane, big), axis=-1, keepdims=True)
        return best, idx

    gmask = (lane >= N_EXPERTS) & (lane < N_EXPERTS + N_GROUPS)
    g_prob = softmax_masked(gmask)
    g_p, g_lane = top1(g_prob, gmask)
    e_lo = (g_lane - N_EXPERTS) * E_PER_GROUP
    emask = (lane >= e_lo) & (lane < e_lo + E_PER_GROUP)
    e_prob = softmax_masked(emask)
    p1, i1 = top1(e_prob, emask)
    p2, i2 = top1(e_prob, emask & (lane != i1))
    den = p1 + p2
    w1 = g_p * (p1 / den)
    w2 = g_p * (p2 / den)
    route = jnp.where(lane == 0, i1.astype(F32), jnp.where(lane == 1, i2.astype(F32),
                      jnp.where(lane == 2, w1, jnp.where(lane == 3, w2, 0.0))))
    cnt = jnp.sum(((lane == i1) | (lane == i2)).astype(F32), axis=0, keepdims=True)
    return route, cnt


def _merge_kernel(at_ref, fo_ref, gate_ref, x_ref, wba_ref, wbf_ref, wout_ref, gt1_ref, sh2_ref,
                  sc2_ref, gffn_ref, wr_ref, br_ref, h1_ref, v2_ref, route_ref, cnt_ref):
    d = x_ref.shape[-1]
    tm = x_ref.shape[1]
    blocks = [slice(i * MERGE_ROWS, (i + 1) * MERGE_ROWS) for i in range(tm // MERGE_ROWS)]
    a = [jnp.dot(at_ref[0, r, :], wba_ref[...], preferred_element_type=F32) for r in blocks]
    fo = [jnp.dot(fo_ref[0, r, :], wbf_ref[...], preferred_element_type=F32) for r in blocks]
    y = []
    for r, ai, fi in zip(blocks, a, fo):
        merged = gate_ref[0, r, :d].astype(F32) * ai + gate_ref[0, r, d:].astype(F32) * fi
        y.append(jnp.dot(merged.astype(BF16), wout_ref[...], preferred_element_type=F32))
    parts = []
    for r, yi in zip(blocks, y):
        h1 = x_ref[0, r, :] + gt1_ref[0] * yi
        h1_ref[0, r, :] = h1
        v2 = _rms(h1, gffn_ref[...]) * (1.0 + sc2_ref[0]) + sh2_ref[0]
        v_hi = v2.astype(BF16)
        v2_ref[0, r, :] = v_hi
        v_lo = (v2 - v_hi.astype(F32)).astype(BF16)
        parts.append(jnp.dot(jnp.concatenate([v_hi, v_lo], axis=1), wr_ref[...],
                             preferred_element_type=F32))
    cnt = None
    for r, pi in zip(blocks, parts):
        route, c = _route(pi, br_ref[...])
        route_ref[0, r, :] = route
        cnt = c if cnt is None else cnt + c
    cnt_ref[0, 0] = cnt


def _merge(attn, four, gates, x, wba, wbf, wout, gt1, sh2, sc2, gffn, wr, br, tm):
    bsz, s, d = x.shape
    const = lambda shape: pl.BlockSpec(shape, lambda b, i: (0,) * len(shape))
    tok = lambda w: pl.BlockSpec((1, tm, w), lambda b, i: (b, i, 0))
    per_b = pl.BlockSpec((1, 1, d), lambda b, i: (b, 0, 0))
    return pl.pallas_call(
        _merge_kernel,
        grid=(bsz, s // tm),
        in_specs=[tok(attn.shape[-1]), tok(F_WIDTH), tok(2 * d), tok(d), const(wba.shape),
                  const(wbf.shape), const(wout.shape), per_b, per_b, per_b, const((1, d)),
                  const(wr.shape), const((1, LANES))],
        out_specs=[tok(d), tok(d), tok(LANES),
                   pl.BlockSpec((1, 1, 1, LANES), lambda b, i: (b, i, 0, 0))],
        out_shape=[jax.ShapeDtypeStruct((bsz, s, d), F32), jax.ShapeDtypeStruct((bsz, s, d), BF16),
                   jax.ShapeDtypeStruct((bsz, s, LANES), F32),
                   jax.ShapeDtypeStruct((bsz, s // tm, 1, LANES), F32)],
        compiler_params=_cparams(("parallel", "parallel")),
        name="merge",
    )(attn, four, gates, x, wba, wbf, wout, gt1, sh2, sc2, gffn, wr, br)


GRAN = 16
MOE_ROW_TILE = 1024


def _moe_metadata(cnt, tm):
    ntile = cnt.shape[0]

    def cumsum(a, axis):
        n = a.shape[axis]
        tri = jnp.asarray(np.triu(np.ones((n, n), np.float32)))
        af = a.astype(F32)
        out = (jnp.dot(af, tri, precision=lax.Precision.HIGHEST) if axis == a.ndim - 1
               else jnp.dot(tri.T, af, precision=lax.Precision.HIGHEST))
        return out.astype(jnp.int32)

    c = cnt[:, :N_EXPERTS].astype(jnp.int32)
    pc = (c + GRAN - 1) // GRAN * GRAN
    loff = cumsum(pc, 1) - pc
    tot = jnp.sum(pc, axis=0)
    tot_r = (tot + MOE_ROW_TILE - 1) // MOE_ROW_TILE * MOE_ROW_TILE
    ends = cumsum(tot_r[None, :], 1)[0]
    base = ends - tot_r
    goff = base[None, :] + cumsum(pc, 0) - pc
    cap_tiles = (2 * ntile * tm + ntile * N_EXPERTS * GRAN) // MOE_ROW_TILE + N_EXPERTS
    n_valid = (ends[-1] // MOE_ROW_TILE).astype(jnp.int32)
    tile_start = jnp.arange(cap_tiles, dtype=jnp.int32) * MOE_ROW_TILE
    tile_expert = jnp.minimum(jnp.sum((ends[None, :] <= tile_start[:, None]).astype(jnp.int32), axis=1),
                              N_EXPERTS - 1)
    flat = lambda a: a.reshape(-1).astype(jnp.int32)
    tail = tot_r - tot + jnp.where(jnp.arange(N_EXPERTS) == N_EXPERTS - 1,
                                   cap_tiles * MOE_ROW_TILE - ends[-1], 0)
    meta = dict(loff=flat(loff), goff=flat(goff), ngran=flat(pc // GRAN),
                ntot=jnp.concatenate([flat(jnp.sum(pc // (2 * GRAN), axis=1)),
                                      flat(jnp.sum(pc // GRAN % 2, axis=1))]),
                tail_start=flat(base + tot), tail_n=flat(tail // GRAN),
                tile_expert=tile_expert, n_valid=n_valid.reshape(1))
    return meta, loff.astype(F32), cap_tiles


def _run_copies(i, loff_s, goff_s, ngran_s, make_copy):
    def per_expert(e, carry):
        n = ngran_s[i * N_EXPERTS + e]
        lo = loff_s[i * N_EXPERTS + e]
        go = goff_s[i * N_EXPERTS + e]

        def per_pair(g, c):
            make_copy(pl.multiple_of(lo + g * 2 * GRAN, GRAN),
                      pl.multiple_of(go + g * 2 * GRAN, GRAN), 2).start()
            return c

        lax.fori_loop(0, n // 2, per_pair, 0)

        @pl.when(n % 2 == 1)
        def _():
            make_copy(pl.multiple_of(lo + (n - 1) * GRAN, GRAN),
                      pl.multiple_of(go + (n - 1) * GRAN, GRAN), 1).start()

        return carry

    lax.fori_loop(0, N_EXPERTS, per_expert, 0)


def _wait_copies(n_pairs, n_singles, make_copy):
    def wait(k):
        def one(_, carry):
            make_copy(0, 0, k).wait()
            return carry
        return one
    lax.fori_loop(0, n_pairs, wait(2), 0)
    lax.fori_loop(0, n_singles, wait(1), 0)


def _local_positions(route, loffv, ltri):
    lane = lax.broadcasted_iota(jnp.int32, route.shape, 1).astype(F32)
    o1 = lane == route[:, 0:1]
    o2 = lane == route[:, 1:2]
    before = jnp.dot(ltri, (o1 | o2).astype(BF16), preferred_element_type=F32)
    start = before + loffv
    pos1 = jnp.sum(jnp.where(o1, start, 0.0), axis=-1, keepdims=True)
    pos2 = jnp.sum(jnp.where(o2, start, 0.0), axis=-1, keepdims=True)
    return pos1, pos2


def _dispatch_kernel(loff_s, goff_s, ngran_s, ntot_s, tstart_s, tn_s, v2_ref, route_ref, loffv_ref,
                     ltri_ref, xs_ref, posw_ref, loc_ref, zero_ref, sem):
    i = pl.program_id(0)
    last = pl.num_programs(0) - 1
    slot = i % 2
    rows = loc_ref.shape[1]
    route = route_ref[...]
    pos1, pos2 = _local_positions(route, loffv_ref[0], ltri_ref[...])
    lane = lax.broadcasted_iota(jnp.int32, route.shape, 1)
    posw = jnp.where(lane == 0, pos1, jnp.where(lane == 1, pos2, route))
    posw_ref[...] = posw
    pos_t = posw.T
    riota = lax.broadcasted_iota(jnp.int32, (rows, 1), 0).astype(F32)
    onehot = ((riota == pos_t[0:1]) | (riota == pos_t[1:2])).astype(BF16)
    loc_ref[slot] = jnp.dot(onehot, v2_ref[...], preferred_element_type=F32).astype(BF16)

    def out_copy(s):
        return lambda lo, go, k: pltpu.make_async_copy(
            loc_ref.at[s, pl.ds(lo, k * GRAN)], xs_ref.at[pl.ds(go, k * GRAN)], sem.at[s])

    _run_copies(i, loff_s, goff_s, ngran_s, out_copy(slot))
    ntile = pl.num_programs(0)

    @pl.when(i > 0)
    def _():
        _wait_copies(ntot_s[i - 1], ntot_s[ntile + i - 1], out_copy(1 - slot))

    @pl.when(i == last)
    def _():
        _wait_copies(ntot_s[i], ntot_s[ntile + i], out_copy(slot))
        zero_ref[...] = jnp.zeros(zero_ref.shape, BF16)

        def zero_copy(go):
            return pltpu.make_async_copy(zero_ref, xs_ref.at[pl.ds(go, GRAN)], sem.at[0])

        def per_expert(e, n_started):
            def per_granule(g, carry):
                zero_copy(pl.multiple_of(tstart_s[e] + g * GRAN, GRAN)).start()
                return carry
            lax.fori_loop(0, tn_s[e], per_granule, 0)
            return n_started + tn_s[e]

        nz = lax.fori_loop(0, N_EXPERTS, per_expert, 0)
        _wait_copies(0, nz, lambda lo, go, k: zero_copy(go))


def _dispatch(v2, route, meta, loffv, ltri, cap_tiles, tm, rows):
    t, d = v2.shape
    ntile = t // tm
    grid_spec = pltpu.PrefetchScalarGridSpec(
        num_scalar_prefetch=6,
        grid=(ntile,),
        in_specs=[pl.BlockSpec((tm, d), lambda i, *_: (i, 0)),
                  pl.BlockSpec((tm, LANES), lambda i, *_: (i, 0)),
                  pl.BlockSpec((1, 1, LANES), lambda i, *_: (i, 0, 0)),
                  pl.BlockSpec((tm, tm), lambda i, *_: (0, 0))],
        out_specs=[pl.BlockSpec(memory_space=pl.ANY),
                   pl.BlockSpec((tm, LANES), lambda i, *_: (i, 0))],
        scratch_shapes=[pltpu.VMEM((2, rows, d), BF16), pltpu.VMEM((GRAN, d), BF16),
                        pltpu.SemaphoreType.DMA((2,))],
    )
    return pl.pallas_call(
        _dispatch_kernel,
        grid_spec=grid_spec,
        out_shape=[jax.ShapeDtypeStruct((cap_tiles * MOE_ROW_TILE, d), BF16),
                   jax.ShapeDtypeStruct((t, LANES), F32)],
        compiler_params=_cparams(("arbitrary",)),
        name="dispatch",
    )(meta["loff"], meta["goff"], meta["ngran"], meta["ntot"], meta["tail_start"], meta["tail_n"],
      v2, route, loffv, ltri)


def _experts_kernel(texp_s, nvalid_s, x_ref, wgu_ref, wd_ref, y_ref):
    del texp_s

    @pl.when(pl.program_id(0) < nvalid_s[0])
    def _():
        hc = jnp.dot(x_ref[...], wgu_ref[0], preferred_element_type=F32)
        g = hc[:, :D_EXPERT]
        hid = (g * jax.nn.sigmoid(g) * hc[:, D_EXPERT:]).astype(BF16)
        y_ref[...] = jnp.dot(hid, wd_ref[0], preferred_element_type=F32).astype(BF16)

    @pl.when(pl.program_id(0) >= nvalid_s[0])
    def _():
        y_ref[...] = jnp.zeros(y_ref.shape, BF16)


def _experts(xs, wgu, wd, meta, cap_tiles):
    d = xs.shape[1]
    row = lambda j, texp, nv: (jnp.minimum(j, nv[0] - 1), 0)
    wsel = lambda j, texp, nv: (texp[jnp.minimum(j, nv[0] - 1)], 0, 0)
    grid_spec = pltpu.PrefetchScalarGridSpec(
        num_scalar_prefetch=2,
        grid=(cap_tiles,),
        in_specs=[pl.BlockSpec((MOE_ROW_TILE, d), row),
                  pl.BlockSpec((1, d, 2 * D_EXPERT), wsel),
                  pl.BlockSpec((1, D_EXPERT, d), wsel)],
        out_specs=pl.BlockSpec((MOE_ROW_TILE, d), lambda j, texp, nv: (j, 0)),
    )
    return pl.pallas_call(
        _experts_kernel,
        grid_spec=grid_spec,
        out_shape=jax.ShapeDtypeStruct(xs.shape, BF16),
        compiler_params=_cparams(("arbitrary",)),
        name="experts",
    )(meta["tile_expert"], meta["n_valid"], xs, wgu, wd)


def _combine_kernel(loff_s, goff_s, ngran_s, ntot_s, posw_ref, ys_ref, h1_ref, gt2_ref, shf_ref,
                    scf_ref, gfin_ref, o_ref, loc_ref, sem):
    i = pl.program_id(0)
    slot = i % 2
    rows = loc_ref.shape[1]

    def in_copy(s):
        return lambda lo, go, k: pltpu.make_async_copy(
            ys_ref.at[pl.ds(go, k * GRAN)], loc_ref.at[s, pl.ds(lo, k * GRAN)], sem.at[s])

    def fetch(tile, s):
        loc_ref[s] = jnp.zeros(loc_ref.shape[1:], BF16)
        _run_copies(tile, loff_s, goff_s, ngran_s, in_copy(s))

    @pl.when(i == 0)
    def _():
        fetch(0, 0)

    @pl.when(i + 1 < pl.num_programs(0))
    def _():
        fetch(i + 1, 1 - slot)

    _wait_copies(ntot_s[i], ntot_s[pl.num_programs(0) + i], in_copy(slot))

    posw = posw_ref[...]
    riota = lax.broadcasted_iota(jnp.int32, (1, rows), 1).astype(F32)
    qw = (jnp.where(riota == posw[:, 0:1], posw[:, 2:3], 0.0)
          + jnp.where(riota == posw[:, 1:2], posw[:, 3:4], 0.0)).astype(BF16)
    moe = jnp.dot(qw, loc_ref[slot], preferred_element_type=F32)
    h2 = h1_ref[...] + gt2_ref[0] * moe
    o_ref[...] = _rms(h2, gfin_ref[...]) * (1.0 + scf_ref[0]) + shf_ref[0]


def _combine(posw, ys, h1, gt2, shf, scf, gfin, meta, tm, rows, tiles_per_batch):
    t, d = h1.shape
    per_b = pl.BlockSpec((1, 1, d), lambda i, *_: (i // tiles_per_batch, 0, 0))
    grid_spec = pltpu.PrefetchScalarGridSpec(
        num_scalar_prefetch=4,
        grid=(t // tm,),
        in_specs=[pl.BlockSpec((tm, LANES), lambda i, *_: (i, 0)),
                  pl.BlockSpec(memory_space=pl.ANY),
                  pl.BlockSpec((tm, d), lambda i, *_: (i, 0)),
                  per_b, per_b, per_b, pl.BlockSpec((1, d), lambda i, *_: (0, 0))],
        out_specs=pl.BlockSpec((tm, d), lambda i, *_: (i, 0)),
        scratch_shapes=[pltpu.VMEM((2, rows, d), BF16), pltpu.SemaphoreType.DMA((2,))],
    )
    return pl.pallas_call(
        _combine_kernel,
        grid_spec=grid_spec,
        out_shape=jax.ShapeDtypeStruct((t, d), F32),
        compiler_params=_cparams(("arbitrary",)),
        name="combine",
    )(meta["loff"], meta["goff"], meta["ngran"], meta["ntot"], posw, ys, h1, gt2, shf, scf, gfin)


def _rot_cols(w):
    half = ROPE // 2
    return jnp.concatenate([-w[..., half:], w[..., :half]], axis=-1)


def _pad_cols(w, lo, width):
    return jnp.pad(w, [(0, 0)] * (w.ndim - 1) + [(lo, width - lo - w.shape[-1])])


def _prep_weights(w_in, w_uq, w_uk, w_uv):
    o2, o3 = Q_LORA + KV_LORA, Q_LORA + KV_LORA + ROPE
    w_kr = w_in[:, o2:o3]
    wcat = jnp.concatenate([
        w_in[:, :o2],
        _pad_cols(w_kr, NOPE, HEAD_PAD), _pad_cols(_rot_cols(w_kr), NOPE, HEAD_PAD),
        w_in[:, o3:]], axis=1).astype(BF16)
    wq_a = _pad_cols(w_uq, 0, HEAD_PAD)
    wq_b = _pad_cols(_rot_cols(w_uq[..., NOPE:]), NOPE, HEAD_PAD)
    r = w_uq.shape[0]
    wqt = jnp.concatenate([wq_a.reshape(r, -1), wq_b.reshape(r, -1)], axis=1).T.astype(BF16)
    wk = _pad_cols(w_uk, 0, HEAD_PAD).reshape(r, -1).astype(BF16)
    wvt = _pad_cols(w_uv, 0, V_ROWS).reshape(r, -1).T.astype(BF16)
    ones = np.zeros((N_HEADS, V_ROWS), np.float32)
    ones[:, V_DIM] = 1.0
    return wcat, wqt, wk, wvt, jnp.asarray(ones.reshape(-1, 1))


def kernel(x, c, positions, w_ada, b_ada, g_norm_mix, w_in, g_q_lat, g_kv_lat, w_uq, w_uk, w_uv, w_branch_attn, w_branch_fourier, b_gates, w_out, g_norm_ffn, w_router_group, b_router_group, w_router_expert, b_router_expert, w_expert_gate, w_expert_up, w_expert_down, w_ada_final, b_ada_final, g_norm_final):
    bsz, s, d = x.shape
    assert w_ada.shape[0] == 1 and s % FFT_N2 == 0
    tm = min(512, s)

    mod = _mods(c, w_ada[0], b_ada[0])
    fmod = _mods(c, w_ada_final, b_ada_final)
    sh1, sc1, gt1, sh2, sc2, gt2 = [m.reshape(bsz, 1, d) for m in jnp.split(mod, 6, axis=-1)]
    shf, scf = [m.reshape(bsz, 1, d) for m in jnp.split(fmod, 2, axis=-1)]

    wcat, wqt, wk, wvt, vones = _prep_weights(w_in[0], w_uq[0], w_uk[0], w_uv[0])
    bd, m1, wb = _dft_constants(s)
    inv_freq = ROPE_THETA ** (-jnp.arange(0, ROPE, 2, dtype=F32) / ROPE)
    invf = jnp.concatenate([inv_freq, inv_freq]).reshape(ROPE, 1)

    qt, k, vt, zr, zi, gates = _inproj(
        x, positions.reshape(bsz, 1, s), sh1, sc1, g_norm_mix[0][None, :], wcat,
        g_q_lat[0][None, :], g_kv_lat[0][None, :], wqt, wk, wvt, vones, bd, b_gates[0][None, :],
        invf, tm)

    attn = _attention(qt, k, vt, tq=min(256, s))
    br, bi = _fft_a(zr, zi, m1, ts2=8)
    four = _fft_b(br, bi, wb, tk1=16)

    wr = jnp.concatenate([w_router_expert[0].reshape(d, N_EXPERTS), w_router_group[0]], axis=1)
    wr = _pad_cols(wr, 0, LANES)
    wr_top = lax.bitcast_convert_type(
        lax.bitcast_convert_type(wr, jnp.uint32) & jnp.uint32(0xFFFF0000), F32)
    wr_hi = wr_top.astype(BF16)
    wr_lo = (wr - wr_top).astype(BF16)
    wr = jnp.concatenate([jnp.concatenate([wr_hi, wr_lo], axis=1),
                          jnp.concatenate([wr_hi, jnp.zeros_like(wr_lo)], axis=1)], axis=0)
    brt = _pad_cols(jnp.concatenate([b_router_expert[0].reshape(-1), b_router_group[0]])[None, :], 0, LANES)
    h1, v2, route, cnt = _merge(attn, four, gates, x, w_branch_attn[0].astype(BF16),
                                w_branch_fourier[0].astype(BF16), w_out[0].astype(BF16), gt1, sh2,
                                sc2, g_norm_ffn[0][None, :], wr, brt, tm)

    t = bsz * s
    meta, loffv, cap_tiles = _moe_metadata(cnt.reshape(-1, LANES), tm)
    rows = 2 * tm + N_EXPERTS * GRAN
    loffv = _pad_cols(loffv, 0, LANES).reshape(-1, 1, LANES)
    ltri = jnp.asarray(np.tril(np.ones((tm, tm), np.float32), -1), BF16)
    xs, posw = _dispatch(v2.reshape(t, d), route.reshape(t, LANES), meta, loffv, ltri, cap_tiles,
                         tm, rows)
    wgu = jnp.concatenate([w_expert_gate[0], w_expert_up[0]], axis=-1).astype(BF16)
    wd = w_expert_down[0].astype(BF16)
    ys = _experts(xs, wgu, wd, meta, cap_tiles)
    out = _combine(posw, ys, h1.reshape(t, d), gt2, shf, scf, g_norm_final[None, :], meta, tm, rows,
                   s // tm)
    return out.reshape(bsz, s, d)
```

```python
import functools

import numpy as np
import jax
import jax.numpy as jnp
from jax import lax
from jax.experimental import pallas as pl
from jax.experimental.pallas import tpu as pltpu

F32 = jnp.float32
BF16 = jnp.bfloat16

N_HEADS = 8
NOPE = 64
ROPE = 32
V_DIM = 64
HEAD_PAD = 128
V_ROWS = 96
Q_LORA = 256
KV_LORA = 256
ROPE_THETA = 10000.0
F_GROUPS = 8
F_GDIM = 64
F_WIDTH = F_GROUPS * F_GDIM
N_GROUPS = 4
E_PER_GROUP = 8
N_EXPERTS = N_GROUPS * E_PER_GROUP
D_EXPERT = 256
EPS = 1e-6
FFT_N2 = 64
LANES = 128
MERGE_ROWS = 128
ATTN_DEPTH = 1
QK_ROWS = 256
V7X_VMEM_BYTES = 64 * 1024 * 1024
VMEM_LIMIT = V7X_VMEM_BYTES - 8 * 1024 * 1024


def _cparams(sem):
    return pltpu.CompilerParams(dimension_semantics=sem, vmem_limit_bytes=VMEM_LIMIT)


def _mods_kernel(c_ref, w_ref, b_ref, o_ref):
    c = c_ref[...]
    ca = c * jax.nn.sigmoid(c)
    o_ref[...] = jnp.dot(ca, w_ref[...], preferred_element_type=F32,
                         precision=lax.Precision.HIGHEST) + b_ref[...]


def _mods(c, w, b):
    bsz, d = c.shape
    n = w.shape[1]
    tn = 1024
    return pl.pallas_call(
        _mods_kernel,
        grid=(n // tn,),
        in_specs=[pl.BlockSpec((bsz, d), lambda j: (0, 0)),
                  pl.BlockSpec((d, tn), lambda j: (0, j)),
                  pl.BlockSpec((1, tn), lambda j: (0, j))],
        out_specs=pl.BlockSpec((bsz, tn), lambda j: (0, j)),
        out_shape=jax.ShapeDtypeStruct((bsz, n), F32),
        compiler_params=_cparams(("arbitrary",)),
        name="mods",
    )(c, w, b.reshape(1, n))


def _rms(x, g):
    r = lax.rsqrt(jnp.mean(x * x, axis=-1, keepdims=True) + EPS)
    return x * r * g


def _inproj_kernel(x_ref, pos_ref, sh_ref, sc_ref, gmix_ref, wcat_ref, gq_ref, gkv_ref,
                   wqt_ref, wk_ref, wvt_ref, vones_ref, bd_ref, bg_ref, invf_ref,
                   qt_ref, k_ref, vt_ref, zr_ref, zi_ref, gate_ref, *, scale):
    hp = N_HEADS * HEAD_PAD
    nt_dims = (((1,), (1,)), ((), ()))
    u = _rms(x_ref[0], gmix_ref[...]) * (1.0 + sc_ref[0]) + sh_ref[0]
    ub = u.astype(BF16)

    o_q, o_kv, o_kr, o_f, o_g = 0, Q_LORA, Q_LORA + KV_LORA, Q_LORA + KV_LORA + 2 * HEAD_PAD, \
        Q_LORA + KV_LORA + 2 * HEAD_PAD + F_WIDTH

    tm = pos_ref.shape[-1]
    ang = invf_ref[...] * pos_ref[0].astype(F32)
    tail = jnp.zeros((HEAD_PAD - NOPE - ROPE, tm), F32)
    cos_t = jnp.concatenate([jnp.ones((NOPE, tm), F32), jnp.cos(ang), tail], axis=0)
    sin_t = jnp.concatenate([jnp.zeros((NOPE, tm), F32), jnp.sin(ang), tail], axis=0)

    ql = jnp.dot(ub, wcat_ref[:, o_q:o_kv], preferred_element_type=F32)
    kvl = jnp.dot(ub, wcat_ref[:, o_kv:o_kr], preferred_element_type=F32)
    kr = jnp.dot(ub, wcat_ref[:, o_kr:o_f], preferred_element_type=F32)
    fin = jnp.dot(ub, wcat_ref[:, o_f:o_g], preferred_element_type=F32).astype(BF16)
    zg = jnp.dot(ub, wcat_ref[:, o_g:], preferred_element_type=F32) + bg_ref[...]

    qn = _rms(ql, gq_ref[...]).astype(BF16)
    kvn = _rms(kvl, gkv_ref[...]).astype(BF16)
    zz = jnp.dot(fin, bd_ref[...], preferred_element_type=F32)
    qab_t = lax.dot_general(wqt_ref[...], qn, nt_dims, preferred_element_type=F32)
    kn = jnp.dot(kvn, wk_ref[...], preferred_element_type=F32)
    vt = lax.dot_general(wvt_ref[...], kvn, nt_dims, preferred_element_type=F32) + vones_ref[...]

    gate_ref[0] = jax.nn.sigmoid(zg).astype(BF16)
    zr_ref[0] = zz[:, :F_WIDTH].astype(BF16)
    zi_ref[0] = zz[:, F_WIDTH:].astype(BF16)
    cq = cos_t * scale
    sq = sin_t * scale
    for h in range(N_HEADS):
        a = qab_t[h * HEAD_PAD:(h + 1) * HEAD_PAD]
        b = qab_t[hp + h * HEAD_PAD:hp + (h + 1) * HEAD_PAD]
        qt_ref[0, h] = (a * cq + b * sq).astype(BF16)
    krr = kr[:, :HEAD_PAD] * cos_t.T + kr[:, HEAD_PAD:] * sin_t.T
    for h in range(N_HEADS):
        k_ref[0, h] = (kn[:, h * HEAD_PAD:(h + 1) * HEAD_PAD] + krr).astype(BF16)
        vt_ref[0, h, 0] = vt[h * V_ROWS:(h + 1) * V_ROWS].astype(BF16)


def _inproj(x, pos_row, sh1, sc1, gmix, wcat, gq, gkv, wqt, wk, wvt, vones, bd, bg, invf, tm):
    bsz, s, d = x.shape
    hp = N_HEADS * HEAD_PAD
    nt = s // tm
    const = lambda shape: pl.BlockSpec(shape, lambda b, i: (0,) * len(shape))
    tok = lambda w: pl.BlockSpec((1, tm, w), lambda b, i: (b, i, 0))
    per_b = pl.BlockSpec((1, 1, d), lambda b, i: (b, 0, 0))
    scale = float((NOPE + ROPE) ** -0.5 * np.log2(np.e))
    return pl.pallas_call(
        functools.partial(_inproj_kernel, scale=scale),
        grid=(bsz, nt),
        in_specs=[tok(d), pl.BlockSpec((1, 1, tm), lambda b, i: (b, 0, i)), per_b, per_b,
                  const((1, d)), const(wcat.shape), const((1, Q_LORA)), const((1, KV_LORA)),
                  const(wqt.shape), const(wk.shape), const(wvt.shape), const(vones.shape),
                  const(bd.shape), const((1, bg.shape[1])), const((ROPE, 1))],
        out_specs=[pl.BlockSpec((1, N_HEADS, HEAD_PAD, tm), lambda b, i: (b, 0, 0, i)),
                   pl.BlockSpec((1, N_HEADS, tm, HEAD_PAD), lambda b, i: (b, 0, i, 0)),
                   pl.BlockSpec((1, N_HEADS, 1, V_ROWS, tm), lambda b, i: (b, 0, i, 0, 0)),
                   tok(F_WIDTH), tok(F_WIDTH), tok(2 * d)],
        out_shape=[jax.ShapeDtypeStruct((bsz, N_HEADS, HEAD_PAD, s), BF16),
                   jax.ShapeDtypeStruct((bsz, N_HEADS, s, HEAD_PAD), BF16),
                   jax.ShapeDtypeStruct((bsz, N_HEADS, nt, V_ROWS, tm), BF16),
                   jax.ShapeDtypeStruct((bsz, s, F_WIDTH), BF16),
                   jax.ShapeDtypeStruct((bsz, s, F_WIDTH), BF16),
                   jax.ShapeDtypeStruct((bsz, s, 2 * d), BF16)],
        compiler_params=_cparams(("parallel", "parallel")),
        name="inproj",
    )(x, pos_row, sh1, sc1, gmix, wcat, gq, gkv, wqt, wk, wvt, vones, bd, bg, invf)


def _attn_kernel(qt_ref, k_ref, vt_ref, o_ref, *p_refs, nk):
    tq = qt_ref.shape[3]
    tk = vt_ref.shape[4]
    qts = [qt_ref[0, hh] for hh in range(2)]
    nbuf = ATTN_DEPTH + 1

    def scores(c, hh):
        ss = []
        for i in range(tk // QK_ROWS):
            off = c * tk + i * QK_ROWS
            ss.append(jnp.dot(k_ref[0, hh, off:off + QK_ROWS, :], qts[hh],
                              preferred_element_type=F32))
        return ss

    def probs(ss, m, p_ref):
        m_new = m
        for s in ss:
            m_new = jnp.maximum(m_new, jnp.max(s, axis=0, keepdims=True))
        for i, s in enumerate(ss):
            p_ref[i * QK_ROWS:(i + 1) * QK_ROWS, :] = jnp.exp2(s - m_new).astype(BF16)
        return m_new, jnp.exp2(m - m_new)

    def accumulate(c, hh, acc, alpha, p_ref):
        return alpha * acc + jnp.dot(vt_ref[0, hh, c], p_ref[...], preferred_element_type=F32)

    hs = range(2)
    buf = lambda c, hh: p_refs[hh * nbuf + c % nbuf]
    m = [jnp.full((1, tq), -jnp.inf, F32) for _ in hs]
    alpha = [[None] * nk for _ in hs]
    accs = [jnp.zeros((V_ROWS, tq), F32) for _ in hs]
    for c in range(nk + ATTN_DEPTH):
        if c < nk:
            s = [scores(c, hh) for hh in hs]
        if c >= ATTN_DEPTH:
            for hh in hs:
                accs[hh] = accumulate(c - ATTN_DEPTH, hh, accs[hh], alpha[hh][c - ATTN_DEPTH],
                                      buf(c - ATTN_DEPTH, hh))
        if c < nk:
            for hh in hs:
                m[hh], alpha[hh][c] = probs(s[hh], m[hh], buf(c, hh))
    out_t = jnp.concatenate([a[:V_DIM] / a[V_DIM:V_DIM + 1] for a in accs], axis=0)
    o_ref[0] = out_t.T.astype(BF16)


def _attention(qt, k, vt, tq):
    bsz, nh, s, _ = k.shape
    nk, tk = vt.shape[2], vt.shape[4]
    return pl.pallas_call(
        functools.partial(_attn_kernel, nk=nk),
        grid=(bsz, nh // 2, s // tq),
        in_specs=[pl.BlockSpec((1, 2, HEAD_PAD, tq), lambda b, j, i: (b, j, 0, i)),
                  pl.BlockSpec((1, 2, s, HEAD_PAD), lambda b, j, i: (b, j, 0, 0)),
                  pl.BlockSpec((1, 2, nk, V_ROWS, tk), lambda b, j, i: (b, j, 0, 0, 0))],
        out_specs=pl.BlockSpec((1, tq, HEAD_PAD), lambda b, j, i: (b, i, j)),
        out_shape=jax.ShapeDtypeStruct((bsz, s, nh * V_DIM), BF16),
        scratch_shapes=[pltpu.VMEM((tk, tq), BF16)] * (2 * (ATTN_DEPTH + 1)),
        compiler_params=_cparams(("parallel", "parallel", "arbitrary")),
        name="attention",
    )(qt, k, vt)


def _fft_a_kernel(zr_ref, zi_ref, m1_ref, br_ref, bi_ref):
    n1 = zr_ref.shape[1]
    ts2 = m1_ref.shape[0]
    for j in range(ts2):
        sl = slice(j * F_WIDTH, (j + 1) * F_WIDTH)
        xx = jnp.concatenate([zr_ref[0, :, sl], zi_ref[0, :, sl]], axis=0)
        y = jnp.dot(m1_ref[j], xx, preferred_element_type=F32)
        br_ref[0, :, sl] = y[:n1].astype(BF16)
        bi_ref[0, :, sl] = y[n1:].astype(BF16)


def _fft_a(zr, zi, m1, ts2):
    bsz, s, _ = zr.shape
    n1 = s // FFT_N2
    zr2 = zr.reshape(bsz, n1, FFT_N2 * F_WIDTH)
    zi2 = zi.reshape(bsz, n1, FFT_N2 * F_WIDTH)
    blk = pl.BlockSpec((1, n1, ts2 * F_WIDTH), lambda b, j: (b, 0, j))
    return pl.pallas_call(
        _fft_a_kernel,
        grid=(bsz, FFT_N2 // ts2),
        in_specs=[blk, blk, pl.BlockSpec((ts2, 2 * n1, 2 * n1), lambda b, j: (j, 0, 0))],
        out_specs=[blk, blk],
        out_shape=[jax.ShapeDtypeStruct(zr2.shape, BF16)] * 2,
        compiler_params=_cparams(("parallel", "parallel")),
        name="fft_a",
    )(zr2, zi2, m1)


def _fft_b_kernel(br_ref, bi_ref, wb_ref, o_ref):
    tk1 = br_ref.shape[1]
    for j in range(tk1):
        xx = jnp.concatenate([br_ref[0, j], bi_ref[0, j]], axis=0)
        y = jnp.dot(wb_ref[...], xx, preferred_element_type=F32)
        o_ref[0, :, j * F_WIDTH:(j + 1) * F_WIDTH] = y.astype(BF16)


def _fft_b(br, bi, wb, tk1):
    bsz, n1, _ = br.shape
    br4 = br.reshape(bsz, n1, FFT_N2, F_WIDTH)
    bi4 = bi.reshape(bsz, n1, FFT_N2, F_WIDTH)
    blk = pl.BlockSpec((1, tk1, FFT_N2, F_WIDTH), lambda b, j: (b, j, 0, 0))
    out = pl.pallas_call(
        _fft_b_kernel,
        grid=(bsz, n1 // tk1),
        in_specs=[blk, blk, pl.BlockSpec((FFT_N2, 2 * FFT_N2), lambda b, j: (0, 0))],
        out_specs=pl.BlockSpec((1, FFT_N2, tk1 * F_WIDTH), lambda b, j: (b, 0, j)),
        out_shape=jax.ShapeDtypeStruct((bsz, FFT_N2, n1 * F_WIDTH), BF16),
        compiler_params=_cparams(("parallel", "parallel")),
        name="fft_b",
    )(br4, bi4, wb)
    return out.reshape(bsz, n1 * FFT_N2, F_WIDTH)


def _dft_constants(s):
    n1, n2 = s // FFT_N2, FFT_N2
    m = np.arange(F_GDIM)
    th = 2.0 * np.pi * np.outer(m, m) / F_GDIM
    eye = np.eye(F_GROUPS)
    bd = np.concatenate([np.kron(eye, np.cos(th)), -np.kron(eye, np.sin(th))], axis=1) / np.sqrt(F_GDIM)
    k1 = np.arange(n1)[None, :, None]
    s1 = np.arange(n1)[None, None, :]
    s2 = np.arange(n2)[:, None, None]
    ph = 2.0 * np.pi * ((k1 * (n2 * s1 + s2)) % s) / s
    cr, ci = np.cos(ph) / np.sqrt(n1), -np.sin(ph) / np.sqrt(n1)
    m1 = np.concatenate([np.concatenate([cr, -ci], axis=2), np.concatenate([ci, cr], axis=2)], axis=1)
    k2 = np.arange(n2)
    tb = 2.0 * np.pi * np.outer(k2, k2) / n2
    wb = np.concatenate([np.cos(tb), np.sin(tb)], axis=1) / np.sqrt(n2)
    return (jnp.asarray(bd, BF16), jnp.asarray(m1, BF16), jnp.asarray(wb, BF16))


def _route(parts, bias):
    logit = parts[:, :LANES] + parts[:, LANES:] + bias
    lane = lax.broadcasted_iota(jnp.int32, logit.shape, 1)
    neg = -jnp.inf
    big = jnp.int32(LANES)

    def softmax_masked(mask):
        z = jnp.where(mask, logit, neg)
        e = jnp.exp(z - jnp.max(z, axis=-1, keepdims=True))
        return e / jnp.sum(e, axis=-1, keepdims=True)

    def top1(p, mask):
        pm = jnp.where(mask, p, -1.0)
        best = jnp.max(pm, axis=-1, keepdims=True)
        idx = jnp.min(jnp.where(pm == best, lane, big), axis=-1, keepdims=True)
        return best, idx

    gmask = (lane >= N_EXPERTS) & (lane < N_EXPERTS + N_GROUPS)
    g_prob = softmax_masked(gmask)
    g_p, g_lane = top1(g_prob, gmask)
    e_lo = (g_lane - N_EXPERTS) * E_PER_GROUP
    emask = (lane >= e_lo) & (lane < e_lo + E_PER_GROUP)
    e_prob = softmax_masked(emask)
    p1, i1 = top1(e_prob, emask)
    p2, i2 = top1(e_prob, emask & (lane != i1))
    den = p1 + p2
    w1 = g_p * (p1 / den)
    w2 = g_p * (p2 / den)
    route = jnp.where(lane == 0, i1.astype(F32), jnp.where(lane == 1, i2.astype(F32),
                      jnp.where(lane == 2, w1, jnp.where(lane == 3, w2, 0.0))))
    cnt = jnp.sum(((lane == i1) | (lane == i2)).astype(F32), axis=0, keepdims=True)
    return route, cnt


def _merge_kernel(at_ref, fo_ref, gate_ref, x_ref, wba_ref, wbf_ref, wout_ref, gt1_ref, sh2_ref,
                  sc2_ref, gffn_ref, wr_ref, br_ref, h1_ref, v2_ref, route_ref, cnt_ref):
    d = x_ref.shape[-1]
    tm = x_ref.shape[1]
    blocks = [slice(i * MERGE_ROWS, (i + 1) * MERGE_ROWS) for i in range(tm // MERGE_ROWS)]
    a = [jnp.dot(at_ref[0, r, :], wba_ref[...], preferred_element_type=F32) for r in blocks]
    fo = [jnp.dot(fo_ref[0, r, :], wbf_ref[...], preferred_element_type=F32) for r in blocks]
    y = []
    for r, ai, fi in zip(blocks, a, fo):
        merged = gate_ref[0, r, :d].astype(F32) * ai + gate_ref[0, r, d:].astype(F32) * fi
        y.append(jnp.dot(merged.astype(BF16), wout_ref[...], preferred_element_type=F32))
    parts = []
    for r, yi in zip(blocks, y):
        h1 = x_ref[0, r, :] + gt1_ref[0] * yi
        h1_ref[0, r, :] = h1
        v2 = _rms(h1, gffn_ref[...]) * (1.0 + sc2_ref[0]) + sh2_ref[0]
        v_hi = v2.astype(BF16)
        v2_ref[0, r, :] = v_hi
        v_lo = (v2 - v_hi.astype(F32)).astype(BF16)
        parts.append(jnp.dot(jnp.concatenate([v_hi, v_lo], axis=1), wr_ref[...],
                             preferred_element_type=F32))
    cnt = None
    for r, pi in zip(blocks, parts):
        route, c = _route(pi, br_ref[...])
        route_ref[0, r, :] = route
        cnt = c if cnt is None else cnt + c
    cnt_ref[0, 0] = cnt


def _merge(attn, four, gates, x, wba, wbf, wout, gt1, sh2, sc2, gffn, wr, br, tm):
    bsz, s, d = x.shape
    const = lambda shape: pl.BlockSpec(shape, lambda b, i: (0,) * len(shape))
    tok = lambda w: pl.BlockSpec((1, tm, w), lambda b, i: (b, i, 0))
    per_b = pl.BlockSpec((1, 1, d), lambda b, i: (b, 0, 0))
    return pl.pallas_call(
        _merge_kernel,
        grid=(bsz, s // tm),
        in_specs=[tok(attn.shape[-1]), tok(F_WIDTH), tok(2 * d), tok(d), const(wba.shape),
                  const(wbf.shape), const(wout.shape), per_b, per_b, per_b, const((1, d)),
                  const(wr.shape), const((1, LANES))],
        out_specs=[tok(d), tok(d), tok(LANES),
                   pl.BlockSpec((1, 1, 1, LANES), lambda b, i: (b, i, 0, 0))],
        out_shape=[jax.ShapeDtypeStruct((bsz, s, d), F32), jax.ShapeDtypeStruct((bsz, s, d), BF16),
                   jax.ShapeDtypeStruct((bsz, s, LANES), F32),
                   jax.ShapeDtypeStruct((bsz, s // tm, 1, LANES), F32)],
        compiler_params=_cparams(("parallel", "parallel")),
        name="merge",
    )(attn, four, gates, x, wba, wbf, wout, gt1, sh2, sc2, gffn, wr, br)


GRAN = 16
MOE_ROW_TILE = 1024


def _moe_metadata(cnt, tm):
    ntile = cnt.shape[0]

    def cumsum(a, axis):
        n = a.shape[axis]
        tri = jnp.asarray(np.triu(np.ones((n, n), np.float32)))
        af = a.astype(F32)
        out = (jnp.dot(af, tri, precision=lax.Precision.HIGHEST) if axis == a.ndim - 1
               else jnp.dot(tri.T, af, precision=lax.Precision.HIGHEST))
        return out.astype(jnp.int32)

    c = cnt[:, :N_EXPERTS].astype(jnp.int32)
    pc = (c + GRAN - 1) // GRAN * GRAN
    loff = cumsum(pc, 1) - pc
    tot = jnp.sum(pc, axis=0)
    tot_r = (tot + MOE_ROW_TILE - 1) // MOE_ROW_TILE * MOE_ROW_TILE
    ends = cumsum(tot_r[None, :], 1)[0]
    base = ends - tot_r
    goff = base[None, :] + cumsum(pc, 0) - pc
    cap_tiles = (2 * ntile * tm + ntile * N_EXPERTS * GRAN) // MOE_ROW_TILE + N_EXPERTS
    n_valid = (ends[-1] // MOE_ROW_TILE).astype(jnp.int32)
    tile_start = jnp.arange(cap_tiles, dtype=jnp.int32) * MOE_ROW_TILE
    tile_expert = jnp.minimum(jnp.sum((ends[None, :] <= tile_start[:, None]).astype(jnp.int32), axis=1),
                              N_EXPERTS - 1)
    flat = lambda a: a.reshape(-1).astype(jnp.int32)
    tail = tot_r - tot + jnp.where(jnp.arange(N_EXPERTS) == N_EXPERTS - 1,
                                   cap_tiles * MOE_ROW_TILE - ends[-1], 0)
    meta = dict(loff=flat(loff), goff=flat(goff), ngran=flat(pc // GRAN),
                ntot=jnp.concatenate([flat(jnp.sum(pc // (2 * GRAN), axis=1)),
                                      flat(jnp.sum(pc // GRAN % 2, axis=1))]),
                tail_start=flat(base + tot), tail_n=flat(tail // GRAN),
                tile_expert=tile_expert, n_valid=n_valid.reshape(1))
    return meta, loff.astype(F32), cap_tiles


def _run_copies(i, loff_s, goff_s, ngran_s, make_copy):
    def per_expert(e, carry):
        n = ngran_s[i * N_EXPERTS + e]
        lo = loff_s[i * N_EXPERTS + e]
        go = goff_s[i * N_EXPERTS + e]

        def per_pair(g, c):
            make_copy(pl.multiple_of(lo + g * 2 * GRAN, GRAN),
                      pl.multiple_of(go + g * 2 * GRAN, GRAN), 2).start()
            return c

        lax.fori_loop(0, n // 2, per_pair, 0)

        @pl.when(n % 2 == 1)
        def _():
            make_copy(pl.multiple_of(lo + (n - 1) * GRAN, GRAN),
                      pl.multiple_of(go + (n - 1) * GRAN, GRAN), 1).start(priority=1)

        return carry

    lax.fori_loop(0, N_EXPERTS, per_expert, 0)


def _wait_copies(n_pairs, n_singles, make_copy):
    def wait(k):
        def one(_, carry):
            make_copy(0, 0, k).wait()
            return carry
        return one
    lax.fori_loop(0, n_pairs, wait(2), 0)
    lax.fori_loop(0, n_singles, wait(1), 0)


def _local_positions(route, loffv, ltri):
    lane = lax.broadcasted_iota(jnp.int32, route.shape, 1).astype(F32)
    o1 = lane == route[:, 0:1]
    o2 = lane == route[:, 1:2]
    before = jnp.dot(ltri, (o1 | o2).astype(BF16), preferred_element_type=F32)
    start = before + loffv
    pos1 = jnp.sum(jnp.where(o1, start, 0.0), axis=-1, keepdims=True)
    pos2 = jnp.sum(jnp.where(o2, start, 0.0), axis=-1, keepdims=True)
    return pos1, pos2


def _dispatch_kernel(loff_s, goff_s, ngran_s, ntot_s, tstart_s, tn_s, v2_ref, route_ref, loffv_ref,
                     ltri_ref, xs_ref, posw_ref, loc_ref, zero_ref, sem):
    i = pl.program_id(0)
    last = pl.num_programs(0) - 1
    slot = i % 2
    rows = loc_ref.shape[1]
    route = route_ref[...]
    pos1, pos2 = _local_positions(route, loffv_ref[0], ltri_ref[...])
    lane = lax.broadcasted_iota(jnp.int32, route.shape, 1)
    posw = jnp.where(lane == 0, pos1, jnp.where(lane == 1, pos2, route))
    posw_ref[...] = posw
    pos_t = posw.T
    riota = lax.broadcasted_iota(jnp.int32, (rows, 1), 0).astype(F32)
    onehot = ((riota == pos_t[0:1]) | (riota == pos_t[1:2])).astype(BF16)
    loc_ref[slot] = jnp.dot(onehot, v2_ref[...], preferred_element_type=F32).astype(BF16)

    def out_copy(s):
        return lambda lo, go, k: pltpu.make_async_copy(
            loc_ref.at[s, pl.ds(lo, k * GRAN)], xs_ref.at[pl.ds(go, k * GRAN)], sem.at[s])

    _run_copies(i, loff_s, goff_s, ngran_s, out_copy(slot))
    ntile = pl.num_programs(0)

    @pl.when(i > 0)
    def _():
        _wait_copies(ntot_s[i - 1], ntot_s[ntile + i - 1], out_copy(1 - slot))

    @pl.when(i == last)
    def _():
        _wait_copies(ntot_s[i], ntot_s[ntile + i], out_copy(slot))
        zero_ref[...] = jnp.zeros(zero_ref.shape, BF16)

        def zero_copy(go):
            return pltpu.make_async_copy(zero_ref, xs_ref.at[pl.ds(go, GRAN)], sem.at[0])

        def per_expert(e, n_started):
            def per_granule(g, carry):
                zero_copy(pl.multiple_of(tstart_s[e] + g * GRAN, GRAN)).start()
                return carry
            lax.fori_loop(0, tn_s[e], per_granule, 0)
            return n_started + tn_s[e]

        nz = lax.fori_loop(0, N_EXPERTS, per_expert, 0)
        _wait_copies(0, nz, lambda lo, go, k: zero_copy(go))


def _dispatch(v2, route, meta, loffv, ltri, cap_tiles, tm, rows):
    t, d = v2.shape
    ntile = t // tm
    grid_spec = pltpu.PrefetchScalarGridSpec(
        num_scalar_prefetch=6,
        grid=(ntile,),
        in_specs=[pl.BlockSpec((tm, d), lambda i, *_: (i, 0)),
                  pl.BlockSpec((tm, LANES), lambda i, *_: (i, 0)),
                  pl.BlockSpec((1, 1, LANES), lambda i, *_: (i, 0, 0)),
                  pl.BlockSpec((tm, tm), lambda i, *_: (0, 0))],
        out_specs=[pl.BlockSpec(memory_space=pl.ANY),
                   pl.BlockSpec((tm, LANES), lambda i, *_: (i, 0))],
        scratch_shapes=[pltpu.VMEM((2, rows, d), BF16), pltpu.VMEM((GRAN, d), BF16),
                        pltpu.SemaphoreType.DMA((2,))],
    )
    return pl.pallas_call(
        _dispatch_kernel,
        grid_spec=grid_spec,
        out_shape=[jax.ShapeDtypeStruct((cap_tiles * MOE_ROW_TILE, d), BF16),
                   jax.ShapeDtypeStruct((t, LANES), F32)],
        compiler_params=_cparams(("arbitrary",)),
        name="dispatch",
    )(meta["loff"], meta["goff"], meta["ngran"], meta["ntot"], meta["tail_start"], meta["tail_n"],
      v2, route, loffv, ltri)


def _experts_kernel(texp_s, nvalid_s, x_ref, wgu_ref, wd_ref, y_ref):
    del texp_s

    @pl.when(pl.program_id(0) < nvalid_s[0])
    def _():
        hc = jnp.dot(x_ref[...], wgu_ref[0], preferred_element_type=F32)
        g = hc[:, :D_EXPERT]
        hid = (g * jax.nn.sigmoid(g) * hc[:, D_EXPERT:]).astype(BF16)
        y_ref[...] = jnp.dot(hid, wd_ref[0], preferred_element_type=F32).astype(BF16)

    @pl.when(pl.program_id(0) >= nvalid_s[0])
    def _():
        y_ref[...] = jnp.zeros(y_ref.shape, BF16)


def _experts(xs, wgu, wd, meta, cap_tiles):
    d = xs.shape[1]
    row = lambda j, texp, nv: (jnp.minimum(j, nv[0] - 1), 0)
    wsel = lambda j, texp, nv: (texp[jnp.minimum(j, nv[0] - 1)], 0, 0)
    grid_spec = pltpu.PrefetchScalarGridSpec(
        num_scalar_prefetch=2,
        grid=(cap_tiles,),
        in_specs=[pl.BlockSpec((MOE_ROW_TILE, d), row),
                  pl.BlockSpec((1, d, 2 * D_EXPERT), wsel),
                  pl.BlockSpec((1, D_EXPERT, d), wsel)],
        out_specs=pl.BlockSpec((MOE_ROW_TILE, d), lambda j, texp, nv: (j, 0)),
    )
    return pl.pallas_call(
        _experts_kernel,
        grid_spec=grid_spec,
        out_shape=jax.ShapeDtypeStruct(xs.shape, BF16),
        compiler_params=_cparams(("arbitrary",)),
        name="experts",
    )(meta["tile_expert"], meta["n_valid"], xs, wgu, wd)


def _combine_kernel(loff_s, goff_s, ngran_s, ntot_s, posw_ref, ys_ref, h1_ref, gt2_ref, shf_ref,
                    scf_ref, gfin_ref, o_ref, loc_ref, sem):
    i = pl.program_id(0)
    slot = i % 2
    rows = loc_ref.shape[1]

    def in_copy(s):
        return lambda lo, go, k: pltpu.make_async_copy(
            ys_ref.at[pl.ds(go, k * GRAN)], loc_ref.at[s, pl.ds(lo, k * GRAN)], sem.at[s])

    def fetch(tile, s):
        loc_ref[s] = jnp.zeros(loc_ref.shape[1:], BF16)
        _run_copies(tile, loff_s, goff_s, ngran_s, in_copy(s))

    @pl.when(i == 0)
    def _():
        fetch(0, 0)

    @pl.when(i + 1 < pl.num_programs(0))
    def _():
        fetch(i + 1, 1 - slot)

    _wait_copies(ntot_s[i], ntot_s[pl.num_programs(0) + i], in_copy(slot))

    posw = posw_ref[...]
    riota = lax.broadcasted_iota(jnp.int32, (1, rows), 1).astype(F32)
    qw = (jnp.where(riota == posw[:, 0:1], posw[:, 2:3], 0.0)
          + jnp.where(riota == posw[:, 1:2], posw[:, 3:4], 0.0)).astype(BF16)
    moe = jnp.dot(qw, loc_ref[slot], preferred_element_type=F32)
    h2 = h1_ref[...] + gt2_ref[0] * moe
    o_ref[...] = _rms(h2, gfin_ref[...]) * (1.0 + scf_ref[0]) + shf_ref[0]


def _combine(posw, ys, h1, gt2, shf, scf, gfin, meta, tm, rows, tiles_per_batch):
    t, d = h1.shape
    per_b = pl.BlockSpec((1, 1, d), lambda i, *_: (i // tiles_per_batch, 0, 0))
    grid_spec = pltpu.PrefetchScalarGridSpec(
        num_scalar_prefetch=4,
        grid=(t // tm,),
        in_specs=[pl.BlockSpec((tm, LANES), lambda i, *_: (i, 0)),
                  pl.BlockSpec(memory_space=pl.ANY),
                  pl.BlockSpec((tm, d), lambda i, *_: (i, 0)),
                  per_b, per_b, per_b, pl.BlockSpec((1, d), lambda i, *_: (0, 0))],
        out_specs=pl.BlockSpec((tm, d), lambda i, *_: (i, 0)),
        scratch_shapes=[pltpu.VMEM((2, rows, d), BF16), pltpu.SemaphoreType.DMA((2,))],
    )
    return pl.pallas_call(
        _combine_kernel,
        grid_spec=grid_spec,
        out_shape=jax.ShapeDtypeStruct((t, d), F32),
        compiler_params=_cparams(("arbitrary",)),
        name="combine",
    )(meta["loff"], meta["goff"], meta["ngran"], meta["ntot"], posw, ys, h1, gt2, shf, scf, gfin)


def _rot_cols(w):
    half = ROPE // 2
    return jnp.concatenate([-w[..., half:], w[..., :half]], axis=-1)


def _pad_cols(w, lo, width):
    return jnp.pad(w, [(0, 0)] * (w.ndim - 1) + [(lo, width - lo - w.shape[-1])])


def _prep_weights(w_in, w_uq, w_uk, w_uv):
    o2, o3 = Q_LORA + KV_LORA, Q_LORA + KV_LORA + ROPE
    w_kr = w_in[:, o2:o3]
    wcat = jnp.concatenate([
        w_in[:, :o2],
        _pad_cols(w_kr, NOPE, HEAD_PAD), _pad_cols(_rot_cols(w_kr), NOPE, HEAD_PAD),
        w_in[:, o3:]], axis=1).astype(BF16)
    wq_a = _pad_cols(w_uq, 0, HEAD_PAD)
    wq_b = _pad_cols(_rot_cols(w_uq[..., NOPE:]), NOPE, HEAD_PAD)
    r = w_uq.shape[0]
    wqt = jnp.concatenate([wq_a.reshape(r, -1), wq_b.reshape(r, -1)], axis=1).T.astype(BF16)
    wk = _pad_cols(w_uk, 0, HEAD_PAD).reshape(r, -1).astype(BF16)
    wvt = _pad_cols(w_uv, 0, V_ROWS).reshape(r, -1).T.astype(BF16)
    ones = np.zeros((N_HEADS, V_ROWS), np.float32)
    ones[:, V_DIM] = 1.0
    return wcat, wqt, wk, wvt, jnp.asarray(ones.reshape(-1, 1))


def kernel(x, c, positions, w_ada, b_ada, g_norm_mix, w_in, g_q_lat, g_kv_lat, w_uq, w_uk, w_uv, w_branch_attn, w_branch_fourier, b_gates, w_out, g_norm_ffn, w_router_group, b_router_group, w_router_expert, b_router_expert, w_expert_gate, w_expert_up, w_expert_down, w_ada_final, b_ada_final, g_norm_final):
    bsz, s, d = x.shape
    assert w_ada.shape[0] == 1 and s % FFT_N2 == 0
    tm = min(512, s)

    mod = _mods(c, w_ada[0], b_ada[0])
    fmod = _mods(c, w_ada_final, b_ada_final)
    sh1, sc1, gt1, sh2, sc2, gt2 = [m.reshape(bsz, 1, d) for m in jnp.split(mod, 6, axis=-1)]
    shf, scf = [m.reshape(bsz, 1, d) for m in jnp.split(fmod, 2, axis=-1)]

    wcat, wqt, wk, wvt, vones = _prep_weights(w_in[0], w_uq[0], w_uk[0], w_uv[0])
    bd, m1, wb = _dft_constants(s)
    inv_freq = ROPE_THETA ** (-jnp.arange(0, ROPE, 2, dtype=F32) / ROPE)
    invf = jnp.concatenate([inv_freq, inv_freq]).reshape(ROPE, 1)

    qt, k, vt, zr, zi, gates = _inproj(
        x, positions.reshape(bsz, 1, s), sh1, sc1, g_norm_mix[0][None, :], wcat,
        g_q_lat[0][None, :], g_kv_lat[0][None, :], wqt, wk, wvt, vones, bd, b_gates[0][None, :],
        invf, tm)

    attn = _attention(qt, k, vt, tq=min(256, s))
    br, bi = _fft_a(zr, zi, m1, ts2=8)
    four = _fft_b(br, bi, wb, tk1=16)

    wr = jnp.concatenate([w_router_expert[0].reshape(d, N_EXPERTS), w_router_group[0]], axis=1)
    wr = _pad_cols(wr, 0, LANES)
    wr_top = lax.bitcast_convert_type(
        lax.bitcast_convert_type(wr, jnp.uint32) & jnp.uint32(0xFFFF0000), F32)
    wr_hi = wr_top.astype(BF16)
    wr_lo = (wr - wr_top).astype(BF16)
    wr = jnp.concatenate([jnp.concatenate([wr_hi, wr_lo], axis=1),
                          jnp.concatenate([wr_hi, jnp.zeros_like(wr_lo)], axis=1)], axis=0)
    brt = _pad_cols(jnp.concatenate([b_router_expert[0].reshape(-1), b_router_group[0]])[None, :], 0, LANES)
    h1, v2, route, cnt = _merge(attn, four, gates, x, w_branch_attn[0].astype(BF16),
                                w_branch_fourier[0].astype(BF16), w_out[0].astype(BF16), gt1, sh2,
                                sc2, g_norm_ffn[0][None, :], wr, brt, tm)

    t = bsz * s
    meta, loffv, cap_tiles = _moe_metadata(cnt.reshape(-1, LANES), tm)
    rows = 2 * tm + N_EXPERTS * GRAN
    loffv = _pad_cols(loffv, 0, LANES).reshape(-1, 1, LANES)
    ltri = jnp.asarray(np.tril(np.ones((tm, tm), np.float32), -1), BF16)
    xs, posw = _dispatch(v2.reshape(t, d), route.reshape(t, LANES), meta, loffv, ltri, cap_tiles,
                         tm, rows)
    wgu = jnp.concatenate([w_expert_gate[0], w_expert_up[0]], axis=-1).astype(BF16)
    wd = w_expert_down[0].astype(BF16)
    ys = _experts(xs, wgu, wd, meta, cap_tiles)
    out = _combine(posw, ys, h1.reshape(t, d), gt2, shf, scf, g_norm_final[None, :], meta, tm, rows,
                   s // tm)
    return out.reshape(bsz, s, d)
```
